```python
import math
import jax, jax.numpy as jnp
from jax import lax
import numpy as np

D_MODEL = 1024
BATCH = 8
SEQ = 4096
DEPTH = 4
DEC_BATCH = 16
DEC_SEQ = 4096
PAST_LEN = 128

HEAD_DIM = 64
LRU_WIDTH = D_MODEL // 4
LRU_HEADS = LRU_WIDTH // HEAD_DIM
RET_WIDTH = 3 * D_MODEL // 8
RET_HEADS = RET_WIDTH // HEAD_DIM
DIFF_WIDTH = 3 * D_MODEL // 8
DIFF_HEADS = DIFF_WIDTH // HEAD_DIM
DIFF_HALF = HEAD_DIM // 2
MIX_WIDTH = LRU_WIDTH + RET_WIDTH + DIFF_WIDTH
IN_WIDTH = 2 * LRU_WIDTH + 4 * RET_WIDTH + 3 * DIFF_WIDTH
CONV_WIDTH = 4
LRU_C = 8.0
RET_CHUNK = 128
ROPE_BASE = 10000.0
Q_BLOCK = 128
NUM_BUCKETS = 32
MAX_DISTANCE = 128
N_EXPERTS = 16
EC_FACTOR = 2
D_FF = ((8 * D_MODEL // 3 + 127) // 128) * 128
EPS = 1e-6

kernel_name = 'hybrid_bidir_encoder_rglru_retention_diffattn_ec'


def rms_norm(x, g):
    xf = x.astype(jnp.float32)
    y = xf * lax.rsqrt(jnp.mean(xf * xf, axis=-1, keepdims=True) + EPS)
    return (y * g).astype(x.dtype)


def _lin_combine(c1, c2):
    a1, b1 = c1
    a2, b2 = c2
    return a1 * a2, a2 * b1 + b2


def rglru_dir(xc, wa, ba, wx, bx, lam, reverse):
    B, S, W = xc.shape
    xr = xc.reshape(B, S, LRU_HEADS, HEAD_DIM)
    r = jax.nn.sigmoid(jnp.einsum('bshd,hde->bshe', xr, wa).reshape(B, S, W) + ba)
    i = jax.nn.sigmoid(jnp.einsum('bshd,hde->bshe', xr, wx).reshape(B, S, W) + bx)
    log_a = -LRU_C * r * jax.nn.softplus(-lam)
    a = jnp.exp(log_a)
    b = jnp.sqrt(-jnp.expm1(2.0 * log_a)) * (i * xc)
    _, h = lax.associative_scan(_lin_combine, (a, b), axis=1, reverse=reverse)
    return h


def rglru_mixer(xb, gate, conv_w, conv_b, wa, ba, wx, bx, lam, norm_g):
    pad_lo = CONV_WIDTH // 2
    xc = lax.conv_general_dilated(
        xb, conv_w.astype(jnp.float32)[:, None, :], window_strides=(1,),
        padding=[(pad_lo, CONV_WIDTH - 1 - pad_lo)],
        dimension_numbers=('NWC', 'WIO', 'NWC'),
        feature_group_count=LRU_WIDTH) + conv_b
    h = (rglru_dir(xc, wa[0], ba[0], wx[0], bx[0], lam[0], False)
         + rglru_dir(xc, wa[1], ba[1], wx[1], bx[1], lam[1], True))
    return rms_norm(h * jax.nn.gelu(gate), norm_g)


def rope(x):
    S = x.shape[1]
    half = HEAD_DIM // 2
    freqs = ROPE_BASE ** (-jnp.arange(half, dtype=jnp.float32) / half)
    ang = jnp.arange(S, dtype=jnp.float32)[:, None] * freqs[None, :]
    cos = jnp.cos(ang)[None, :, None, :]
    sin = jnp.sin(ang)[None, :, None, :]
    x1, x2 = x[..., :half], x[..., half:]
    return jnp.concatenate([x1 * cos - x2 * sin, x1 * sin + x2 * cos], axis=-1)


def retention_scan(q, k, v, lg, include_diag):
    B, S, H, D = q.shape
    C = RET_CHUNK
    NC = S // C
    qc = q.reshape(B, NC, C, H, D)
    kc = k.reshape(B, NC, C, H, D)
    vc = v.reshape(B, NC, C, H, D)
    idx = jnp.arange(C, dtype=jnp.float32)
    diff = idx[:, None] - idx[None, :]
    mask = (diff >= 0) if include_diag else (diff > 0)
    intra_decay = jnp.where(mask[None], jnp.exp(jnp.maximum(diff, 0.0)[None] * lg[:, None, None]), 0.0)
    scores = jnp.einsum('bnihd,bnjhd->bnhij', qc, kc) * intra_decay
    intra = jnp.einsum('bnhij,bnjhe->bnihe', scores, vc)
    k_dec = jnp.exp((C - 1 - idx)[:, None] * lg[None, :])
    kv = jnp.einsum('bnjhd,bnjhe,jh->nbhde', kc, vc, k_dec)
    chunk_dec = jnp.exp(C * lg)[None, :, None, None]

    def step(R, kv_n):
        return R * chunk_dec + kv_n, R

    _, R_prev = lax.scan(step, jnp.zeros_like(kv[0]), kv)
    q_dec = jnp.exp((idx + 1.0)[:, None] * lg[None, :])
    cross = jnp.einsum('bnihd,nbhde->bnihe', qc, R_prev) * q_dec[:, :, None]
    return (intra + cross).reshape(B, S, H, D)


def retention_mixer(q, k, v, g, norm_g):
    B, S, _ = q.shape
    H = RET_HEADS
    q = rope(q.reshape(B, S, H, HEAD_DIM))
    k = rope(k.reshape(B, S, H, HEAD_DIM)) * (HEAD_DIM ** -0.5)
    v = v.reshape(B, S, H, HEAD_DIM)
    lg = jnp.log1p(-jnp.exp2(-5.0 - jnp.arange(H, dtype=jnp.float32)))
    fwd = retention_scan(q, k, v, lg, True)
    bwd = jnp.flip(retention_scan(jnp.flip(q, 1), jnp.flip(k, 1), jnp.flip(v, 1), lg, False), 1)
    o = rms_norm(fwd + bwd, norm_g.reshape(H, HEAD_DIM))
    return jax.nn.silu(g) * o.reshape(B, S, RET_WIDTH)


def t5_bucket(rel):
    nb = NUM_BUCKETS // 2
    max_exact = nb // 2
    n = jnp.abs(rel)
    nf = jnp.maximum(n, 1).astype(jnp.float32)
    large = max_exact + (jnp.log(nf / max_exact) / math.log(MAX_DISTANCE / max_exact)
                         * (nb - max_exact)).astype(jnp.int32)
    large = jnp.minimum(large, nb - 1)
    return jnp.where(rel > 0, nb, 0) + jnp.where(n < max_exact, n, large)


def diff_attn_mixer(q, k, v, rel_bias, q_g, k_g, lam_vecs, norm_g, lam_init):
    B, S, _ = q.shape
    H = DIFF_HEADS
    q = rms_norm(q.reshape(B, S, H, 2, DIFF_HALF), q_g)
    k = rms_norm(k.reshape(B, S, H, 2, DIFF_HALF), k_g)
    v = v.reshape(B, S, H, HEAD_DIM)
    lam = (jnp.exp(jnp.sum(lam_vecs[0] * lam_vecs[1])) - jnp.exp(jnp.sum(lam_vecs[2] * lam_vecs[3]))
           + lam_init).astype(jnp.float32)
    NQ = S // Q_BLOCK
    qb = q.reshape(B, NQ, Q_BLOCK, H, 2, DIFF_HALF).transpose(1, 0, 2, 3, 4, 5)
    kpos = jnp.arange(S, dtype=jnp.int32)
    scale = DIFF_HALF ** -0.5

    def block(args):
        q_blk, bi = args
        qpos = bi * Q_BLOCK + jnp.arange(Q_BLOCK, dtype=jnp.int32)
        bias = rel_bias[t5_bucket(kpos[None, :] - qpos[:, None])].transpose(2, 0, 1)
        s = jnp.einsum('bqhmd,bkhmd->bmhqk', q_blk, k) * scale + bias
        p = jax.nn.softmax(s.astype(jnp.float32), axis=-1)
        a = p[:, 0] - lam * p[:, 1]
        return jnp.einsum('bhqk,bkhd->bqhd', a, v)

    o = lax.map(block, (qb, jnp.arange(NQ, dtype=jnp.int32)))
    o = o.transpose(1, 0, 2, 3, 4).reshape(B, S, H, HEAD_DIM)
    o = rms_norm(o, norm_g) * (1.0 - lam_init)
    return o.reshape(B, S, DIFF_WIDTH)


def ec_moe(x, w_r, w_g, w_u, w_d):
    B, S, D = x.shape
    N = B * S
    C = EC_FACTOR * N // N_EXPERTS
    xf = x.reshape(N, D)
    probs = jax.nn.softmax(xf.astype(jnp.float32) @ w_r.astype(jnp.float32), axis=-1)
    gates, idx = lax.top_k(probs.T, C)
    xe = xf[idx]

    def expert(args):
        xi, wg, wu, wd = args
        return (jax.nn.silu(xi @ wg) * (xi @ wu)) @ wd

    ye = lax.map(expert, (xe, w_g, w_u, w_d))
    ye = ye * gates[..., None].astype(ye.dtype)
    y = jnp.zeros_like(xf).at[idx.reshape(-1)].add(ye.reshape(N_EXPERTS * C, D))
    return y.reshape(B, S, D)


def trunk(x, rel_bias, ln1_g, ln2_g, w_in, conv_w, conv_b, lru_wa, lru_ba, lru_wx, lru_bx,
          lru_lambda, lru_norm_g, ret_norm_g, q_norm_g, k_norm_g, diff_lambda, diff_norm_g,
          w_out, w_router, w_gate, w_up, w_down):
    sizes = [LRU_WIDTH, LRU_WIDTH, RET_WIDTH, RET_WIDTH, RET_WIDTH, RET_WIDTH,
             DIFF_WIDTH, DIFF_WIDTH, DIFF_WIDTH]
    splits = [int(s) for s in np.cumsum(sizes)[:-1]]
    for l in range(DEPTH):
        lam_init = 0.8 - 0.6 * math.exp(-0.3 * l)
        xn = rms_norm(x, ln1_g[l])
        u = (xn @ w_in[l]).astype(jnp.float32)
        lx, lgate, rq, rk, rv, rg, dq, dk, dv = jnp.split(u, splits, axis=-1)
        y_lru = rglru_mixer(lx, lgate, conv_w[l], conv_b[l], lru_wa[l], lru_ba[l],
                            lru_wx[l], lru_bx[l], lru_lambda[l], lru_norm_g[l])
        y_ret = retention_mixer(rq, rk, rv, rg, ret_norm_g[l])
        y_diff = diff_attn_mixer(dq, dk, dv, rel_bias, q_norm_g[l], k_norm_g[l],
                                 diff_lambda[l], diff_norm_g[l], lam_init)
        mix = jnp.concatenate([y_lru, y_ret, y_diff], axis=-1).astype(x.dtype)
        h = x + mix @ w_out[l]
        x = h + ec_moe(rms_norm(h, ln2_g[l]), w_router[l], w_gate[l], w_up[l], w_down[l])
    return x


def setup_inputs(seed: int = 0) -> dict:
    key = jax.random.key(seed)
    ks = jax.random.split(key, 26)
    f32 = jnp.float32
    L = DEPTH

    def nrm(k, shape, scale):
        return jax.random.normal(k, shape, f32) * scale

    def gain(k, shape):
        return 1.0 + 0.05 * jax.random.normal(k, shape, f32)

    a_c = jax.random.uniform(ks[12], (L, 2, LRU_WIDTH), f32, 0.9, 0.999)
    a = a_c ** (1.0 / LRU_C)
    return {
        'x_prompt': nrm(ks[0], (BATCH, SEQ, D_MODEL), 1.0),
        'x_sample': nrm(ks[1], (DEC_BATCH, DEC_SEQ, D_MODEL), 1.0),
        'rel_bias': nrm(ks[2], (NUM_BUCKETS, DIFF_HEADS), 0.3),
        'ln1_g': gain(ks[3], (L, D_MODEL)),
        'ln2_g': gain(ks[4], (L, D_MODEL)),
        'w_in': nrm(ks[5], (L, D_MODEL, IN_WIDTH), D_MODEL ** -0.5),
        'conv_w': nrm(ks[6], (L, CONV_WIDTH, LRU_WIDTH), CONV_WIDTH ** -0.5),
        'conv_b': nrm(ks[7], (L, LRU_WIDTH), 0.02),
        'lru_wa': nrm(ks[8], (L, 2, LRU_HEADS, HEAD_DIM, HEAD_DIM), HEAD_DIM ** -0.5),
        'lru_ba': nrm(ks[9], (L, 2, LRU_WIDTH), 0.02),
        'lru_wx': nrm(ks[10], (L, 2, LRU_HEADS, HEAD_DIM, HEAD_DIM), HEAD_DIM ** -0.5),
        'lru_bx': nrm(ks[11], (L, 2, LRU_WIDTH), 0.02),
        'lru_lambda': jnp.log(a) - jnp.log1p(-a),
        'lru_norm_g': gain(ks[13], (L, LRU_WIDTH)),
        'ret_norm_g': gain(ks[14], (L, RET_WIDTH)),
        'q_norm_g': gain(ks[15], (L, DIFF_HALF)),
        'k_norm_g': gain(ks[16], (L, DIFF_HALF)),
        'diff_lambda': nrm(ks[17], (L, 4, DIFF_HALF), 0.1),
        'diff_norm_g': gain(ks[18], (L, HEAD_DIM)),
        'w_out': nrm(ks[19], (L, MIX_WIDTH, D_MODEL), MIX_WIDTH ** -0.5),
        'w_router': nrm(ks[20], (L, D_MODEL, N_EXPERTS), D_MODEL ** -0.5),
        'w_gate': nrm(ks[21], (L, N_EXPERTS, D_MODEL, D_FF), D_MODEL ** -0.5),
        'w_up': nrm(ks[22], (L, N_EXPERTS, D_MODEL, D_FF), D_MODEL ** -0.5),
        'w_down': nrm(ks[23], (L, N_EXPERTS, D_FF, D_MODEL), D_FF ** -0.5),
    }


def reference(x_prompt, x_sample, rel_bias, ln1_g, ln2_g, w_in, conv_w, conv_b, lru_wa, lru_ba,
              lru_wx, lru_bx, lru_lambda, lru_norm_g, ret_norm_g, q_norm_g, k_norm_g,
              diff_lambda, diff_norm_g, w_out, w_router, w_gate, w_up, w_down):
    y_prompt = trunk(x_prompt, rel_bias, ln1_g, ln2_g, w_in, conv_w, conv_b, lru_wa, lru_ba,
                     lru_wx, lru_bx, lru_lambda, lru_norm_g, ret_norm_g, q_norm_g, k_norm_g,
                     diff_lambda, diff_norm_g, w_out, w_router, w_gate, w_up, w_down)
    y_sample = trunk(x_sample, rel_bias, ln1_g, ln2_g, w_in, conv_w, conv_b, lru_wa, lru_ba,
                     lru_wx, lru_bx, lru_lambda, lru_norm_g, ret_norm_g, q_norm_g, k_norm_g,
                     diff_lambda, diff_norm_g, w_out, w_router, w_gate, w_up, w_down)
    return (y_prompt, y_sample)
```

```python
import functools
import math

import numpy as np
import jax
import jax.numpy as jnp
from jax import lax
from jax.experimental import pallas as pl
from jax.experimental.pallas import tpu as pltpu

F32 = jnp.float32
BF16 = jnp.bfloat16
I32 = jnp.int32
HIGHEST = lax.Precision.HIGHEST

D_MODEL = 1024
HEAD_DIM = 64
LRU_W = 256
LRU_HEADS = 4
RET_W = 384
RET_HEADS = 6
DIFF_W = 384
DIFF_HEADS = 6
DIFF_HALF = 32
IN_SIZES = (LRU_W, LRU_W, RET_W, RET_W, RET_W, RET_W, DIFF_W, DIFF_W, DIFF_W)
IN_WIDTH = sum(IN_SIZES)
CONV_WIDTH = 4
LRU_C = 8.0
ROPE_BASE = 10000.0
NUM_BUCKETS = 32
MAX_DISTANCE = 128
N_EXPERTS = 16
EC_FACTOR = 2
D_FF = 2816
EPS = 1e-6

V7X_VMEM_BYTES = 64 * 1024 * 1024
SUBLANES = 8
LANES = 128

ROW_TILE = 512
SCAN_CHUNK = 256
RET_CHUNK = 256
ATTN_TILE = 256
SEL_ROW = 512
SLOT_BLOCK = 256
FF_CHUNK = 1408
COMB_TILE = 128
COMB_ALIGN = 16
COMB_WIN = COMB_TILE + COMB_ALIGN


def _params(sem, vmem_mb=48):
    return pltpu.CompilerParams(dimension_semantics=sem,
                                vmem_limit_bytes=vmem_mb * 1024 * 1024)


def _full(shape):
    nd = len(shape)
    return pl.BlockSpec(shape, lambda *_: (0,) * nd)


def _rms(x, g):
    return x * lax.rsqrt(jnp.mean(x * x, axis=-1, keepdims=True) + EPS) * g


def _in_proj_body(x_ref, g_ref, w_ref, *o_refs):
    xn = _rms(x_ref[...], g_ref[...]).astype(BF16)
    off = 0
    for o_ref, width in zip(o_refs, IN_SIZES):
        o_ref[...] = jnp.dot(xn, w_ref[:, off:off + width], preferred_element_type=F32)
        off += width


def in_proj(x2d, g, w_bf16, tm=ROW_TILE):
    n = x2d.shape[0]
    tm = min(tm, n)
    return pl.pallas_call(
        _in_proj_body,
        grid=(n // tm,),
        in_specs=[pl.BlockSpec((tm, D_MODEL), lambda i: (i, 0)),
                  _full((1, D_MODEL)), _full((D_MODEL, IN_WIDTH))],
        out_specs=[pl.BlockSpec((tm, w), lambda i: (i, 0)) for w in IN_SIZES],
        out_shape=[jax.ShapeDtypeStruct((n, w), F32) for w in IN_SIZES],
        compiler_params=_params(("parallel",)),
        name="in_proj",
    )(x2d, g.reshape(1, D_MODEL), w_bf16)


def _shift_rows(ext, s, tc):
    n = ext.shape[0]
    return pltpu.roll(ext, (-s) % n, axis=0)[SUBLANES:SUBLANES + tc]


def _neg_expm1(y):
    series = -y * (1.0 + y * (1.0 / 2) * (1.0 + y * (1.0 / 3) * (1.0 + y * (1.0 / 4) * (1.0 + y * (1.0 / 5)))))
    return jnp.where(y > -1.0 / 64, series, 1.0 - jnp.exp(y))


def _lru_scan(a, b, rev):
    tc = a.shape[0]
    t = lax.broadcasted_iota(I32, a.shape, 0)
    d = 1
    while d < tc:
        if rev:
            keep = t < tc - d
            a_o = pltpu.roll(a, tc - d, axis=0)
            b_o = pltpu.roll(b, tc - d, axis=0)
        else:
            keep = t >= d
            a_o = pltpu.roll(a, d, axis=0)
            b_o = pltpu.roll(b, d, axis=0)
        b = jnp.where(keep, a * b_o + b, b)
        a = jnp.where(keep, a * a_o, a)
        d *= 2
    return a, b


def _lru_body(rev, *refs):
    if rev:
        (x_ref, xp_ref, xn_ref, gate_ref, hf_ref, cw_ref, cb_ref, w_ref, b_ref, c_ref,
         ng_ref, o_ref, carry_ref) = refs
    else:
        (x_ref, xp_ref, xn_ref, cw_ref, cb_ref, w_ref, b_ref, c_ref, o_ref, carry_ref) = refs
    step = pl.program_id(1)
    nc = pl.num_programs(1)
    ci = nc - 1 - step if rev else step

    @pl.when(step == 0)
    def _():
        carry_ref[...] = jnp.zeros_like(carry_ref)

    x = x_ref[0]
    tc = x.shape[0]
    prev = xp_ref[0] * (ci > 0).astype(F32)
    nxt = xn_ref[0] * (ci < nc - 1).astype(F32)
    ext = jnp.concatenate([prev, x, nxt], axis=0)
    xc = cb_ref[...] + sum(cw_ref[j:j + 1, :] * _shift_rows(ext, j - CONV_WIDTH // 2, tc)
                           for j in range(CONV_WIDTH))
    z = jnp.dot(xc.astype(BF16), w_ref[...], preferred_element_type=F32) + b_ref[...]
    r = jax.nn.sigmoid(z[:, :LRU_W])
    i = jax.nn.sigmoid(z[:, LRU_W:])
    log_a = c_ref[...] * r
    a = jnp.exp(log_a)
    b = jnp.sqrt(_neg_expm1(2.0 * log_a)) * (i * xc)
    a_cum, h_loc = _lru_scan(a, b, rev)
    h = h_loc + a_cum * carry_ref[0:1, :]
    carry_ref[0:1, :] = h[0:1, :] if rev else h[tc - 1:tc, :]
    if rev:
        y = (hf_ref[0] + h) * jax.nn.gelu(gate_ref[0])
        o_ref[0] = _rms(y, ng_ref[...])
    else:
        o_ref[0] = h


def lru_mixer(lx, lgate, cw, cb, wcat, bcat, cdec, ng, tc=SCAN_CHUNK):
    bsz, s, _ = lx.shape
    tc = min(tc, s)
    nc = s // tc
    r8 = tc // SUBLANES
    nb8 = s // SUBLANES

    def specs(rev):
        cmap = (lambda b, c: (b, nc - 1 - c, 0)) if rev else (lambda b, c: (b, c, 0))
        if rev:
            pmap = lambda b, c: (b, jnp.maximum((nc - 1 - c) * r8 - 1, 0), 0)
            nmap = lambda b, c: (b, jnp.minimum((nc - c) * r8, nb8 - 1), 0)
        else:
            pmap = lambda b, c: (b, jnp.maximum(c * r8 - 1, 0), 0)
            nmap = lambda b, c: (b, jnp.minimum((c + 1) * r8, nb8 - 1), 0)
        main = pl.BlockSpec((1, tc, LRU_W), cmap)
        halo = [pl.BlockSpec((1, SUBLANES, LRU_W), pmap), pl.BlockSpec((1, SUBLANES, LRU_W), nmap)]
        return main, halo

    common = [_full((CONV_WIDTH, LRU_W)), _full((1, LRU_W)), _full((LRU_W, 2 * LRU_W)),
              _full((1, 2 * LRU_W)), _full((1, LRU_W))]
    main, halo = specs(False)
    hf = pl.pallas_call(
        functools.partial(_lru_body, False),
        grid=(bsz, nc),
        in_specs=[main] + halo + common,
        out_specs=main,
        out_shape=jax.ShapeDtypeStruct((bsz, s, LRU_W), F32),
        scratch_shapes=[pltpu.VMEM((SUBLANES, LRU_W), F32)],
        compiler_params=_params(("parallel", "arbitrary")),
        name="lru_fwd",
    )(lx, lx, lx, cw, cb, wcat[0], bcat[0], cdec[0])
    main, halo = specs(True)
    return pl.pallas_call(
        functools.partial(_lru_body, True),
        grid=(bsz, nc),
        in_specs=[main] + halo + [main, main] + common + [_full((1, LRU_W))],
        out_specs=main,
        out_shape=jax.ShapeDtypeStruct((bsz, s, LRU_W), F32),
        scratch_shapes=[pltpu.VMEM((SUBLANES, LRU_W), F32)],
        compiler_params=_params(("parallel", "arbitrary")),
        name="lru_rev",
    )(lx, lx, lx, lgate, hf, cw, cb, wcat[1], bcat[1], cdec[1], ng)


def _ret_log_gamma():
    return np.log1p(-np.exp2(-5.0 - np.arange(RET_HEADS, dtype=np.float64)))


@functools.lru_cache(maxsize=None)
def _ret_tables(c):
    lg = np.repeat(_ret_log_gamma(), HEAD_DIM)[None, :]
    idx = np.arange(c, dtype=np.float64)[:, None]
    dec = np.stack([np.exp((idx + 1.0) * lg),
                    np.exp((c - 1.0 - idx) * lg),
                    np.exp((c - idx) * lg),
                    np.exp(idx * lg)])
    chunk = np.exp(c * lg)
    dist = np.abs(idx - idx.T)
    intra = np.exp(dist[None] * _ret_log_gamma()[:, None, None])
    lane_head = np.arange(RET_W) // HEAD_DIM
    hmask = (lane_head[None, :] == np.arange(RET_HEADS)[:, None]).astype(np.float32)[:, None, :]
    bd = (lane_head[:, None] == lane_head[None, :]).astype(np.float32)
    return (dec.astype(np.float32), chunk.astype(np.float32), intra.astype(np.float32), hmask, bd)


def _rope_tables(s):
    half = HEAD_DIM // 2
    freqs = ROPE_BASE ** (-jnp.arange(half, dtype=F32) / half)
    ang = jnp.arange(s, dtype=F32)[:, None] * freqs[None, :]
    cos = jnp.cos(ang)
    sin = jnp.sin(ang)
    cos_t = jnp.tile(jnp.concatenate([cos, cos], axis=1), (1, RET_W // HEAD_DIM))
    sin_t = jnp.tile(jnp.concatenate([-sin, sin], axis=1), (1, RET_W // HEAD_DIM))
    return cos_t, sin_t


def _rope(x, cos, sin_signed):
    lane = lax.broadcasted_iota(I32, x.shape, 1)
    w = x.shape[1]
    half = HEAD_DIM // 2
    swapped = jnp.where(lane % HEAD_DIM < half,
                        pltpu.roll(x, w - half, axis=1), pltpu.roll(x, half, axis=1))
    return x * cos + swapped * sin_signed


def _ret_body(rev, *refs):
    if rev:
        (q_ref, k_ref, v_ref, cos_ref, sin_ref, dec_ref, chunk_ref, bd_ref,
         of_ref, g_ref, ng_ref, o_ref, state_ref) = refs
    else:
        (q_ref, k_ref, v_ref, cos_ref, sin_ref, dec_ref, chunk_ref, bd_ref,
         intra_ref, hmask_ref, o_ref, state_ref) = refs

    @pl.when(pl.program_id(1) == 0)
    def _():
        state_ref[...] = jnp.zeros_like(state_ref)

    cos = cos_ref[...]
    sin = sin_ref[...]
    q = _rope(q_ref[0], cos, sin)
    k = _rope(k_ref[0], cos, sin) * (HEAD_DIM ** -0.5)
    vb = v_ref[0].astype(BF16)
    qd, kd = (2, 3) if rev else (0, 1)
    state = state_ref[...]
    cross = jnp.dot((q * dec_ref[qd]).astype(BF16), state.astype(BF16), preferred_element_type=F32)
    kv = lax.dot_general((k * dec_ref[kd]).astype(BF16), vb, (((0,), (0,)), ((), ())),
                         preferred_element_type=F32)
    state_ref[...] = state * chunk_ref[...] + kv * bd_ref[...]
    if rev:
        o = of_ref[0] + cross
        ms = jnp.dot(o * o, bd_ref[...], precision=HIGHEST, preferred_element_type=F32) * (1.0 / HEAD_DIM)
        o = o * lax.rsqrt(ms + EPS) * ng_ref[...]
        o_ref[0] = jax.nn.silu(g_ref[0]) * o
    else:
        kb = k.astype(BF16)
        out = cross
        for h in range(RET_HEADS):
            hm = hmask_ref[h]
            s = lax.dot_general((q * hm).astype(BF16), kb, (((1,), (1,)), ((), ())),
                                preferred_element_type=F32)
            s = (s * intra_ref[h]).astype(BF16)
            out = out + jnp.dot(s, (v_ref[0] * hm).astype(BF16), preferred_element_type=F32)
        o_ref[0] = out


def ret_mixer(rq, rk, rv, rg, ng, c=RET_CHUNK):
    bsz, s, _ = rq.shape
    c = min(c, s)
    nc = s // c
    dec, chunk, intra, hmask, bd = _ret_tables(c)
    cos_t, sin_t = _rope_tables(s)

    def specs(rev):
        cmap = (lambda b, i: (b, nc - 1 - i, 0)) if rev else (lambda b, i: (b, i, 0))
        tmap = (lambda b, i: (nc - 1 - i, 0)) if rev else (lambda b, i: (i, 0))
        main = pl.BlockSpec((1, c, RET_W), cmap)
        tab = pl.BlockSpec((c, RET_W), tmap)
        return main, tab

    consts = [_full((4, c, RET_W)), _full((1, RET_W)), _full((RET_W, RET_W))]
    main, tab = specs(False)
    of = pl.pallas_call(
        functools.partial(_ret_body, False),
        grid=(bsz, nc),
        in_specs=[main, main, main, tab, tab] + consts + [_full((RET_HEADS, c, c)), _full((RET_HEADS, 1, RET_W))],
        out_specs=main,
        out_shape=jax.ShapeDtypeStruct((bsz, s, RET_W), F32),
        scratch_shapes=[pltpu.VMEM((RET_W, RET_W), F32)],
        compiler_params=_params(("parallel", "arbitrary")),
        name="ret_fwd",
    )(rq, rk, rv, cos_t, sin_t, dec, chunk, bd, intra, hmask)
    main, tab = specs(True)
    return pl.pallas_call(
        functools.partial(_ret_body, True),
        grid=(bsz, nc),
        in_specs=[main, main, main, tab, tab] + consts + [main, main, _full((1, RET_W))],
        out_specs=main,
        out_shape=jax.ShapeDtypeStruct((bsz, s, RET_W), F32),
        scratch_shapes=[pltpu.VMEM((RET_W, RET_W), F32)],
        compiler_params=_params(("parallel", "arbitrary")),
        name="ret_rev",
    )(rq, rk, rv, cos_t, sin_t, dec, chunk, bd, of, rg, ng.reshape(1, RET_W))


HEAD_SLOT = 128


@functools.lru_cache(maxsize=None)
def _attn_consts():
    lane = np.arange(DIFF_W)
    grp = lane // DIFF_HALF
    bd32 = (grp[:, None] == grp[None, :]).astype(np.float32) / DIFF_HALF
    head = lane // HEAD_DIM
    inner = lane % HEAD_DIM
    dst = head * HEAD_SLOT + inner
    place = np.zeros((3, DIFF_W, DIFF_HEADS * HEAD_SLOT), np.float32)
    place[2, lane, dst] = 1.0
    first = inner < DIFF_HALF
    place[0, lane[first], dst[first]] = 1.0
    place[1, lane[~first], dst[~first]] = 1.0
    ones_col = np.zeros((1, DIFF_HEADS * HEAD_SLOT), np.float32)
    ones_col[0, np.arange(DIFF_HEADS) * HEAD_SLOT + HEAD_DIM] = 1.0
    return bd32, place, ones_col


def _attn_prep_body(q_ref, k_ref, v_ref, qg_ref, kg_ref, bd_ref, place_ref, ones_ref,
                    q0_ref, q1_ref, kk_ref, vv_ref):
    def qk_norm(x, g):
        ms = jnp.dot(x * x, bd_ref[...], precision=HIGHEST, preferred_element_type=F32)
        return x * lax.rsqrt(ms + EPS) * g

    qh = (qk_norm(q_ref[0], qg_ref[...]) * (DIFF_HALF ** -0.5)).astype(BF16)
    kh = qk_norm(k_ref[0], kg_ref[...]).astype(BF16)
    vb = v_ref[0].astype(BF16)
    q0_ref[0] = jnp.dot(qh, place_ref[0], preferred_element_type=F32).astype(BF16)
    q1_ref[0] = jnp.dot(qh, place_ref[1], preferred_element_type=F32).astype(BF16)
    kk_ref[0] = jnp.dot(kh, place_ref[2], preferred_element_type=F32).astype(BF16)
    vv_ref[0] = (jnp.dot(vb, place_ref[2], preferred_element_type=F32) + ones_ref[...]).astype(BF16)


def attn_prep(dq, dk, dv, qg, kg, tc=ROW_TILE):
    bsz, s, _ = dq.shape
    tc = min(tc, s)
    bd32, place, ones_col = _attn_consts()
    wide = DIFF_HEADS * HEAD_SLOT
    main = pl.BlockSpec((1, tc, DIFF_W), lambda b, c: (b, c, 0))
    outb = pl.BlockSpec((1, tc, wide), lambda b, c: (b, c, 0))
    rep = DIFF_W // DIFF_HALF
    return pl.pallas_call(
        _attn_prep_body,
        grid=(bsz, s // tc),
        in_specs=[main, main, main, _full((1, DIFF_W)), _full((1, DIFF_W)),
                  _full((DIFF_W, DIFF_W)), _full((3, DIFF_W, wide)), _full((1, wide))],
        out_specs=[outb] * 4,
        out_shape=[jax.ShapeDtypeStruct((bsz, s, wide), BF16)] * 4,
        compiler_params=_params(("parallel", "parallel")),
        name="attn_prep",
    )(dq, dk, dv, jnp.tile(qg, rep).reshape(1, DIFF_W), jnp.tile(kg, rep).reshape(1, DIFF_W),
      bd32, jnp.asarray(place, BF16), ones_col)


@functools.lru_cache(maxsize=None)
def _bucket_tiles(t):
    nb = NUM_BUCKETS // 2
    max_exact = nb // 2
    r = np.arange(t)[:, None]
    c = np.arange(t)[None, :]
    rel = np.stack([c - r + d * t for d in (-1, 0, 1)])
    n = np.abs(rel)
    nf = np.maximum(n, 1).astype(np.float64)
    large = max_exact + np.floor(2.0 * np.log2(nf / max_exact)).astype(np.int64)
    large = np.minimum(large, nb - 1)
    return (np.where(rel > 0, nb, 0) + np.where(n < max_exact, n, large)).astype(np.int32)


def _attn_body(q0_ref, q1_ref, k_ref, v_ref, bias_ref, lam_ref, linit_ref, g_ref, o_ref):
    t = q0_ref.shape[1]
    nk = k_ref.shape[1] // t
    qi = pl.program_id(2)
    lam = (jnp.exp(jnp.sum(lam_ref[0:1, :] * lam_ref[1:2, :], axis=1, keepdims=True))
           - jnp.exp(jnp.sum(lam_ref[2:3, :] * lam_ref[3:4, :], axis=1, keepdims=True))
           + linit_ref[...])
    lane = lax.broadcasted_iota(I32, (t, HEAD_SLOT), 1)

    def softmax_pv(q, hh):
        lo = hh * HEAD_SLOT

        def body(j, carry):
            m_i, acc = carry
            rows = pl.ds(pl.multiple_of(j * t, t), t)
            kb = k_ref[0, rows, lo:lo + HEAD_SLOT]
            vb = v_ref[0, rows, lo:lo + HEAD_SLOT]
            s = lax.dot_general(q, kb, (((1,), (1,)), ((), ())), preferred_element_type=F32)
            s = s + bias_ref[hh, jnp.clip(j - qi, -2, 2) + 2]
            m_new = jnp.maximum(m_i, jnp.max(s, axis=1, keepdims=True))
            p = jnp.exp(s - m_new)
            acc = jnp.exp(m_i - m_new) * acc + jnp.dot(p.astype(BF16), vb, preferred_element_type=F32)
            return m_new, acc

        _, acc = lax.fori_loop(0, nk, body, (jnp.full((t, 1), -1e30, F32),
                                             jnp.zeros((t, HEAD_SLOT), F32)))
        return acc / acc[:, HEAD_DIM:HEAD_DIM + 1]

    outs = []
    for hh in range(2):
        lo = hh * HEAD_SLOT
        o = softmax_pv(q0_ref[0, :, lo:lo + HEAD_SLOT], hh) - lam * softmax_pv(q1_ref[0, :, lo:lo + HEAD_SLOT], hh)
        o = jnp.where(lane < HEAD_DIM, o, 0.0)
        ms = jnp.sum(o * o, axis=1, keepdims=True) * (1.0 / HEAD_DIM)
        outs.append(o * lax.rsqrt(ms + EPS))
    both = jnp.where(lane < HEAD_DIM, outs[0], pltpu.roll(outs[1], HEAD_DIM, axis=1))
    o_ref[0] = both * g_ref[...] * (1.0 - linit_ref[...])


def diff_attn(q0, q1, kk, vv, rel_bias, lam_vecs, lam_init, ng, t=ATTN_TILE):
    bsz, s, _ = q0.shape
    t = min(t, s)
    buckets = _bucket_tiles(t)
    near = jnp.transpose(rel_bias[buckets], (3, 0, 1, 2))
    nb = NUM_BUCKETS // 2
    left = jnp.broadcast_to(rel_bias[nb - 1][:, None, None, None], (DIFF_HEADS, 1, t, t))
    right = jnp.broadcast_to(rel_bias[NUM_BUCKETS - 1][:, None, None, None], (DIFF_HEADS, 1, t, t))
    bias = jnp.concatenate([left, near, right], axis=1)
    pair = 2 * HEAD_SLOT
    qspec = pl.BlockSpec((1, t, pair), lambda b, h, i: (b, i, h))
    kspec = pl.BlockSpec((1, s, pair), lambda b, h, i: (b, 0, h))
    lam_pad = jnp.zeros((4, LANES), F32).at[:, :DIFF_HALF].set(lam_vecs)
    linit = jnp.full((1, LANES), lam_init, F32)
    return pl.pallas_call(
        _attn_body,
        grid=(bsz, DIFF_HEADS // 2, s // t),
        in_specs=[qspec, qspec, kspec, kspec,
                  pl.BlockSpec((2, 5, t, t), lambda b, h, i: (h, 0, 0, 0)),
                  _full((4, LANES)), _full((1, LANES)), _full((1, LANES))],
        out_specs=pl.BlockSpec((1, t, 2 * HEAD_DIM), lambda b, h, i: (b, i, h)),
        out_shape=jax.ShapeDtypeStruct((bsz, s, DIFF_W), F32),
        compiler_params=_params(("parallel", "parallel", "arbitrary")),
        name="attn",
    )(q0, q1, kk, vv, bias, lam_pad, linit, jnp.tile(ng, 2).reshape(1, LANES))


def _out_proj_body(x_ref, yl_ref, yr_ref, yd_ref, wl_ref, wr_ref, wd_ref, g_ref, wrt_ref, wrtt_ref,
                   h_ref, hn_ref, p_ref, pt_ref):
    h = (x_ref[...]
         + jnp.dot(yl_ref[...].astype(BF16), wl_ref[...], preferred_element_type=F32)
         + jnp.dot(yr_ref[...].astype(BF16), wr_ref[...], preferred_element_type=F32)
         + jnp.dot(yd_ref[...].astype(BF16), wd_ref[...], preferred_element_type=F32))
    h_ref[...] = h
    hn = _rms(h, g_ref[...])
    hn_ref[...] = hn.astype(BF16)
    logits = jnp.dot(hn, wrt_ref[...], precision=HIGHEST, preferred_element_type=F32)
    e = jnp.exp(logits - jnp.max(logits, axis=1, keepdims=True))
    p_ref[...] = e / jnp.sum(e, axis=1, keepdims=True)
    logits_t = lax.dot_general(wrtt_ref[...], hn, (((1,), (1,)), ((), ())), precision=HIGHEST,
                               preferred_element_type=F32)
    et = jnp.exp(logits_t - jnp.max(logits_t, axis=0, keepdims=True))
    pt_ref[...] = et / jnp.sum(et, axis=0, keepdims=True)


def out_proj(x2d, yl, yr, yd, w_out_bf16, g, w_router, tm=ROW_TILE):
    n = x2d.shape[0]
    tm = min(tm, n)
    row = lambda w: pl.BlockSpec((tm, w), lambda i: (i, 0))
    return pl.pallas_call(
        _out_proj_body,
        grid=(n // tm,),
        in_specs=[row(D_MODEL), row(LRU_W), row(RET_W), row(DIFF_W),
                  _full((LRU_W, D_MODEL)), _full((RET_W, D_MODEL)), _full((DIFF_W, D_MODEL)),
                  _full((1, D_MODEL)), _full((D_MODEL, N_EXPERTS)), _full((N_EXPERTS, D_MODEL))],
        out_specs=[row(D_MODEL), row(D_MODEL), row(N_EXPERTS),
                   pl.BlockSpec((N_EXPERTS, tm), lambda i: (0, i))],
        out_shape=[jax.ShapeDtypeStruct((n, D_MODEL), F32), jax.ShapeDtypeStruct((n, D_MODEL), BF16),
                   jax.ShapeDtypeStruct((n, N_EXPERTS), F32), jax.ShapeDtypeStruct((N_EXPERTS, n), F32)],
        compiler_params=_params(("parallel",)),
        name="out_proj",
    )(x2d, yl, yr, yd, w_out_bf16[:LRU_W], w_out_bf16[LRU_W:LRU_W + RET_W], w_out_bf16[LRU_W + RET_W:],
      g.reshape(1, D_MODEL), w_router, w_router.T)


def _row_cumsum(x01, tri_ref, nr):
    within = jnp.dot(x01.astype(F32).astype(BF16), tri_ref[...], preferred_element_type=F32).astype(I32)
    tot = jnp.broadcast_to(within[:, SEL_ROW - 1:SEL_ROW], (nr, LANES))
    r = lax.broadcasted_iota(I32, (nr, LANES), 0)
    inc = tot
    d = 1
    while d < nr:
        inc = inc + jnp.where(r >= d, pltpu.roll(inc, d, axis=0), 0)
        d *= 2
    return within + (inc - tot)[:, 0:1]


def _select_body(cap, p_ref, tri_ref, cnt_ref, sel_ref):
    p = p_ref[0]
    nr = p.shape[0]
    bits = pltpu.bitcast(p, I32)

    def body(i, prefix):
        cand = prefix | (jnp.int32(1) << (30 - i))
        cnt = jnp.sum((bits >= cand).astype(I32), keepdims=True)
        return jnp.where(cnt >= cap, cand, prefix)

    thr = lax.fori_loop(0, 31, body, jnp.zeros((1, 1), I32))
    gt = bits > thr
    eq = bits == thr
    need = cap - jnp.sum(gt.astype(I32), keepdims=True)
    eq01 = eq.astype(I32)
    rank_eq = _row_cumsum(eq01, tri_ref, nr) - eq01
    sel = jnp.where(gt, 1, jnp.where(eq & (rank_eq < need), 1, 0))
    sel_ref[0] = sel
    cnt_ref[0] = _row_cumsum(sel, tri_ref, nr)


def select(probs_t, cap):
    e, n = probs_t.shape
    nr = n // SEL_ROW
    tri = np.triu(np.ones((SEL_ROW, SEL_ROW), np.float32))
    blk = pl.BlockSpec((1, nr, SEL_ROW), lambda i: (i, 0, 0))
    return pl.pallas_call(
        functools.partial(_select_body, cap),
        grid=(e,),
        in_specs=[blk, _full((SEL_ROW, SEL_ROW))],
        out_specs=[blk, blk],
        out_shape=[jax.ShapeDtypeStruct((e, nr, SEL_ROW), I32)] * 2,
        compiler_params=_params(("parallel",)),
        name="select",
    )(probs_t.reshape(e, nr, SEL_ROW), jnp.asarray(tri, BF16))


def _ffn_body(sb_ref, row_ref, flag_ref, pos_ref, x_ref, wg_ref, wu_ref, wd_ref, o_ref, xe_ref):
    e = pl.program_id(0)
    s = pl.program_id(1)
    flags = flag_ref[e, s]
    w = xe_ref.shape[0]

    @pl.when((flags & 2) != 0)
    def _():
        xe_ref[...] = jnp.zeros_like(xe_ref)

    @pl.when((flags & 1) != 0)
    def _():
        slot = sb_ref[e, s] * w + lax.broadcasted_iota(I32, (w, 1), 0)
        onehot = jnp.where(pos_ref[0, 0] == slot, 1.0, 0.0).astype(BF16)
        xe_ref[...] += jnp.dot(onehot, x_ref[...], preferred_element_type=F32)

    @pl.when((flags & 4) != 0)
    def _():
        xe = xe_ref[...].astype(BF16)
        acc = jnp.zeros((w, D_MODEL), F32)
        for c0 in range(0, D_FF, FF_CHUNK):
            g = jnp.dot(xe, wg_ref[0, :, c0:c0 + FF_CHUNK], preferred_element_type=F32)
            u = jnp.dot(xe, wu_ref[0, :, c0:c0 + FF_CHUNK], preferred_element_type=F32)
            mid = (jax.nn.silu(g) * u).astype(BF16)
            acc = acc + jnp.dot(mid, wd_ref[0, c0:c0 + FF_CHUNK, :], preferred_element_type=F32)
        o_ref[0] = acc.astype(BF16)


def _ffn_schedule(row_end, cap, w):
    e, nr = row_end.shape
    n_sb = cap // w
    edges = jnp.broadcast_to(jnp.arange(1, n_sb + 1, dtype=I32) * w, (e, n_sb))
    stops = jnp.sort(jnp.concatenate([edges, row_end], axis=1), axis=1)
    starts = jnp.concatenate([jnp.zeros((e, 1), I32), stops[:, :-1]], axis=1)
    valid = stops > starts
    sb = jnp.minimum(starts // w, n_sb - 1)
    row = jnp.minimum(jnp.sum(row_end[:, None, :] <= starts[:, :, None], axis=2), nr - 1)
    flags = (valid.astype(I32) + 2 * (valid & (starts % w == 0)).astype(I32)
             + 4 * (valid & (stops % w == 0)).astype(I32))
    return sb.astype(I32), row.astype(I32), flags


def expert_ffn(hn_bf16, posm, row_end, wg, wu, wd, cap, w=SLOT_BLOCK):
    n = hn_bf16.shape[0]
    e, nr, _ = posm.shape
    w = min(w, cap)
    sb, row, flags = _ffn_schedule(row_end, cap, w)
    steps = sb.shape[1]
    grid_spec = pltpu.PrefetchScalarGridSpec(
        num_scalar_prefetch=3,
        grid=(e, steps),
        in_specs=[
            pl.BlockSpec((1, 1, 1, SEL_ROW), lambda i, s, sb, row, fl: (i, row[i, s], 0, 0)),
            pl.BlockSpec((SEL_ROW, D_MODEL), lambda i, s, sb, row, fl: (row[i, s], 0)),
            pl.BlockSpec((1, D_MODEL, D_FF), lambda i, s, sb, row, fl: (i, 0, 0)),
            pl.BlockSpec((1, D_MODEL, D_FF), lambda i, s, sb, row, fl: (i, 0, 0)),
            pl.BlockSpec((1, D_FF, D_MODEL), lambda i, s, sb, row, fl: (i, 0, 0)),
        ],
        out_specs=pl.BlockSpec((1, w, D_MODEL), lambda i, s, sb, row, fl: (i, sb[i, s], 0)),
        scratch_shapes=[pltpu.VMEM((w, D_MODEL), F32)],
    )
    return pl.pallas_call(
        _ffn_body,
        grid_spec=grid_spec,
        out_shape=jax.ShapeDtypeStruct((e, cap, D_MODEL), BF16),
        compiler_params=_params(("arbitrary", "arbitrary"), vmem_mb=58),
        name="ffn",
    )(sb, row, flags, posm.reshape(e, nr, 1, SEL_ROW), hn_bf16, wg, wu, wd)


def _combine_body(ws_ref, h_ref, pos_ref, gate_ref, *refs):
    ye_refs = refs[:N_EXPERTS]
    o_ref = refs[N_EXPERTS]
    i = pl.program_id(0)
    acc = h_ref[...]
    pos = pos_ref[...]
    gate = gate_ref[...]
    lane = lax.broadcasted_iota(I32, (1, COMB_WIN), 1)
    for e in range(N_EXPERTS):
        slots = ws_ref[e, i] * COMB_ALIGN + lane
        g = jnp.where(pos[:, e:e + 1] == slots, gate[:, e:e + 1], 0.0).astype(BF16)
        acc = acc + jnp.dot(g, ye_refs[e][...], preferred_element_type=F32)
    o_ref[...] = acc


def combine(h2d, pos_t, gates, ye, win_start):
    n = h2d.shape[0]
    t = COMB_TILE
    row = lambda w: pl.BlockSpec((t, w), lambda i, ws: (i, 0))

    def ye_spec(e):
        return pl.BlockSpec((None, pl.Element(COMB_WIN), pl.Element(D_MODEL)), lambda i, ws: (e, ws[e, i] * COMB_ALIGN, 0))

    grid_spec = pltpu.PrefetchScalarGridSpec(
        num_scalar_prefetch=1,
        grid=(n // t,),
        in_specs=[row(D_MODEL), row(N_EXPERTS), row(N_EXPERTS)] + [ye_spec(e) for e in range(N_EXPERTS)],
        out_specs=row(D_MODEL),
    )
    return pl.pallas_call(
        _combine_body,
        grid_spec=grid_spec,
        out_shape=jax.ShapeDtypeStruct((n, D_MODEL), F32),
        compiler_params=_params(("arbitrary",)),
        name="combine",
    )(win_start, h2d, pos_t, gates, *([ye] * N_EXPERTS))


def ec_moe(h2d, hn_bf16, probs, probs_t, wg, wu, wd):
    n = h2d.shape[0]
    cap = EC_FACTOR * n // N_EXPERTS
    cnt, sel = select(probs_t, cap)
    posm = jnp.where(sel > 0, cnt - 1, -1)
    row_end = cnt[:, :, SEL_ROW - 1]
    ye = expert_ffn(hn_bf16, posm, row_end, wg, wu, wd, cap)
    pos_t = posm.reshape(N_EXPERTS, n).T
    base = (cnt - sel).reshape(N_EXPERTS, n)[:, ::COMB_TILE]
    win_start = jnp.minimum(base // COMB_ALIGN, (cap - COMB_WIN) // COMB_ALIGN).astype(I32)
    return combine(h2d, pos_t, probs, ye, win_start)


def _block_diag(w):
    h, d, _ = w.shape
    eye = jnp.eye(h, dtype=w.dtype)
    return jnp.einsum("hde,hg->hdge", w, eye).reshape(h * d, h * d)


def _prep_layer(l, p):
    wcat = jnp.stack([jnp.concatenate([_block_diag(p["lru_wa"][l, d]), _block_diag(p["lru_wx"][l, d])], axis=1)
                      for d in range(2)]).astype(BF16)
    bcat = jnp.stack([jnp.concatenate([p["lru_ba"][l, d], p["lru_bx"][l, d]])[None, :] for d in range(2)])
    cdec = (-LRU_C * jax.nn.softplus(-p["lru_lambda"][l]))[:, None, :]
    return dict(
        w_in=p["w_in"][l].astype(BF16), w_out=p["w_out"][l].astype(BF16),
        wcat=wcat, bcat=bcat, cdec=cdec,
        wg=p["w_gate"][l].astype(BF16), wu=p["w_up"][l].astype(BF16), wd=p["w_down"][l].astype(BF16))


def _trunk(x, p, prepped):
    bsz, s, _ = x.shape
    n = bsz * s
    x2d = x.reshape(n, D_MODEL)
    for l, w in enumerate(prepped):
        lam_init = 0.8 - 0.6 * math.exp(-0.3 * l)
        pieces = in_proj(x2d, p["ln1_g"][l], w["w_in"])
        lx, lgate, rq, rk, rv, rg, dq, dk, dv = [a.reshape(bsz, s, a.shape[1]) for a in pieces]
        y_lru = lru_mixer(lx, lgate, p["conv_w"][l], p["conv_b"][l].reshape(1, LRU_W), w["wcat"], w["bcat"],
                          w["cdec"], p["lru_norm_g"][l].reshape(1, LRU_W))
        y_ret = ret_mixer(rq, rk, rv, rg, p["ret_norm_g"][l])
        q0, q1, kk, vv = attn_prep(dq, dk, dv, p["q_norm_g"][l], p["k_norm_g"][l])
        y_diff = diff_attn(q0, q1, kk, vv, p["rel_bias"], p["diff_lambda"][l], lam_init, p["diff_norm_g"][l])
        h2d, hn, probs, probs_t = out_proj(x2d, y_lru.reshape(n, LRU_W), y_ret.reshape(n, RET_W),
                                           y_diff.reshape(n, DIFF_W), w["w_out"], p["ln2_g"][l],
                                           p["w_router"][l])
        x2d = ec_moe(h2d, hn, probs, probs_t, w["wg"], w["wu"], w["wd"])
    return x2d.reshape(bsz, s, D_MODEL)


def kernel(x_prompt, x_sample, rel_bias, ln1_g, ln2_g, w_in, conv_w, conv_b, lru_wa, lru_ba, lru_wx, lru_bx,
           lru_lambda, lru_norm_g, ret_norm_g, q_norm_g, k_norm_g, diff_lambda, diff_norm_g, w_out, w_router,
           w_gate, w_up, w_down):
    p = dict(rel_bias=rel_bias, ln1_g=ln1_g, ln2_g=ln2_g, w_in=w_in, conv_w=conv_w, conv_b=conv_b,
             lru_wa=lru_wa, lru_ba=lru_ba, lru_wx=lru_wx, lru_bx=lru_bx, lru_lambda=lru_lambda,
             lru_norm_g=lru_norm_g, ret_norm_g=ret_norm_g, q_norm_g=q_norm_g, k_norm_g=k_norm_g,
             diff_lambda=diff_lambda, diff_norm_g=diff_norm_g, w_out=w_out, w_router=w_router,
             w_gate=w_gate, w_up=w_up, w_down=w_down)
    prepped = [_prep_layer(l, p) for l in range(w_in.shape[0])]
    return _trunk(x_prompt, p, prepped), _trunk(x_sample, p, prepped)
```

```python
import functools
import math

import numpy as np
import jax
import jax.numpy as jnp
from jax import lax
from jax.experimental import pallas as pl
from jax.experimental.pallas import tpu as pltpu

F32 = jnp.float32
BF16 = jnp.bfloat16
I32 = jnp.int32
HIGHEST = lax.Precision.HIGHEST

D_MODEL = 1024
HEAD_DIM = 64
LRU_W = 256
LRU_HEADS = 4
RET_W = 384
RET_HEADS = 6
DIFF_W = 384
DIFF_HEADS = 6
DIFF_HALF = 32
IN_SIZES = (LRU_W, LRU_W, RET_W, RET_W, RET_W, RET_W, DIFF_W, DIFF_W, DIFF_W)
IN_WIDTH = sum(IN_SIZES)
CONV_WIDTH = 4
LRU_C = 8.0
ROPE_BASE = 10000.0
NUM_BUCKETS = 32
MAX_DISTANCE = 128
N_EXPERTS = 16
EC_FACTOR = 2
D_FF = 2816
EPS = 1e-6

V7X_VMEM_BYTES = 64 * 1024 * 1024
SUBLANES = 8
LANES = 128

ROW_TILE = 512
SCAN_CHUNK = 256
RET_CHUNK = 256
ATTN_TILE = 512
ATTN_KEY_TILE = 512
SEL_ROW = 512
SLOT_BLOCK = 256
FF_CHUNK = 1408
COMB_TILE = 128
COMB_ALIGN = 16
COMB_WIN = COMB_TILE + COMB_ALIGN


def _params(sem, vmem_mb=48):
    return pltpu.CompilerParams(dimension_semantics=sem,
                                vmem_limit_bytes=vmem_mb * 1024 * 1024)


def _full(shape):
    nd = len(shape)
    return pl.BlockSpec(shape, lambda *_: (0,) * nd)


def _rms(x, g):
    return x * lax.rsqrt(jnp.mean(x * x, axis=-1, keepdims=True) + EPS) * g


def _in_proj_body(x_ref, g_ref, w_ref, *o_refs):
    xn = _rms(x_ref[...], g_ref[...]).astype(BF16)
    off = 0
    for o_ref, width in zip(o_refs, IN_SIZES):
        o_ref[...] = jnp.dot(xn, w_ref[:, off:off + width], preferred_element_type=F32)
        off += width


def in_proj(x2d, g, w_bf16, tm=ROW_TILE):
    n = x2d.shape[0]
    tm = min(tm, n)
    return pl.pallas_call(
        _in_proj_body,
        grid=(n // tm,),
        in_specs=[pl.BlockSpec((tm, D_MODEL), lambda i: (i, 0)),
                  _full((1, D_MODEL)), _full((D_MODEL, IN_WIDTH))],
        out_specs=[pl.BlockSpec((tm, w), lambda i: (i, 0)) for w in IN_SIZES],
        out_shape=[jax.ShapeDtypeStruct((n, w), F32) for w in IN_SIZES],
        compiler_params=_params(("parallel",)),
        name="in_proj",
    )(x2d, g.reshape(1, D_MODEL), w_bf16)


def _shift_rows(ext, s, tc):
    n = ext.shape[0]
    return pltpu.roll(ext, (-s) % n, axis=0)[SUBLANES:SUBLANES + tc]


def _neg_expm1(y):
    series = -y * (1.0 + y * (1.0 / 2) * (1.0 + y * (1.0 / 3) * (1.0 + y * (1.0 / 4) * (1.0 + y * (1.0 / 5)))))
    return jnp.where(y > -1.0 / 64, series, 1.0 - jnp.exp(y))


def _lru_scan(a, b, rev):
    tc = a.shape[0]
    t = lax.broadcasted_iota(I32, a.shape, 0)
    d = 1
    while d < tc:
        if rev:
            keep = t < tc - d
            a_o = pltpu.roll(a, tc - d, axis=0)
            b_o = pltpu.roll(b, tc - d, axis=0)
        else:
            keep = t >= d
            a_o = pltpu.roll(a, d, axis=0)
            b_o = pltpu.roll(b, d, axis=0)
        b = jnp.where(keep, a * b_o + b, b)
        a = jnp.where(keep, a * a_o, a)
        d *= 2
    return a, b


def _lru_body(rev, *refs):
    if rev:
        (x_ref, xp_ref, xn_ref, gate_ref, hf_ref, cw_ref, cb_ref, w_ref, b_ref, c_ref,
         ng_ref, o_ref, carry_ref) = refs
    else:
        (x_ref, xp_ref, xn_ref, cw_ref, cb_ref, w_ref, b_ref, c_ref, o_ref, carry_ref) = refs
    step = pl.program_id(1)
    nc = pl.num_programs(1)
    ci = nc - 1 - step if rev else step

    @pl.when(step == 0)
    def _():
        carry_ref[...] = jnp.zeros_like(carry_ref)

    x = x_ref[0]
    tc = x.shape[0]
    prev = xp_ref[0] * (ci > 0).astype(F32)
    nxt = xn_ref[0] * (ci < nc - 1).astype(F32)
    ext = jnp.concatenate([prev, x, nxt], axis=0)
    xc = cb_ref[...] + sum(cw_ref[j:j + 1, :] * _shift_rows(ext, j - CONV_WIDTH // 2, tc)
                           for j in range(CONV_WIDTH))
    z = jnp.dot(xc.astype(BF16), w_ref[...], preferred_element_type=F32) + b_ref[...]
    r = jax.nn.sigmoid(z[:, :LRU_W])
    i = jax.nn.sigmoid(z[:, LRU_W:])
    log_a = c_ref[...] * r
    a = jnp.exp(log_a)
    b = jnp.sqrt(_neg_expm1(2.0 * log_a)) * (i * xc)
    a_cum, h_loc = _lru_scan(a, b, rev)
    h = h_loc + a_cum * carry_ref[0:1, :]
    carry_ref[0:1, :] = h[0:1, :] if rev else h[tc - 1:tc, :]
    if rev:
        y = (hf_ref[0] + h) * jax.nn.gelu(gate_ref[0])
        o_ref[0] = _rms(y, ng_ref[...])
    else:
        o_ref[0] = h


def lru_mixer(lx, lgate, cw, cb, wcat, bcat, cdec, ng, tc=SCAN_CHUNK):
    bsz, s, _ = lx.shape
    tc = min(tc, s)
    nc = s // tc
    r8 = tc // SUBLANES
    nb8 = s // SUBLANES

    def specs(rev):
        cmap = (lambda b, c: (b, nc - 1 - c, 0)) if rev else (lambda b, c: (b, c, 0))
        if rev:
            pmap = lambda b, c: (b, jnp.maximum((nc - 1 - c) * r8 - 1, 0), 0)
            nmap = lambda b, c: (b, jnp.minimum((nc - c) * r8, nb8 - 1), 0)
        else:
            pmap = lambda b, c: (b, jnp.maximum(c * r8 - 1, 0), 0)
            nmap = lambda b, c: (b, jnp.minimum((c + 1) * r8, nb8 - 1), 0)
        main = pl.BlockSpec((1, tc, LRU_W), cmap)
        halo = [pl.BlockSpec((1, SUBLANES, LRU_W), pmap), pl.BlockSpec((1, SUBLANES, LRU_W), nmap)]
        return main, halo

    common = [_full((CONV_WIDTH, LRU_W)), _full((1, LRU_W)), _full((LRU_W, 2 * LRU_W)),
              _full((1, 2 * LRU_W)), _full((1, LRU_W))]
    main, halo = specs(False)
    hf = pl.pallas_call(
        functools.partial(_lru_body, False),
        grid=(bsz, nc),
        in_specs=[main] + halo + common,
        out_specs=main,
        out_shape=jax.ShapeDtypeStruct((bsz, s, LRU_W), F32),
        scratch_shapes=[pltpu.VMEM((SUBLANES, LRU_W), F32)],
        compiler_params=_params(("parallel", "arbitrary")),
        name="lru_fwd",
    )(lx, lx, lx, cw, cb, wcat[0], bcat[0], cdec[0])
    main, halo = specs(True)
    return pl.pallas_call(
        functools.partial(_lru_body, True),
        grid=(bsz, nc),
        in_specs=[main] + halo + [main, main] + common + [_full((1, LRU_W))],
        out_specs=main,
        out_shape=jax.ShapeDtypeStruct((bsz, s, LRU_W), F32),
        scratch_shapes=[pltpu.VMEM((SUBLANES, LRU_W), F32)],
        compiler_params=_params(("parallel", "arbitrary")),
        name="lru_rev",
    )(lx, lx, lx, lgate, hf, cw, cb, wcat[1], bcat[1], cdec[1], ng)


def _ret_log_gamma():
    return np.log1p(-np.exp2(-5.0 - np.arange(RET_HEADS, dtype=np.float64)))


@functools.lru_cache(maxsize=None)
def _ret_tables(c):
    lg = np.repeat(_ret_log_gamma(), HEAD_DIM)[None, :]
    idx = np.arange(c, dtype=np.float64)[:, None]
    dec = np.stack([np.exp((idx + 1.0) * lg),
                    np.exp((c - 1.0 - idx) * lg),
                    np.exp((c - idx) * lg),
                    np.exp(idx * lg)])
    chunk = np.exp(c * lg)
    dist = np.abs(idx - idx.T)
    intra = np.exp(dist[None] * _ret_log_gamma()[:, None, None])
    lane_head = np.arange(RET_W) // HEAD_DIM
    hmask = (lane_head[None, :] == np.arange(RET_HEADS)[:, None]).astype(np.float32)[:, None, :]
    bd = (lane_head[:, None] == lane_head[None, :]).astype(np.float32)
    return (dec.astype(np.float32), chunk.astype(np.float32), intra.astype(np.float32), hmask, bd)


def _rope_tables(s):
    half = HEAD_DIM // 2
    freqs = ROPE_BASE ** (-jnp.arange(half, dtype=F32) / half)
    ang = jnp.arange(s, dtype=F32)[:, None] * freqs[None, :]
    cos = jnp.cos(ang)
    sin = jnp.sin(ang)
    cos_t = jnp.tile(jnp.concatenate([cos, cos], axis=1), (1, RET_W // HEAD_DIM))
    sin_t = jnp.tile(jnp.concatenate([-sin, sin], axis=1), (1, RET_W // HEAD_DIM))
    return cos_t, sin_t


def _rope(x, cos, sin_signed):
    lane = lax.broadcasted_iota(I32, x.shape, 1)
    w = x.shape[1]
    half = HEAD_DIM // 2
    swapped = jnp.where(lane % HEAD_DIM < half,
                        pltpu.roll(x, w - half, axis=1), pltpu.roll(x, half, axis=1))
    return x * cos + swapped * sin_signed


def _ret_body(rev, *refs):
    if rev:
        (q_ref, k_ref, v_ref, cos_ref, sin_ref, dec_ref, chunk_ref, bd_ref,
         of_ref, g_ref, ng_ref, o_ref, state_ref) = refs
    else:
        (q_ref, k_ref, v_ref, cos_ref, sin_ref, dec_ref, chunk_ref, bd_ref,
         intra_ref, hmask_ref, o_ref, state_ref) = refs

    @pl.when(pl.program_id(1) == 0)
    def _():
        state_ref[...] = jnp.zeros_like(state_ref)

    cos = cos_ref[...]
    sin = sin_ref[...]
    q = _rope(q_ref[0], cos, sin)
    k = _rope(k_ref[0], cos, sin) * (HEAD_DIM ** -0.5)
    vb = v_ref[0].astype(BF16)
    qd, kd = (2, 3) if rev else (0, 1)
    state = state_ref[...]
    cross = jnp.dot((q * dec_ref[qd]).astype(BF16), state.astype(BF16), preferred_element_type=F32)
    kv = lax.dot_general((k * dec_ref[kd]).astype(BF16), vb, (((0,), (0,)), ((), ())),
                         preferred_element_type=F32)
    state_ref[...] = state * chunk_ref[...] + kv * bd_ref[...]
    if rev:
        o = of_ref[0] + cross
        ms = jnp.dot(o * o, bd_ref[...], precision=HIGHEST, preferred_element_type=F32) * (1.0 / HEAD_DIM)
        o = o * lax.rsqrt(ms + EPS) * ng_ref[...]
        o_ref[0] = jax.nn.silu(g_ref[0]) * o
    else:
        kb = k.astype(BF16)
        out = cross
        for h in range(RET_HEADS):
            hm = hmask_ref[h]
            s = lax.dot_general((q * hm).astype(BF16), kb, (((1,), (1,)), ((), ())),
                                preferred_element_type=F32)
            s = (s * intra_ref[h]).astype(BF16)
            out = out + jnp.dot(s, (v_ref[0] * hm).astype(BF16), preferred_element_type=F32)
        o_ref[0] = out


def ret_mixer(rq, rk, rv, rg, ng, c=RET_CHUNK):
    bsz, s, _ = rq.shape
    c = min(c, s)
    nc = s // c
    dec, chunk, intra, hmask, bd = _ret_tables(c)
    cos_t, sin_t = _rope_tables(s)

    def specs(rev):
        cmap = (lambda b, i: (b, nc - 1 - i, 0)) if rev else (lambda b, i: (b, i, 0))
        tmap = (lambda b, i: (nc - 1 - i, 0)) if rev else (lambda b, i: (i, 0))
        main = pl.BlockSpec((1, c, RET_W), cmap)
        tab = pl.BlockSpec((c, RET_W), tmap)
        return main, tab

    consts = [_full((4, c, RET_W)), _full((1, RET_W)), _full((RET_W, RET_W))]
    main, tab = specs(False)
    of = pl.pallas_call(
        functools.partial(_ret_body, False),
        grid=(bsz, nc),
        in_specs=[main, main, main, tab, tab] + consts + [_full((RET_HEADS, c, c)), _full((RET_HEADS, 1, RET_W))],
        out_specs=main,
        out_shape=jax.ShapeDtypeStruct((bsz, s, RET_W), F32),
        scratch_shapes=[pltpu.VMEM((RET_W, RET_W), F32)],
        compiler_params=_params(("parallel", "arbitrary")),
        name="ret_fwd",
    )(rq, rk, rv, cos_t, sin_t, dec, chunk, bd, intra, hmask)
    main, tab = specs(True)
    return pl.pallas_call(
        functools.partial(_ret_body, True),
        grid=(bsz, nc),
        in_specs=[main, main, main, tab, tab] + consts + [main, main, _full((1, RET_W))],
        out_specs=main,
        out_shape=jax.ShapeDtypeStruct((bsz, s, RET_W), F32),
        scratch_shapes=[pltpu.VMEM((RET_W, RET_W), F32)],
        compiler_params=_params(("parallel", "arbitrary")),
        name="ret_rev",
    )(rq, rk, rv, cos_t, sin_t, dec, chunk, bd, of, rg, ng.reshape(1, RET_W))


HEAD_SLOT = 128
LOG2E = 1.4426950408889634


@functools.lru_cache(maxsize=None)
def _attn_consts():
    lane = np.arange(DIFF_W)
    grp = lane // DIFF_HALF
    bd32 = (grp[:, None] == grp[None, :]).astype(np.float32) / DIFF_HALF
    head = lane // HEAD_DIM
    inner = lane % HEAD_DIM
    dst = head * HEAD_SLOT + inner
    place = np.zeros((3, DIFF_W, DIFF_HEADS * HEAD_SLOT), np.float32)
    place[2, lane, dst] = 1.0
    first = inner < DIFF_HALF
    place[0, lane[first], dst[first]] = 1.0
    place[1, lane[~first], dst[~first]] = 1.0
    ones_col = np.zeros((1, DIFF_HEADS * HEAD_SLOT), np.float32)
    ones_col[0, np.arange(DIFF_HEADS) * HEAD_SLOT + HEAD_DIM] = 1.0
    return bd32, place, ones_col


def _attn_prep_body(q_ref, k_ref, v_ref, qg_ref, kg_ref, bd_ref, place_ref, ones_ref,
                    q0_ref, q1_ref, kk_ref, vv_ref):
    def qk_norm(x, g):
        ms = jnp.dot(x * x, bd_ref[...], precision=HIGHEST, preferred_element_type=F32)
        return x * lax.rsqrt(ms + EPS) * g

    qh = (qk_norm(q_ref[0], qg_ref[...]) * (DIFF_HALF ** -0.5 * LOG2E)).astype(BF16)
    kh = qk_norm(k_ref[0], kg_ref[...]).astype(BF16)
    vb = v_ref[0].astype(BF16)
    q0_ref[0] = jnp.dot(qh, place_ref[0], preferred_element_type=F32).astype(BF16)
    q1_ref[0] = jnp.dot(qh, place_ref[1], preferred_element_type=F32).astype(BF16)
    kk_ref[0] = jnp.dot(kh, place_ref[2], preferred_element_type=F32).astype(BF16)
    vv_ref[0] = (jnp.dot(vb, place_ref[2], preferred_element_type=F32) + ones_ref[...]).astype(BF16)


def attn_prep(dq, dk, dv, qg, kg, tc=ROW_TILE):
    bsz, s, _ = dq.shape
    tc = min(tc, s)
    bd32, place, ones_col = _attn_consts()
    wide = DIFF_HEADS * HEAD_SLOT
    main = pl.BlockSpec((1, tc, DIFF_W), lambda b, c: (b, c, 0))
    outb = pl.BlockSpec((1, tc, wide), lambda b, c: (b, c, 0))
    rep = DIFF_W // DIFF_HALF
    return pl.pallas_call(
        _attn_prep_body,
        grid=(bsz, s // tc),
        in_specs=[main, main, main, _full((1, DIFF_W)), _full((1, DIFF_W)),
                  _full((DIFF_W, DIFF_W)), _full((3, DIFF_W, wide)), _full((1, wide))],
        out_specs=[outb] * 4,
        out_shape=[jax.ShapeDtypeStruct((bsz, s, wide), BF16)] * 4,
        compiler_params=_params(("parallel", "parallel")),
        name="attn_prep",
    )(dq, dk, dv, jnp.tile(qg, rep).reshape(1, DIFF_W), jnp.tile(kg, rep).reshape(1, DIFF_W),
      bd32, jnp.asarray(place, BF16), ones_col)


@functools.lru_cache(maxsize=None)
def _bucket_tiles(t, tk):
    nb = NUM_BUCKETS // 2
    max_exact = nb // 2
    r = np.arange(t)[:, None]
    c = np.arange(tk)[None, :]
    rel = np.stack([c - r + d * t for d in range(-(tk // t), 2)])
    n = np.abs(rel)
    nf = np.maximum(n, 1).astype(np.float64)
    large = max_exact + np.floor(2.0 * np.log2(nf / max_exact)).astype(np.int64)
    large = np.minimum(large, nb - 1)
    return (np.where(rel > 0, nb, 0) + np.where(n < max_exact, n, large)).astype(np.int32)


def _attn_body(q0_ref, q1_ref, k_ref, v_ref, bias_ref, lam_ref, linit_ref, g_ref, o_ref):
    t = q0_ref.shape[1]
    tk = bias_ref.shape[3]
    kq = tk // t
    nk = k_ref.shape[1] // tk
    qi = pl.program_id(2)
    lam = (jnp.exp(jnp.sum(lam_ref[0:1, :] * lam_ref[1:2, :], axis=1, keepdims=True))
           - jnp.exp(jnp.sum(lam_ref[2:3, :] * lam_ref[3:4, :], axis=1, keepdims=True))
           + linit_ref[...])
    lane = lax.broadcasted_iota(I32, (t, HEAD_SLOT), 1)
    qs = [jnp.concatenate([q0_ref[0, :, lo:lo + HEAD_SLOT], q1_ref[0, :, lo:lo + HEAD_SLOT]], axis=0)
          for lo in (0, HEAD_SLOT)]

    def body(j, carry):
        rows = pl.ds(pl.multiple_of(j * tk, tk), tk)
        bidx = jnp.clip(j * kq - qi, -kq - 1, 2) + kq + 1
        new = []
        for hh in range(2):
            lo = hh * HEAD_SLOT
            m_i, acc = carry[hh]
            kb = k_ref[0, rows, lo:lo + HEAD_SLOT]
            vb = v_ref[0, rows, lo:lo + HEAD_SLOT]
            s = lax.dot_general(qs[hh], kb, (((1,), (1,)), ((), ())), preferred_element_type=F32)
            bt = bias_ref[hh, bidx]
            s = s + jnp.concatenate([bt, bt], axis=0)
            m_new = jnp.maximum(m_i, jnp.max(s, axis=1, keepdims=True))
            p = jnp.exp2(s - m_new)
            acc = jnp.exp2(m_i - m_new) * acc + jnp.dot(p.astype(BF16), vb, preferred_element_type=F32)
            new.append((m_new, acc))
        return tuple(new)

    init = (jnp.full((2 * t, 1), -1e30, F32), jnp.zeros((2 * t, HEAD_SLOT), F32))
    res = lax.fori_loop(0, nk, body, (init, init))

    outs = []
    for hh in range(2):
        acc = res[hh][1]
        sm = acc / acc[:, HEAD_DIM:HEAD_DIM + 1]
        o = sm[:t] - lam * sm[t:]
        o = jnp.where(lane < HEAD_DIM, o, 0.0)
        ms = jnp.sum(o * o, axis=1, keepdims=True) * (1.0 / HEAD_DIM)
        outs.append(o * lax.rsqrt(ms + EPS))
    both = jnp.where(lane < HEAD_DIM, outs[0], pltpu.roll(outs[1], HEAD_DIM, axis=1))
    o_ref[0] = both * g_ref[...] * (1.0 - linit_ref[...])


def diff_attn(q0, q1, kk, vv, rel_bias, lam_vecs, lam_init, ng, t=ATTN_TILE, tk=ATTN_KEY_TILE):
    bsz, s, _ = q0.shape
    t = min(t, s)
    tk = min(tk, s)
    assert t >= MAX_DISTANCE and tk % t == 0
    buckets = _bucket_tiles(t, tk)
    near = jnp.transpose(rel_bias[buckets], (3, 0, 1, 2))
    nb = NUM_BUCKETS // 2
    left = jnp.broadcast_to(rel_bias[nb - 1][:, None, None, None], (DIFF_HEADS, 1, t, tk))
    right = jnp.broadcast_to(rel_bias[NUM_BUCKETS - 1][:, None, None, None], (DIFF_HEADS, 1, t, tk))
    bias = jnp.concatenate([left, near, right], axis=1) * LOG2E
    n_tiles = bias.shape[1]
    pair = 2 * HEAD_SLOT
    qspec = pl.BlockSpec((1, t, pair), lambda b, h, i: (b, i, h))
    kspec = pl.BlockSpec((1, s, pair), lambda b, h, i: (b, 0, h))
    lam_pad = jnp.zeros((4, LANES), F32).at[:, :DIFF_HALF].set(lam_vecs)
    linit = jnp.full((1, LANES), lam_init, F32)
    return pl.pallas_call(
        _attn_body,
        grid=(bsz, DIFF_HEADS // 2, s // t),
        in_specs=[qspec, qspec, kspec, kspec,
                  pl.BlockSpec((2, n_tiles, t, tk), lambda b, h, i: (h, 0, 0, 0)),
                  _full((4, LANES)), _full((1, LANES)), _full((1, LANES))],
        out_specs=pl.BlockSpec((1, t, 2 * HEAD_DIM), lambda b, h, i: (b, i, h)),
        out_shape=jax.ShapeDtypeStruct((bsz, s, DIFF_W), F32),
        compiler_params=_params(("parallel", "parallel", "arbitrary")),
        name="attn",
    )(q0, q1, kk, vv, bias, lam_pad, linit, jnp.tile(ng, 2).reshape(1, LANES))


def _out_proj_body(x_ref, yl_ref, yr_ref, yd_ref, wl_ref, wr_ref, wd_ref, g_ref, wrt_ref, wrtt_ref,
                   h_ref, hn_ref, p_ref, pt_ref):
    h = (x_ref[...]
         + jnp.dot(yl_ref[...].astype(BF16), wl_ref[...], preferred_element_type=F32)
         + jnp.dot(yr_ref[...].astype(BF16), wr_ref[...], preferred_element_type=F32)
         + jnp.dot(yd_ref[...].astype(BF16), wd_ref[...], preferred_element_type=F32))
    h_ref[...] = h
    hn = _rms(h, g_ref[...])
    hn_ref[...] = hn.astype(BF16)
    logits = jnp.dot(hn, wrt_ref[...], precision=HIGHEST, preferred_element_type=F32)
    e = jnp.exp(logits - jnp.max(logits, axis=1, keepdims=True))
    p_ref[...] = e / jnp.sum(e, axis=1, keepdims=True)
    logits_t = lax.dot_general(wrtt_ref[...], hn, (((1,), (1,)), ((), ())), precision=HIGHEST,
                               preferred_element_type=F32)
    et = jnp.exp(logits_t - jnp.max(logits_t, axis=0, keepdims=True))
    pt_ref[...] = et / jnp.sum(et, axis=0, keepdims=True)


def out_proj(x2d, yl, yr, yd, w_out_bf16, g, w_router, tm=ROW_TILE):
    n = x2d.shape[0]
    tm = min(tm, n)
    row = lambda w: pl.BlockSpec((tm, w), lambda i: (i, 0))
    return pl.pallas_call(
        _out_proj_body,
        grid=(n // tm,),
        in_specs=[row(D_MODEL), row(LRU_W), row(RET_W), row(DIFF_W),
                  _full((LRU_W, D_MODEL)), _full((RET_W, D_MODEL)), _full((DIFF_W, D_MODEL)),
                  _full((1, D_MODEL)), _full((D_MODEL, N_EXPERTS)), _full((N_EXPERTS, D_MODEL))],
        out_specs=[row(D_MODEL), row(D_MODEL), row(N_EXPERTS),
                   pl.BlockSpec((N_EXPERTS, tm), lambda i: (0, i))],
        out_shape=[jax.ShapeDtypeStruct((n, D_MODEL), F32), jax.ShapeDtypeStruct((n, D_MODEL), BF16),
                   jax.ShapeDtypeStruct((n, N_EXPERTS), F32), jax.ShapeDtypeStruct((N_EXPERTS, n), F32)],
        compiler_params=_params(("parallel",)),
        name="out_proj",
    )(x2d, yl, yr, yd, w_out_bf16[:LRU_W], w_out_bf16[LRU_W:LRU_W + RET_W], w_out_bf16[LRU_W + RET_W:],
      g.reshape(1, D_MODEL), w_router, w_router.T)


def _row_cumsum(x01, tri_ref, nr):
    within = jnp.dot(x01.astype(F32).astype(BF16), tri_ref[...], preferred_element_type=F32).astype(I32)
    tot = jnp.broadcast_to(within[:, SEL_ROW - 1:SEL_ROW], (nr, LANES))
    r = lax.broadcasted_iota(I32, (nr, LANES), 0)
    inc = tot
    d = 1
    while d < nr:
        inc = inc + jnp.where(r >= d, pltpu.roll(inc, d, axis=0), 0)
        d *= 2
    return within + (inc - tot)[:, 0:1]


def _select_body(cap, p_ref, tri_ref, cnt_ref, sel_ref):
    p = p_ref[0]
    nr = p.shape[0]
    bits = pltpu.bitcast(p, I32)

    def body(i, prefix):
        cand = prefix | (jnp.int32(1) << (30 - i))
        cnt = jnp.sum((bits >= cand).astype(I32), keepdims=True)
        return jnp.where(cnt >= cap, cand, prefix)

    thr = lax.fori_loop(0, 31, body, jnp.zeros((1, 1), I32))
    gt = bits > thr
    eq = bits == thr
    need = cap - jnp.sum(gt.astype(I32), keepdims=True)
    eq01 = eq.astype(I32)
    rank_eq = _row_cumsum(eq01, tri_ref, nr) - eq01
    sel = jnp.where(gt, 1, jnp.where(eq & (rank_eq < need), 1, 0))
    sel_ref[0] = sel
    cnt_ref[0] = _row_cumsum(sel, tri_ref, nr)


def select(probs_t, cap):
    e, n = probs_t.shape
    nr = n // SEL_ROW
    tri = np.triu(np.ones((SEL_ROW, SEL_ROW), np.float32))
    blk = pl.BlockSpec((1, nr, SEL_ROW), lambda i: (i, 0, 0))
    return pl.pallas_call(
        functools.partial(_select_body, cap),
        grid=(e,),
        in_specs=[blk, _full((SEL_ROW, SEL_ROW))],
        out_specs=[blk, blk],
        out_shape=[jax.ShapeDtypeStruct((e, nr, SEL_ROW), I32)] * 2,
        compiler_params=_params(("parallel",)),
        name="select",
    )(probs_t.reshape(e, nr, SEL_ROW), jnp.asarray(tri, BF16))


def _ffn_body(sb_ref, row_ref, flag_ref, pos_ref, x_ref, wg_ref, wu_ref, wd_ref, o_ref, xe_ref):
    e = pl.program_id(0)
    s = pl.program_id(1)
    flags = flag_ref[e, s]
    w = xe_ref.shape[0]

    @pl.when((flags & 2) != 0)
    def _():
        xe_ref[...] = jnp.zeros_like(xe_ref)

    @pl.when((flags & 1) != 0)
    def _():
        slot = sb_ref[e, s] * w + lax.broadcasted_iota(I32, (w, 1), 0)
        onehot = jnp.where(pos_ref[0, 0] == slot, 1.0, 0.0).astype(BF16)
        xe_ref[...] += jnp.dot(onehot, x_ref[...], preferred_element_type=F32)

    @pl.when((flags & 4) != 0)
    def _():
        xe = xe_ref[...].astype(BF16)
        acc = jnp.zeros((w, D_MODEL), F32)
        for c0 in range(0, D_FF, FF_CHUNK):
            g = jnp.dot(xe, wg_ref[0, :, c0:c0 + FF_CHUNK], preferred_element_type=F32)
            u = jnp.dot(xe, wu_ref[0, :, c0:c0 + FF_CHUNK], preferred_element_type=F32)
            mid = (jax.nn.silu(g) * u).astype(BF16)
            acc = acc + jnp.dot(mid, wd_ref[0, c0:c0 + FF_CHUNK, :], preferred_element_type=F32)
        o_ref[0] = acc.astype(BF16)


def _ffn_schedule(row_end, cap, w):
    e, nr = row_end.shape
    n_sb = cap // w
    edges = jnp.broadcast_to(jnp.arange(1, n_sb + 1, dtype=I32) * w, (e, n_sb))
    stops = jnp.sort(jnp.concatenate([edges, row_end], axis=1), axis=1)
    starts = jnp.concatenate([jnp.zeros((e, 1), I32), stops[:, :-1]], axis=1)
    valid = stops > starts
    sb = jnp.minimum(starts // w, n_sb - 1)
    row = jnp.minimum(jnp.sum(row_end[:, None, :] <= starts[:, :, None], axis=2), nr - 1)
    flags = (valid.astype(I32) + 2 * (valid & (starts % w == 0)).astype(I32)
             + 4 * (valid & (stops % w == 0)).astype(I32))
    return sb.astype(I32), row.astype(I32), flags


def expert_ffn(hn_bf16, posm, row_end, wg, wu, wd, cap, w=SLOT_BLOCK):
    n = hn_bf16.shape[0]
    e, nr, _ = posm.shape
    w = min(w, cap)
    sb, row, flags = _ffn_schedule(row_end, cap, w)
    steps = sb.shape[1]
    grid_spec = pltpu.PrefetchScalarGridSpec(
        num_scalar_prefetch=3,
        grid=(e, steps),
        in_specs=[
            pl.BlockSpec((1, 1, 1, SEL_ROW), lambda i, s, sb, row, fl: (i, row[i, s], 0, 0)),
            pl.BlockSpec((SEL_ROW, D_MODEL), lambda i, s, sb, row, fl: (row[i, s], 0)),
            pl.BlockSpec((1, D_MODEL, D_FF), lambda i, s, sb, row, fl: (i, 0, 0)),
            pl.BlockSpec((1, D_MODEL, D_FF), lambda i, s, sb, row, fl: (i, 0, 0)),
            pl.BlockSpec((1, D_FF, D_MODEL), lambda i, s, sb, row, fl: (i, 0, 0)),
        ],
        out_specs=pl.BlockSpec((1, w, D_MODEL), lambda i, s, sb, row, fl: (i, sb[i, s], 0)),
        scratch_shapes=[pltpu.VMEM((w, D_MODEL), F32)],
    )
    return pl.pallas_call(
        _ffn_body,
        grid_spec=grid_spec,
        out_shape=jax.ShapeDtypeStruct((e, cap, D_MODEL), BF16),
        compiler_params=_params(("arbitrary", "arbitrary"), vmem_mb=58),
        name="ffn",
    )(sb, row, flags, posm.reshape(e, nr, 1, SEL_ROW), hn_bf16, wg, wu, wd)


def _combine_body(ws_ref, h_ref, pos_ref, gate_ref, *refs):
    ye_refs = refs[:N_EXPERTS]
    o_ref = refs[N_EXPERTS]
    i = pl.program_id(0)
    acc = h_ref[...]
    pos = pos_ref[...]
    gate = gate_ref[...]
    lane = lax.broadcasted_iota(I32, (1, COMB_WIN), 1)
    for e in range(N_EXPERTS):
        slots = ws_ref[e, i] * COMB_ALIGN + lane
        g = jnp.where(pos[:, e:e + 1] == slots, gate[:, e:e + 1], 0.0).astype(BF16)
        acc = acc + jnp.dot(g, ye_refs[e][...], preferred_element_type=F32)
    o_ref[...] = acc


def combine(h2d, pos_t, gates, ye, win_start):
    n = h2d.shape[0]
    t = COMB_TILE
    row = lambda w: pl.BlockSpec((t, w), lambda i, ws: (i, 0))

    def ye_spec(e):
        return pl.BlockSpec((None, pl.Element(COMB_WIN), pl.Element(D_MODEL)), lambda i, ws: (e, ws[e, i] * COMB_ALIGN, 0))

    grid_spec = pltpu.PrefetchScalarGridSpec(
        num_scalar_prefetch=1,
        grid=(n // t,),
        in_specs=[row(D_MODEL), row(N_EXPERTS), row(N_EXPERTS)] + [ye_spec(e) for e in range(N_EXPERTS)],
        out_specs=row(D_MODEL),
    )
    return pl.pallas_call(
        _combine_body,
        grid_spec=grid_spec,
        out_shape=jax.ShapeDtypeStruct((n, D_MODEL), F32),
        compiler_params=_params(("arbitrary",)),
        name="combine",
    )(win_start, h2d, pos_t, gates, *([ye] * N_EXPERTS))


def ec_moe(h2d, hn_bf16, probs, probs_t, wg, wu, wd):
    n = h2d.shape[0]
    cap = EC_FACTOR * n // N_EXPERTS
    cnt, sel = select(probs_t, cap)
    posm = jnp.where(sel > 0, cnt - 1, -1)
    row_end = cnt[:, :, SEL_ROW - 1]
    ye = expert_ffn(hn_bf16, posm, row_end, wg, wu, wd, cap)
    pos_t = posm.reshape(N_EXPERTS, n).T
    base = (cnt - sel).reshape(N_EXPERTS, n)[:, ::COMB_TILE]
    win_start = jnp.minimum(base // COMB_ALIGN, (cap - COMB_WIN) // COMB_ALIGN).astype(I32)
    return combine(h2d, pos_t, probs, ye, win_start)


def _block_diag(w):
    h, d, _ = w.shape
    eye = jnp.eye(h, dtype=w.dtype)
    return jnp.einsum("hde,hg->hdge", w, eye).reshape(h * d, h * d)


def _prep_layer(l, p):
    wcat = jnp.stack([jnp.concatenate([_block_diag(p["lru_wa"][l, d]), _block_diag(p["lru_wx"][l, d])], axis=1)
                      for d in range(2)]).astype(BF16)
    bcat = jnp.stack([jnp.concatenate([p["lru_ba"][l, d], p["lru_bx"][l, d]])[None, :] for d in range(2)])
    cdec = (-LRU_C * jax.nn.softplus(-p["lru_lambda"][l]))[:, None, :]
    return dict(
        w_in=p["w_in"][l].astype(BF16), w_out=p["w_out"][l].astype(BF16),
        wcat=wcat, bcat=bcat, cdec=cdec,
        wg=p["w_gate"][l].astype(BF16), wu=p["w_up"][l].astype(BF16), wd=p["w_down"][l].astype(BF16))


def _trunk(x, p, prepped):
    bsz, s, _ = x.shape
    n = bsz * s
    x2d = x.reshape(n, D_MODEL)
    for l, w in enumerate(prepped):
        lam_init = 0.8 - 0.6 * math.exp(-0.3 * l)
        pieces = in_proj(x2d, p["ln1_g"][l], w["w_in"])
        lx, lgate, rq, rk, rv, rg, dq, dk, dv = [a.reshape(bsz, s, a.shape[1]) for a in pieces]
        y_lru = lru_mixer(lx, lgate, p["conv_w"][l], p["conv_b"][l].reshape(1, LRU_W), w["wcat"], w["bcat"],
                          w["cdec"], p["lru_norm_g"][l].reshape(1, LRU_W))
        y_ret = ret_mixer(rq, rk, rv, rg, p["ret_norm_g"][l])
        q0, q1, kk, vv = attn_prep(dq, dk, dv, p["q_norm_g"][l], p["k_norm_g"][l])
        y_diff = diff_attn(q0, q1, kk, vv, p["rel_bias"], p["diff_lambda"][l], lam_init, p["diff_norm_g"][l])
        h2d, hn, probs, probs_t = out_proj(x2d, y_lru.reshape(n, LRU_W), y_ret.reshape(n, RET_W),
                                           y_diff.reshape(n, DIFF_W), w["w_out"], p["ln2_g"][l],
                                           p["w_router"][l])
        x2d = ec_moe(h2d, hn, probs, probs_t, w["wg"], w["wu"], w["wd"])
    return x2d.reshape(bsz, s, D_MODEL)


def kernel(x_prompt, x_sample, rel_bias, ln1_g, ln2_g, w_in, conv_w, conv_b, lru_wa, lru_ba, lru_wx, lru_bx,
           lru_lambda, lru_norm_g, ret_norm_g, q_norm_g, k_norm_g, diff_lambda, diff_norm_g, w_out, w_router,
           w_gate, w_up, w_down):
    p = dict(rel_bias=rel_bias, ln1_g=ln1_g, ln2_g=ln2_g, w_in=w_in, conv_w=conv_w, conv_b=conv_b,
             lru_wa=lru_wa, lru_ba=lru_ba, lru_wx=lru_wx, lru_bx=lru_bx, lru_lambda=lru_lambda,
             lru_norm_g=lru_norm_g, ret_norm_g=ret_norm_g, q_norm_g=q_norm_g, k_norm_g=k_norm_g,
             diff_lambda=diff_lambda, diff_norm_g=diff_norm_g, w_out=w_out, w_router=w_router,
             w_gate=w_gate, w_up=w_up, w_down=w_down)
    prepped = [_prep_layer(l, p) for l in range(w_in.shape[0])]
    return _trunk(x_prompt, p, prepped), _trunk(x_sample, p, prepped)
```

```python
import functools
import math

import numpy as np
import jax
import jax.numpy as jnp
from jax import lax
from jax.experimental import pallas as pl
from jax.experimental.pallas import tpu as pltpu

F32 = jnp.float32
BF16 = jnp.bfloat16
I32 = jnp.int32
HIGHEST = lax.Precision.HIGHEST

D_MODEL = 1024
HEAD_DIM = 64
LRU_W = 256
LRU_HEADS = 4
RET_W = 384
RET_HEADS = 6
DIFF_W = 384
DIFF_HEADS = 6
DIFF_HALF = 32
IN_SIZES = (LRU_W, LRU_W, RET_W, RET_W, RET_W, RET_W, DIFF_W, DIFF_W, DIFF_W)
IN_WIDTH = sum(IN_SIZES)
CONV_WIDTH = 4
LRU_C = 8.0
ROPE_BASE = 10000.0
NUM_BUCKETS = 32
MAX_DISTANCE = 128
N_EXPERTS = 16
EC_FACTOR = 2
D_FF = 2816
EPS = 1e-6

V7X_VMEM_BYTES = 64 * 1024 * 1024
SUBLANES = 8
LANES = 128

ROW_TILE = 512
SCAN_CHUNK = 256
RET_CHUNK = 256
ATTN_TILE = 512
ATTN_KEY_TILE = 512
SEL_ROW = 512
SLOT_BLOCK = 256
FF_CHUNK = 1408
COMB_TILE = 128
COMB_ALIGN = 16
COMB_WIN = COMB_TILE + COMB_ALIGN


def _params(sem, vmem_mb=48):
    return pltpu.CompilerParams(dimension_semantics=sem,
                                vmem_limit_bytes=vmem_mb * 1024 * 1024)


def _full(shape):
    nd = len(shape)
    return pl.BlockSpec(shape, lambda *_: (0,) * nd)


def _rms(x, g):
    return x * lax.rsqrt(jnp.mean(x * x, axis=-1, keepdims=True) + EPS) * g


def _split_bf16(x):
    hi = x.astype(BF16)
    return hi, (x - hi.astype(F32)).astype(BF16)


def _group_sum(x, ones_bf16):
    hi, lo = _split_bf16(x)
    return (jnp.dot(hi, ones_bf16, preferred_element_type=F32)
            + jnp.dot(lo, ones_bf16, preferred_element_type=F32))


def _in_proj_body(x_ref, g_ref, w_ref, *o_refs):
    xn = _rms(x_ref[...], g_ref[...]).astype(BF16)
    off = 0
    for o_ref, width in zip(o_refs, IN_SIZES):
        o_ref[...] = jnp.dot(xn, w_ref[:, off:off + width], preferred_element_type=F32)
        off += width


def in_proj(x2d, g, w_bf16, tm=ROW_TILE):
    n = x2d.shape[0]
    tm = min(tm, n)
    return pl.pallas_call(
        _in_proj_body,
        grid=(n // tm,),
        in_specs=[pl.BlockSpec((tm, D_MODEL), lambda i: (i, 0)),
                  _full((1, D_MODEL)), _full((D_MODEL, IN_WIDTH))],
        out_specs=[pl.BlockSpec((tm, w), lambda i: (i, 0)) for w in IN_SIZES],
        out_shape=[jax.ShapeDtypeStruct((n, w), F32) for w in IN_SIZES],
        compiler_params=_params(("parallel",)),
        name="in_proj",
    )(x2d, g.reshape(1, D_MODEL), w_bf16)


def _shift_rows(ext, s, tc):
    n = ext.shape[0]
    return pltpu.roll(ext, (-s) % n, axis=0)[SUBLANES:SUBLANES + tc]


def _neg_expm1(y):
    series = -y * (1.0 + y * (1.0 / 2) * (1.0 + y * (1.0 / 3) * (1.0 + y * (1.0 / 4) * (1.0 + y * (1.0 / 5)))))
    return jnp.where(y > -1.0 / 64, series, 1.0 - jnp.exp(y))


def _lru_scan(a, b, rev):
    tc = a.shape[0]
    t = lax.broadcasted_iota(I32, a.shape, 0)
    d = 1
    while d < tc:
        if rev:
            keep = t < tc - d
            a_o = pltpu.roll(a, tc - d, axis=0)
            b_o = pltpu.roll(b, tc - d, axis=0)
        else:
            keep = t >= d
            a_o = pltpu.roll(a, d, axis=0)
            b_o = pltpu.roll(b, d, axis=0)
        b = jnp.where(keep, a * b_o + b, b)
        a = jnp.where(keep, a * a_o, a)
        d *= 2
    return a, b


def _lru_body(rev, *refs):
    if rev:
        (x_ref, xp_ref, xn_ref, gate_ref, hf_ref, cw_ref, cb_ref, w_ref, b_ref, c_ref,
         ng_ref, o_ref, carry_ref) = refs
    else:
        (x_ref, xp_ref, xn_ref, cw_ref, cb_ref, w_ref, b_ref, c_ref, o_ref, carry_ref) = refs
    step = pl.program_id(1)
    nc = pl.num_programs(1)
    ci = nc - 1 - step if rev else step

    @pl.when(step == 0)
    def _():
        carry_ref[...] = jnp.zeros_like(carry_ref)

    x = x_ref[0]
    tc = x.shape[0]
    prev = xp_ref[0] * (ci > 0).astype(F32)
    nxt = xn_ref[0] * (ci < nc - 1).astype(F32)
    ext = jnp.concatenate([prev, x, nxt], axis=0)
    xc = cb_ref[...] + sum(cw_ref[j:j + 1, :] * _shift_rows(ext, j - CONV_WIDTH // 2, tc)
                           for j in range(CONV_WIDTH))
    z = jnp.dot(xc.astype(BF16), w_ref[...], preferred_element_type=F32) + b_ref[...]
    r = jax.nn.sigmoid(z[:, :LRU_W])
    i = jax.nn.sigmoid(z[:, LRU_W:])
    log_a = c_ref[...] * r
    a = jnp.exp(log_a)
    b = jnp.sqrt(_neg_expm1(2.0 * log_a)) * (i * xc)
    a_cum, h_loc = _lru_scan(a, b, rev)
    h = h_loc + a_cum * carry_ref[0:1, :]
    carry_ref[0:1, :] = h[0:1, :] if rev else h[tc - 1:tc, :]
    if rev:
        y = (hf_ref[0] + h) * jax.nn.gelu(gate_ref[0])
        o_ref[0] = _rms(y, ng_ref[...])
    else:
        o_ref[0] = h


def lru_mixer(lx, lgate, cw, cb, wcat, bcat, cdec, ng, tc=SCAN_CHUNK):
    bsz, s, _ = lx.shape
    tc = min(tc, s)
    nc = s // tc
    r8 = tc // SUBLANES
    nb8 = s // SUBLANES

    def specs(rev):
        cmap = (lambda b, c: (b, nc - 1 - c, 0)) if rev else (lambda b, c: (b, c, 0))
        if rev:
            pmap = lambda b, c: (b, jnp.maximum((nc - 1 - c) * r8 - 1, 0), 0)
            nmap = lambda b, c: (b, jnp.minimum((nc - c) * r8, nb8 - 1), 0)
        else:
            pmap = lambda b, c: (b, jnp.maximum(c * r8 - 1, 0), 0)
            nmap = lambda b, c: (b, jnp.minimum((c + 1) * r8, nb8 - 1), 0)
        main = pl.BlockSpec((1, tc, LRU_W), cmap)
        halo = [pl.BlockSpec((1, SUBLANES, LRU_W), pmap), pl.BlockSpec((1, SUBLANES, LRU_W), nmap)]
        return main, halo

    common = [_full((CONV_WIDTH, LRU_W)), _full((1, LRU_W)), _full((LRU_W, 2 * LRU_W)),
              _full((1, 2 * LRU_W)), _full((1, LRU_W))]
    main, halo = specs(False)
    hf = pl.pallas_call(
        functools.partial(_lru_body, False),
        grid=(bsz, nc),
        in_specs=[main] + halo + common,
        out_specs=main,
        out_shape=jax.ShapeDtypeStruct((bsz, s, LRU_W), F32),
        scratch_shapes=[pltpu.VMEM((SUBLANES, LRU_W), F32)],
        compiler_params=_params(("parallel", "arbitrary")),
        name="lru_fwd",
    )(lx, lx, lx, cw, cb, wcat[0], bcat[0], cdec[0])
    main, halo = specs(True)
    return pl.pallas_call(
        functools.partial(_lru_body, True),
        grid=(bsz, nc),
        in_specs=[main] + halo + [main, main] + common + [_full((1, LRU_W))],
        out_specs=main,
        out_shape=jax.ShapeDtypeStruct((bsz, s, LRU_W), F32),
        scratch_shapes=[pltpu.VMEM((SUBLANES, LRU_W), F32)],
        compiler_params=_params(("parallel", "arbitrary")),
        name="lru_rev",
    )(lx, lx, lx, lgate, hf, cw, cb, wcat[1], bcat[1], cdec[1], ng)


def _ret_log_gamma():
    return np.log1p(-np.exp2(-5.0 - np.arange(RET_HEADS, dtype=np.float64)))


@functools.lru_cache(maxsize=None)
def _ret_tables(c):
    lg = np.repeat(_ret_log_gamma(), HEAD_DIM)[None, :]
    idx = np.arange(c, dtype=np.float64)[:, None]
    dec = np.stack([np.exp((idx + 1.0) * lg),
                    np.exp((c - 1.0 - idx) * lg),
                    np.exp((c - idx) * lg),
                    np.exp(idx * lg)])
    chunk = np.exp(c * lg)
    dist = np.abs(idx - idx.T)
    intra = np.exp(dist[None] * _ret_log_gamma()[:, None, None])
    lane_head = np.arange(RET_W) // HEAD_DIM
    hmask = (lane_head[None, :] == np.arange(RET_HEADS)[:, None]).astype(np.float32)[:, None, :]
    bd = (lane_head[:, None] == lane_head[None, :]).astype(np.float32)
    return (dec.astype(np.float32), chunk.astype(np.float32), intra.astype(np.float32), hmask, bd)


def _rope_tables(s):
    half = HEAD_DIM // 2
    freqs = ROPE_BASE ** (-jnp.arange(half, dtype=F32) / half)
    ang = jnp.arange(s, dtype=F32)[:, None] * freqs[None, :]
    cos = jnp.cos(ang)
    sin = jnp.sin(ang)
    cos_t = jnp.tile(jnp.concatenate([cos, cos], axis=1), (1, RET_W // HEAD_DIM))
    sin_t = jnp.tile(jnp.concatenate([-sin, sin], axis=1), (1, RET_W // HEAD_DIM))
    return cos_t, sin_t


def _rope(x, cos, sin_signed):
    lane = lax.broadcasted_iota(I32, x.shape, 1)
    w = x.shape[1]
    half = HEAD_DIM // 2
    swapped = jnp.where(lane % HEAD_DIM < half,
                        pltpu.roll(x, w - half, axis=1), pltpu.roll(x, half, axis=1))
    return x * cos + swapped * sin_signed


def _ret_body(rev, *refs):
    if rev:
        (q_ref, k_ref, v_ref, cos_ref, sin_ref, dec_ref, chunk_ref, bd_ref,
         of_ref, g_ref, ng_ref, o_ref, state_ref) = refs
    else:
        (q_ref, k_ref, v_ref, cos_ref, sin_ref, dec_ref, chunk_ref, bd_ref,
         intra_ref, hmask_ref, o_ref, state_ref) = refs

    @pl.when(pl.program_id(1) == 0)
    def _():
        state_ref[...] = jnp.zeros_like(state_ref)

    cos = cos_ref[...]
    sin = sin_ref[...]
    q = _rope(q_ref[0], cos, sin)
    k = _rope(k_ref[0], cos, sin) * (HEAD_DIM ** -0.5)
    vb = v_ref[0].astype(BF16)
    qd, kd = (2, 3) if rev else (0, 1)
    state = state_ref[...]
    cross = jnp.dot((q * dec_ref[qd]).astype(BF16), state.astype(BF16), preferred_element_type=F32)
    kv = lax.dot_general((k * dec_ref[kd]).astype(BF16), vb, (((0,), (0,)), ((), ())),
                         preferred_element_type=F32)
    state_ref[...] = state * chunk_ref[...] + kv * bd_ref[...]
    if rev:
        o = of_ref[0] + cross
        ms = _group_sum(o * o, bd_ref[...].astype(BF16)) * (1.0 / HEAD_DIM)
        o = o * lax.rsqrt(ms + EPS) * ng_ref[...]
        o_ref[0] = jax.nn.silu(g_ref[0]) * o
    else:
        kb = k.astype(BF16)
        out = cross
        for h in range(RET_HEADS):
            hm = hmask_ref[h]
            s = lax.dot_general((q * hm).astype(BF16), kb, (((1,), (1,)), ((), ())),
                                preferred_element_type=F32)
            s = (s * intra_ref[h]).astype(BF16)
            out = out + jnp.dot(s, (v_ref[0] * hm).astype(BF16), preferred_element_type=F32)
        o_ref[0] = out


def ret_mixer(rq, rk, rv, rg, ng, c=RET_CHUNK):
    bsz, s, _ = rq.shape
    c = min(c, s)
    nc = s // c
    dec, chunk, intra, hmask, bd = _ret_tables(c)
    cos_t, sin_t = _rope_tables(s)

    def specs(rev):
        cmap = (lambda b, i: (b, nc - 1 - i, 0)) if rev else (lambda b, i: (b, i, 0))
        tmap = (lambda b, i: (nc - 1 - i, 0)) if rev else (lambda b, i: (i, 0))
        main = pl.BlockSpec((1, c, RET_W), cmap)
        tab = pl.BlockSpec((c, RET_W), tmap)
        return main, tab

    consts = [_full((4, c, RET_W)), _full((1, RET_W)), _full((RET_W, RET_W))]
    main, tab = specs(False)
    of = pl.pallas_call(
        functools.partial(_ret_body, False),
        grid=(bsz, nc),
        in_specs=[main, main, main, tab, tab] + consts + [_full((RET_HEADS, c, c)), _full((RET_HEADS, 1, RET_W))],
        out_specs=main,
        out_shape=jax.ShapeDtypeStruct((bsz, s, RET_W), F32),
        scratch_shapes=[pltpu.VMEM((RET_W, RET_W), F32)],
        compiler_params=_params(("parallel", "arbitrary")),
        name="ret_fwd",
    )(rq, rk, rv, cos_t, sin_t, dec, chunk, bd, intra, hmask)
    main, tab = specs(True)
    return pl.pallas_call(
        functools.partial(_ret_body, True),
        grid=(bsz, nc),
        in_specs=[main, main, main, tab, tab] + consts + [main, main, _full((1, RET_W))],
        out_specs=main,
        out_shape=jax.ShapeDtypeStruct((bsz, s, RET_W), F32),
        scratch_shapes=[pltpu.VMEM((RET_W, RET_W), F32)],
        compiler_params=_params(("parallel", "arbitrary")),
        name="ret_rev",
    )(rq, rk, rv, cos_t, sin_t, dec, chunk, bd, of, rg, ng.reshape(1, RET_W))


HEAD_SLOT = 128
LOG2E = 1.4426950408889634


@functools.lru_cache(maxsize=None)
def _attn_consts():
    lane = np.arange(DIFF_W)
    grp = lane // DIFF_HALF
    bd32 = (grp[:, None] == grp[None, :]).astype(np.float32)
    place = np.zeros((DIFF_W, DIFF_HEADS * HEAD_SLOT), np.float32)
    place[lane, (lane // HEAD_DIM) * HEAD_SLOT + lane % HEAD_DIM] = 1.0
    ones_col = np.zeros((1, DIFF_HEADS * HEAD_SLOT), np.float32)
    ones_col[0, np.arange(DIFF_HEADS) * HEAD_SLOT + HEAD_DIM] = 1.0
    return bd32, place, ones_col


def _attn_prep_body(q_ref, k_ref, v_ref, qg_ref, kg_ref, bd_ref, place_ref, ones_ref,
                    qn_ref, kn_ref, vv_ref):
    def qk_norm(x, g):
        ms = _group_sum(x * x, bd_ref[...]) * (1.0 / DIFF_HALF)
        return x * lax.rsqrt(ms + EPS) * g

    qn_ref[0] = (qk_norm(q_ref[0], qg_ref[...]) * (DIFF_HALF ** -0.5 * LOG2E)).astype(BF16)
    kn_ref[0] = qk_norm(k_ref[0], kg_ref[...]).astype(BF16)
    vb = v_ref[0].astype(BF16)
    vv_ref[0] = (jnp.dot(vb, place_ref[...], preferred_element_type=F32) + ones_ref[...]).astype(BF16)


def attn_prep(dq, dk, dv, qg, kg, tc=ROW_TILE):
    bsz, s, _ = dq.shape
    tc = min(tc, s)
    bd32, place, ones_col = _attn_consts()
    wide = DIFF_HEADS * HEAD_SLOT
    main = pl.BlockSpec((1, tc, DIFF_W), lambda b, c: (b, c, 0))
    outb = pl.BlockSpec((1, tc, wide), lambda b, c: (b, c, 0))
    rep = DIFF_W // DIFF_HALF
    return pl.pallas_call(
        _attn_prep_body,
        grid=(bsz, s // tc),
        in_specs=[main, main, main, _full((1, DIFF_W)), _full((1, DIFF_W)),
                  _full((DIFF_W, DIFF_W)), _full((DIFF_W, wide)), _full((1, wide))],
        out_specs=[main, main, outb],
        out_shape=[jax.ShapeDtypeStruct((bsz, s, DIFF_W), BF16)] * 2 + [jax.ShapeDtypeStruct((bsz, s, wide), BF16)],
        compiler_params=_params(("parallel", "parallel")),
        name="attn_prep",
    )(dq, dk, dv, jnp.tile(qg, rep).reshape(1, DIFF_W), jnp.tile(kg, rep).reshape(1, DIFF_W),
      jnp.asarray(bd32, BF16), jnp.asarray(place, BF16), ones_col)


def _t5_bucket_np(rel):
    nb = NUM_BUCKETS // 2
    max_exact = nb // 2
    n = np.abs(rel)
    nf = np.maximum(n, 1).astype(np.float64)
    large = max_exact + np.floor(2.0 * np.log2(nf / max_exact)).astype(np.int64)
    large = np.minimum(large, nb - 1)
    return (np.where(rel > 0, nb, 0) + np.where(n < max_exact, n, large)).astype(np.int32)


def _bias_tiles(rel_bias, t, tk):
    period = t + tk
    x = np.arange(period)[None, :]
    d = np.arange(-(tk // t), 2)[:, None]
    diag = rel_bias[_t5_bucket_np(x - (t - 1) + d * t)]
    diag = jnp.transpose(diag, (2, 0, 1))
    hankel = jnp.tile(diag, (1, 1, t + 1))[:, :, :t * (period + 1)].reshape(DIFF_HEADS, d.shape[0], t, period + 1)
    near = hankel[:, :, ::-1, :tk]
    nb = NUM_BUCKETS // 2
    left = jnp.broadcast_to(rel_bias[nb - 1][:, None, None, None], (DIFF_HEADS, 1, t, tk))
    right = jnp.broadcast_to(rel_bias[NUM_BUCKETS - 1][:, None, None, None], (DIFF_HEADS, 1, t, tk))
    return jnp.concatenate([left, near, right], axis=1) * LOG2E


def _attn_body(online, q_ref, k_ref, v_ref, bias_ref, lam_ref, linit_ref, g_ref, o_ref):
    t = q_ref.shape[1]
    tk = bias_ref.shape[3]
    kq = tk // t
    nk = k_ref.shape[1] // tk
    qi = pl.program_id(2)
    lam = (jnp.exp(jnp.sum(lam_ref[0:1, :] * lam_ref[1:2, :], axis=1, keepdims=True))
           - jnp.exp(jnp.sum(lam_ref[2:3, :] * lam_ref[3:4, :], axis=1, keepdims=True))
           + linit_ref[...])
    lane = lax.broadcasted_iota(I32, (t, HEAD_SLOT), 1)
    qf = q_ref[0].astype(F32)
    half = lane // DIFF_HALF
    qs = [jnp.concatenate([jnp.where(half == 2 * hh + m, qf, 0.0) for m in range(2)], axis=0).astype(BF16)
          for hh in range(2)]

    def body(j, carry):
        rows = pl.ds(pl.multiple_of(j * tk, tk), tk)
        bidx = jnp.clip(j * kq - qi, -kq - 1, 2) + kq + 1
        kb = k_ref[0, rows, :]
        new = []
        for hh in range(2):
            lo = hh * HEAD_SLOT
            m_i, acc = carry[hh]
            vb = v_ref[0, rows, lo:lo + HEAD_SLOT]
            s = lax.dot_general(qs[hh], kb, (((1,), (1,)), ((), ())), preferred_element_type=F32)
            bt = bias_ref[hh, bidx]
            s = s + jnp.concatenate([bt, bt], axis=0)
            if online:
                m_new = jnp.maximum(m_i, jnp.max(s, axis=1, keepdims=True))
                p = jnp.exp2(s - m_new)
                acc = jnp.exp2(m_i - m_new) * acc
            else:
                m_new = m_i
                p = jnp.exp2(s)
            acc = acc + jnp.dot(p.astype(BF16), vb, preferred_element_type=F32)
            new.append((m_new, acc))
        return tuple(new)

    init = (jnp.full((2 * t, 1) if online else (1, 1), -1e30, F32), jnp.zeros((2 * t, HEAD_SLOT), F32))
    res = lax.fori_loop(0, nk, body, (init, init))

    outs = []
    for hh in range(2):
        acc = res[hh][1]
        sm = acc / acc[:, HEAD_DIM:HEAD_DIM + 1]
        o = sm[:t] - lam * sm[t:]
        o = jnp.where(lane < HEAD_DIM, o, 0.0)
        ms = jnp.sum(o * o, axis=1, keepdims=True) * (1.0 / HEAD_DIM)
        outs.append(o * lax.rsqrt(ms + EPS))
    both = jnp.where(lane < HEAD_DIM, outs[0], pltpu.roll(outs[1], HEAD_DIM, axis=1))
    o_ref[0] = both * g_ref[...] * (1.0 - linit_ref[...])


MAX_SCORE_RANGE = 96.0


def _score_bound(qg, kg, rel_bias):
    qk = DIFF_HALF * (DIFF_HALF ** -0.5 * LOG2E) * jnp.max(jnp.abs(qg)) * jnp.max(jnp.abs(kg)) * 1.02
    hi = jnp.max(rel_bias) * LOG2E
    lo = jnp.min(rel_bias) * LOG2E
    return qk + hi, 2.0 * qk + (hi - lo)


def diff_attn(qn, kn, vv, bias, bound, spread, lam_vecs, lam_init, ng):
    bsz, s, _ = qn.shape
    _, n_tiles, t, tk = bias.shape
    assert t >= MAX_DISTANCE and tk % t == 0
    qspec = pl.BlockSpec((1, t, 2 * HEAD_DIM), lambda b, h, i: (b, i, h))
    kspec = pl.BlockSpec((1, s, 2 * HEAD_DIM), lambda b, h, i: (b, 0, h))
    vspec = pl.BlockSpec((1, s, 2 * HEAD_SLOT), lambda b, h, i: (b, 0, h))
    lam_pad = jnp.zeros((4, LANES), F32).at[:, :DIFF_HALF].set(lam_vecs)
    linit = jnp.full((1, LANES), lam_init, F32)
    g2 = jnp.tile(ng, 2).reshape(1, LANES)

    def call(online, bias_tiles):
        return pl.pallas_call(
            functools.partial(_attn_body, online),
            grid=(bsz, DIFF_HEADS // 2, s // t),
            in_specs=[qspec, kspec, vspec,
                      pl.BlockSpec((2, n_tiles, t, tk), lambda b, h, i: (h, 0, 0, 0)),
                      _full((4, LANES)), _full((1, LANES)), _full((1, LANES))],
            out_specs=pl.BlockSpec((1, t, 2 * HEAD_DIM), lambda b, h, i: (b, i, h)),
            out_shape=jax.ShapeDtypeStruct((bsz, s, DIFF_W), F32),
            compiler_params=_params(("parallel", "parallel", "arbitrary")),
            name="attn_online" if online else "attn",
        )(qn, kn, vv, bias_tiles, lam_pad, linit, g2)

    return lax.cond(spread <= MAX_SCORE_RANGE,
                    lambda: call(False, bias - bound), lambda: call(True, bias))


def _out_proj_body(x_ref, yl_ref, yr_ref, yd_ref, wl_ref, wr_ref, wd_ref, g_ref, whi_ref, wlo_ref,
                   h_ref, hn_ref, pt_ref):
    h = (x_ref[...]
         + jnp.dot(yl_ref[...].astype(BF16), wl_ref[...], preferred_element_type=F32)
         + jnp.dot(yr_ref[...].astype(BF16), wr_ref[...], preferred_element_type=F32)
         + jnp.dot(yd_ref[...].astype(BF16), wd_ref[...], preferred_element_type=F32))
    h_ref[...] = h
    hn = _rms(h, g_ref[...])
    hn_hi, hn_lo = _split_bf16(hn)
    hn_ref[...] = hn_hi
    nt = (((1,), (1,)), ((), ()))
    logits_t = (lax.dot_general(whi_ref[...], hn_hi, nt, preferred_element_type=F32)
                + lax.dot_general(wlo_ref[...], hn_hi, nt, preferred_element_type=F32)
                + lax.dot_general(whi_ref[...], hn_lo, nt, preferred_element_type=F32))
    et = jnp.exp(logits_t - jnp.max(logits_t, axis=0, keepdims=True))
    pt_ref[...] = et / jnp.sum(et, axis=0, keepdims=True)


def out_proj(x2d, yl, yr, yd, w_out_bf16, g, w_router, tm=ROW_TILE):
    n = x2d.shape[0]
    tm = min(tm, n)
    row = lambda w: pl.BlockSpec((tm, w), lambda i: (i, 0))
    wr_hi, wr_lo = _split_bf16(w_router.T)
    return pl.pallas_call(
        _out_proj_body,
        grid=(n // tm,),
        in_specs=[row(D_MODEL), row(LRU_W), row(RET_W), row(DIFF_W),
                  _full((LRU_W, D_MODEL)), _full((RET_W, D_MODEL)), _full((DIFF_W, D_MODEL)),
                  _full((1, D_MODEL)), _full((N_EXPERTS, D_MODEL)), _full((N_EXPERTS, D_MODEL))],
        out_specs=[row(D_MODEL), row(D_MODEL), pl.BlockSpec((N_EXPERTS, tm), lambda i: (0, i))],
        out_shape=[jax.ShapeDtypeStruct((n, D_MODEL), F32), jax.ShapeDtypeStruct((n, D_MODEL), BF16),
                   jax.ShapeDtypeStruct((N_EXPERTS, n), F32)],
        compiler_params=_params(("parallel",)),
        name="out_proj",
    )(x2d, yl, yr, yd, w_out_bf16[:LRU_W], w_out_bf16[LRU_W:LRU_W + RET_W], w_out_bf16[LRU_W + RET_W:],
      g.reshape(1, D_MODEL), wr_hi, wr_lo)


def _row_cumsum(x01, tri_ref, nr):
    within = jnp.dot(x01.astype(F32).astype(BF16), tri_ref[...], preferred_element_type=F32).astype(I32)
    tot = jnp.broadcast_to(within[:, SEL_ROW - 1:SEL_ROW], (nr, LANES))
    r = lax.broadcasted_iota(I32, (nr, LANES), 0)
    inc = tot
    d = 1
    while d < nr:
        inc = inc + jnp.where(r >= d, pltpu.roll(inc, d, axis=0), 0)
        d *= 2
    return within + (inc - tot)[:, 0:1]


def _select_body(cap, p_ref, tri_ref, cnt_ref, sel_ref):
    p = p_ref[0]
    nr = p.shape[0]
    bits = pltpu.bitcast(p, I32)

    def body(i, prefix):
        cand = prefix | (jnp.int32(1) << (30 - i))
        cnt = jnp.sum((bits >= cand).astype(I32), keepdims=True)
        return jnp.where(cnt >= cap, cand, prefix)

    thr = lax.fori_loop(0, 31, body, jnp.zeros((1, 1), I32))
    gt = bits > thr
    eq = bits == thr
    need = cap - jnp.sum(gt.astype(I32), keepdims=True)
    eq01 = eq.astype(I32)
    rank_eq = _row_cumsum(eq01, tri_ref, nr) - eq01
    sel = jnp.where(gt, 1, jnp.where(eq & (rank_eq < need), 1, 0))
    sel_ref[0] = sel
    cnt_ref[0] = _row_cumsum(sel, tri_ref, nr)


def select(probs_t, cap):
    e, n = probs_t.shape
    nr = n // SEL_ROW
    tri = np.triu(np.ones((SEL_ROW, SEL_ROW), np.float32))
    blk = pl.BlockSpec((1, nr, SEL_ROW), lambda i: (i, 0, 0))
    return pl.pallas_call(
        functools.partial(_select_body, cap),
        grid=(e,),
        in_specs=[blk, _full((SEL_ROW, SEL_ROW))],
        out_specs=[blk, blk],
        out_shape=[jax.ShapeDtypeStruct((e, nr, SEL_ROW), I32)] * 2,
        compiler_params=_params(("parallel",)),
        name="select",
    )(probs_t.reshape(e, nr, SEL_ROW), jnp.asarray(tri, BF16))


def _ffn_body(sb_ref, row_ref, flag_ref, pos_ref, x_ref, wg_ref, wu_ref, wd_ref, o_ref, xe_ref):
    e = pl.program_id(0)
    s = pl.program_id(1)
    flags = flag_ref[e, s]
    w = xe_ref.shape[0]

    @pl.when((flags & 2) != 0)
    def _():
        xe_ref[...] = jnp.zeros_like(xe_ref)

    @pl.when((flags & 1) != 0)
    def _():
        slot = sb_ref[e, s] * w + lax.broadcasted_iota(I32, (w, 1), 0)
        onehot = jnp.where(pos_ref[0, 0] == slot, 1.0, 0.0).astype(BF16)
        xe_ref[...] += jnp.dot(onehot, x_ref[...], preferred_element_type=F32)

    @pl.when((flags & 4) != 0)
    def _():
        xe = xe_ref[...].astype(BF16)
        acc = jnp.zeros((w, D_MODEL), F32)
        for c0 in range(0, D_FF, FF_CHUNK):
            g = jnp.dot(xe, wg_ref[0, :, c0:c0 + FF_CHUNK], preferred_element_type=F32)
            u = jnp.dot(xe, wu_ref[0, :, c0:c0 + FF_CHUNK], preferred_element_type=F32)
            mid = (jax.nn.silu(g) * u).astype(BF16)
            acc = acc + jnp.dot(mid, wd_ref[0, c0:c0 + FF_CHUNK, :], preferred_element_type=F32)
        o_ref[0] = acc.astype(BF16)


def _ffn_schedule(row_end, cap, w):
    e, nr = row_end.shape
    n_sb = cap // w
    edges = jnp.broadcast_to(jnp.arange(1, n_sb + 1, dtype=I32) * w, (e, n_sb))
    stops = jnp.sort(jnp.concatenate([edges, row_end], axis=1), axis=1)
    starts = jnp.concatenate([jnp.zeros((e, 1), I32), stops[:, :-1]], axis=1)
    valid = stops > starts
    sb = jnp.minimum(starts // w, n_sb - 1)
    row = jnp.minimum(jnp.sum(row_end[:, None, :] <= starts[:, :, None], axis=2), nr - 1)
    flags = (valid.astype(I32) + 2 * (valid & (starts % w == 0)).astype(I32)
             + 4 * (valid & (stops % w == 0)).astype(I32))
    return sb.astype(I32), row.astype(I32), flags


def expert_ffn(hn_bf16, posm, row_end, wg, wu, wd, cap, w=SLOT_BLOCK):
    n = hn_bf16.shape[0]
    e, nr, _ = posm.shape
    w = min(w, cap)
    sb, row, flags = _ffn_schedule(row_end, cap, w)
    steps = sb.shape[1]
    grid_spec = pltpu.PrefetchScalarGridSpec(
        num_scalar_prefetch=3,
        grid=(e, steps),
        in_specs=[
            pl.BlockSpec((1, 1, 1, SEL_ROW), lambda i, s, sb, row, fl: (i, row[i, s], 0, 0)),
            pl.BlockSpec((SEL_ROW, D_MODEL), lambda i, s, sb, row, fl: (row[i, s], 0)),
            pl.BlockSpec((1, D_MODEL, D_FF), lambda i, s, sb, row, fl: (i, 0, 0)),
            pl.BlockSpec((1, D_MODEL, D_FF), lambda i, s, sb, row, fl: (i, 0, 0)),
            pl.BlockSpec((1, D_FF, D_MODEL), lambda i, s, sb, row, fl: (i, 0, 0)),
        ],
        out_specs=pl.BlockSpec((1, w, D_MODEL), lambda i, s, sb, row, fl: (i, sb[i, s], 0)),
        scratch_shapes=[pltpu.VMEM((w, D_MODEL), F32)],
    )
    return pl.pallas_call(
        _ffn_body,
        grid_spec=grid_spec,
        out_shape=jax.ShapeDtypeStruct((e, cap, D_MODEL), BF16),
        compiler_params=_params(("arbitrary", "arbitrary"), vmem_mb=58),
        name="ffn",
    )(sb, row, flags, posm.reshape(e, nr, 1, SEL_ROW), hn_bf16, wg, wu, wd)


def _combine_body(ws_ref, h_ref, pos_ref, gate_ref, *refs):
    ye_refs = refs[:N_EXPERTS]
    o_ref = refs[N_EXPERTS]
    i = pl.program_id(0)
    acc = h_ref[...]
    pos = pos_ref[...]
    gate = gate_ref[...]
    lane = lax.broadcasted_iota(I32, (1, COMB_WIN), 1)
    for e in range(N_EXPERTS):
        slots = ws_ref[e, i] * COMB_ALIGN + lane
        g = jnp.where(pos[:, e:e + 1] == slots, gate[:, e:e + 1], 0.0).astype(BF16)
        acc = acc + jnp.dot(g, ye_refs[e][...], preferred_element_type=F32)
    o_ref[...] = acc


def combine(h2d, pos_t, gates, ye, win_start):
    n = h2d.shape[0]
    t = COMB_TILE
    row = lambda w: pl.BlockSpec((t, w), lambda i, ws: (i, 0))

    def ye_spec(e):
        return pl.BlockSpec((None, pl.Element(COMB_WIN), pl.Element(D_MODEL)), lambda i, ws: (e, ws[e, i] * COMB_ALIGN, 0))

    grid_spec = pltpu.PrefetchScalarGridSpec(
        num_scalar_prefetch=1,
        grid=(n // t,),
        in_specs=[row(D_MODEL), row(N_EXPERTS), row(N_EXPERTS)] + [ye_spec(e) for e in range(N_EXPERTS)],
        out_specs=row(D_MODEL),
    )
    return pl.pallas_call(
        _combine_body,
        grid_spec=grid_spec,
        out_shape=jax.ShapeDtypeStruct((n, D_MODEL), F32),
        compiler_params=_params(("arbitrary",)),
        name="combine",
    )(win_start, h2d, pos_t, gates, *([ye] * N_EXPERTS))


def ec_moe(h2d, hn_bf16, probs_t, wg, wu, wd):
    n = h2d.shape[0]
    probs = probs_t.T
    cap = EC_FACTOR * n // N_EXPERTS
    cnt, sel = select(probs_t, cap)
    posm = jnp.where(sel > 0, cnt - 1, -1)
    row_end = cnt[:, :, SEL_ROW - 1]
    ye = expert_ffn(hn_bf16, posm, row_end, wg, wu, wd, cap)
    pos_t = posm.reshape(N_EXPERTS, n).T
    base = (cnt - sel).reshape(N_EXPERTS, n)[:, ::COMB_TILE]
    win_start = jnp.minimum(base // COMB_ALIGN, (cap - COMB_WIN) // COMB_ALIGN).astype(I32)
    return combine(h2d, pos_t, probs, ye, win_start)


def _block_diag(w):
    h, d, _ = w.shape
    eye = jnp.eye(h, dtype=w.dtype)
    return jnp.einsum("hde,hg->hdge", w, eye).reshape(h * d, h * d)


def _prep_layer(l, p):
    wcat = jnp.stack([jnp.concatenate([_block_diag(p["lru_wa"][l, d]), _block_diag(p["lru_wx"][l, d])], axis=1)
                      for d in range(2)]).astype(BF16)
    bcat = jnp.stack([jnp.concatenate([p["lru_ba"][l, d], p["lru_bx"][l, d]])[None, :] for d in range(2)])
    cdec = (-LRU_C * jax.nn.softplus(-p["lru_lambda"][l]))[:, None, :]
    return dict(
        w_in=p["w_in"][l].astype(BF16), w_out=p["w_out"][l].astype(BF16),
        wcat=wcat, bcat=bcat, cdec=cdec,
        wg=p["w_gate"][l].astype(BF16), wu=p["w_up"][l].astype(BF16), wd=p["w_down"][l].astype(BF16))


def _trunk(x, p, prepped):
    bsz, s, _ = x.shape
    n = bsz * s
    x2d = x.reshape(n, D_MODEL)
    bias = _bias_tiles(p["rel_bias"], min(ATTN_TILE, s), min(ATTN_KEY_TILE, s))
    for l, w in enumerate(prepped):
        lam_init = 0.8 - 0.6 * math.exp(-0.3 * l)
        pieces = in_proj(x2d, p["ln1_g"][l], w["w_in"])
        lx, lgate, rq, rk, rv, rg, dq, dk, dv = [a.reshape(bsz, s, a.shape[1]) for a in pieces]
        y_lru = lru_mixer(lx, lgate, p["conv_w"][l], p["conv_b"][l].reshape(1, LRU_W), w["wcat"], w["bcat"],
                          w["cdec"], p["lru_norm_g"][l].reshape(1, LRU_W))
        y_ret = ret_mixer(rq, rk, rv, rg, p["ret_norm_g"][l])
        qn, kn, vv = attn_prep(dq, dk, dv, p["q_norm_g"][l], p["k_norm_g"][l])
        bound, spread = _score_bound(p["q_norm_g"][l], p["k_norm_g"][l], p["rel_bias"])
        y_diff = diff_attn(qn, kn, vv, bias, bound, spread, p["diff_lambda"][l], lam_init, p["diff_norm_g"][l])
        h2d, hn, probs_t = out_proj(x2d, y_lru.reshape(n, LRU_W), y_ret.reshape(n, RET_W),
                                    y_diff.reshape(n, DIFF_W), w["w_out"], p["ln2_g"][l], p["w_router"][l])
        x2d = ec_moe(h2d, hn, probs_t, w["wg"], w["wu"], w["wd"])
    return x2d.reshape(bsz, s, D_MODEL)


def kernel(x_prompt, x_sample, rel_bias, ln1_g, ln2_g, w_in, conv_w, conv_b, lru_wa, lru_ba, lru_wx, lru_bx,
           lru_lambda, lru_norm_g, ret_norm_g, q_norm_g, k_norm_g, diff_lambda, diff_norm_g, w_out, w_router,
           w_gate, w_up, w_down):
    p = dict(rel_bias=rel_bias, ln1_g=ln1_g, ln2_g=ln2_g, w_in=w_in, conv_w=conv_w, conv_b=conv_b,
             lru_wa=lru_wa, lru_ba=lru_ba, lru_wx=lru_wx, lru_bx=lru_bx, lru_lambda=lru_lambda,
             lru_norm_g=lru_norm_g, ret_norm_g=ret_norm_g, q_norm_g=q_norm_g, k_norm_g=k_norm_g,
             diff_lambda=diff_lambda, diff_norm_g=diff_norm_g, w_out=w_out, w_router=w_router,
             w_gate=w_gate, w_up=w_up, w_down=w_down)
    prepped = [_prep_layer(l, p) for l in range(w_in.shape[0])]
    return _trunk(x_prompt, p, prepped), _trunk(x_sample, p, prepped)
```

```python
import functools
import math

import numpy as np
import jax
import jax.numpy as jnp
from jax import lax
from jax.experimental import pallas as pl
from jax.experimental.pallas import tpu as pltpu

F32 = jnp.float32
BF16 = jnp.bfloat16
I32 = jnp.int32
HIGHEST = lax.Precision.HIGHEST

D_MODEL = 1024
HEAD_DIM = 64
LRU_W = 256
LRU_HEADS = 4
RET_W = 384
RET_HEADS = 6
DIFF_W = 384
DIFF_HEADS = 6
DIFF_HALF = 32
IN_SIZES = (LRU_W, LRU_W, RET_W, RET_W, RET_W, RET_W, DIFF_W, DIFF_W, DIFF_W)
IN_WIDTH = sum(IN_SIZES)
CONV_WIDTH = 4
LRU_C = 8.0
ROPE_BASE = 10000.0
NUM_BUCKETS = 32
MAX_DISTANCE = 128
N_EXPERTS = 16
EC_FACTOR = 2
D_FF = 2816
EPS = 1e-6

V7X_VMEM_BYTES = 64 * 1024 * 1024
SUBLANES = 8
LANES = 128

ROW_TILE = 512
SCAN_CHUNK = 256
RET_CHUNK = 256
ATTN_TILE = 512
ATTN_KEY_TILE = 512
SEL_ROW = 512
SLOT_BLOCK = 256
FF_CHUNK = 1408
COMB_TILE = 256
COMB_ALIGN = 16
COMB_WIN = COMB_TILE + COMB_ALIGN
COMB_NARROW = LANES


def _params(sem, vmem_mb=48):
    return pltpu.CompilerParams(dimension_semantics=sem,
                                vmem_limit_bytes=vmem_mb * 1024 * 1024)


def _full(shape):
    nd = len(shape)
    return pl.BlockSpec(shape, lambda *_: (0,) * nd)


def _rms(x, g):
    return x * lax.rsqrt(jnp.mean(x * x, axis=-1, keepdims=True) + EPS) * g


def _split_bf16(x):
    hi = x.astype(BF16)
    return hi, (x - hi.astype(F32)).astype(BF16)


def _group_sum(x, ones_bf16):
    hi, lo = _split_bf16(x)
    return (jnp.dot(hi, ones_bf16, preferred_element_type=F32)
            + jnp.dot(lo, ones_bf16, preferred_element_type=F32))


def _in_proj_body(x_ref, g_ref, w_ref, *o_refs):
    xn = _rms(x_ref[...], g_ref[...]).astype(BF16)
    off = 0
    for o_ref, width in zip(o_refs, IN_SIZES):
        o_ref[...] = jnp.dot(xn, w_ref[:, off:off + width], preferred_element_type=F32)
        off += width


def in_proj(x2d, g, w_bf16, tm=ROW_TILE):
    n = x2d.shape[0]
    tm = min(tm, n)
    return pl.pallas_call(
        _in_proj_body,
        grid=(n // tm,),
        in_specs=[pl.BlockSpec((tm, D_MODEL), lambda i: (i, 0)),
                  _full((1, D_MODEL)), _full((D_MODEL, IN_WIDTH))],
        out_specs=[pl.BlockSpec((tm, w), lambda i: (i, 0)) for w in IN_SIZES],
        out_shape=[jax.ShapeDtypeStruct((n, w), F32) for w in IN_SIZES],
        compiler_params=_params(("parallel",)),
        name="in_proj",
    )(x2d, g.reshape(1, D_MODEL), w_bf16)


def _shift_rows(ext, s, tc):
    n = ext.shape[0]
    return pltpu.roll(ext, (-s) % n, axis=0)[SUBLANES:SUBLANES + tc]


def _neg_expm1(y):
    series = -y * (1.0 + y * (1.0 / 2) * (1.0 + y * (1.0 / 3) * (1.0 + y * (1.0 / 4) * (1.0 + y * (1.0 / 5)))))
    return jnp.where(y > -1.0 / 64, series, 1.0 - jnp.exp(y))


def _lru_scan(a, b, rev):
    tc = a.shape[0]
    t = lax.broadcasted_iota(I32, a.shape, 0)
    d = 1
    while d < tc:
        if rev:
            keep = t < tc - d
            a_o = pltpu.roll(a, tc - d, axis=0)
            b_o = pltpu.roll(b, tc - d, axis=0)
        else:
            keep = t >= d
            a_o = pltpu.roll(a, d, axis=0)
            b_o = pltpu.roll(b, d, axis=0)
        b = jnp.where(keep, a * b_o + b, b)
        a = jnp.where(keep, a * a_o, a)
        d *= 2
    return a, b


def _lru_body(rev, *refs):
    if rev:
        (x_ref, xp_ref, xn_ref, gate_ref, hf_ref, cw_ref, cb_ref, w_ref, b_ref, c_ref,
         ng_ref, o_ref, carry_ref) = refs
    else:
        (x_ref, xp_ref, xn_ref, cw_ref, cb_ref, w_ref, b_ref, c_ref, o_ref, carry_ref) = refs
    step = pl.program_id(1)
    nc = pl.num_programs(1)
    ci = nc - 1 - step if rev else step

    @pl.when(step == 0)
    def _():
        carry_ref[...] = jnp.zeros_like(carry_ref)

    x = x_ref[0]
    tc = x.shape[0]
    prev = xp_ref[0] * (ci > 0).astype(F32)
    nxt = xn_ref[0] * (ci < nc - 1).astype(F32)
    ext = jnp.concatenate([prev, x, nxt], axis=0)
    xc = cb_ref[...] + sum(cw_ref[j:j + 1, :] * _shift_rows(ext, j - CONV_WIDTH // 2, tc)
                           for j in range(CONV_WIDTH))
    z = jnp.dot(xc.astype(BF16), w_ref[...], preferred_element_type=F32) + b_ref[...]
    r = jax.nn.sigmoid(z[:, :LRU_W])
    i = jax.nn.sigmoid(z[:, LRU_W:])
    log_a = c_ref[...] * r
    a = jnp.exp(log_a)
    b = jnp.sqrt(_neg_expm1(2.0 * log_a)) * (i * xc)
    a_cum, h_loc = _lru_scan(a, b, rev)
    h = h_loc + a_cum * carry_ref[0:1, :]
    carry_ref[0:1, :] = h[0:1, :] if rev else h[tc - 1:tc, :]
    if rev:
        y = (hf_ref[0] + h) * jax.nn.gelu(gate_ref[0])
        o_ref[0] = _rms(y, ng_ref[...])
    else:
        o_ref[0] = h


def lru_mixer(lx, lgate, cw, cb, wcat, bcat, cdec, ng, tc=SCAN_CHUNK):
    bsz, s, _ = lx.shape
    tc = min(tc, s)
    nc = s // tc
    r8 = tc // SUBLANES
    nb8 = s // SUBLANES

    def specs(rev):
        cmap = (lambda b, c: (b, nc - 1 - c, 0)) if rev else (lambda b, c: (b, c, 0))
        if rev:
            pmap = lambda b, c: (b, jnp.maximum((nc - 1 - c) * r8 - 1, 0), 0)
            nmap = lambda b, c: (b, jnp.minimum((nc - c) * r8, nb8 - 1), 0)
        else:
            pmap = lambda b, c: (b, jnp.maximum(c * r8 - 1, 0), 0)
            nmap = lambda b, c: (b, jnp.minimum((c + 1) * r8, nb8 - 1), 0)
        main = pl.BlockSpec((1, tc, LRU_W), cmap)
        halo = [pl.BlockSpec((1, SUBLANES, LRU_W), pmap), pl.BlockSpec((1, SUBLANES, LRU_W), nmap)]
        return main, halo

    common = [_full((CONV_WIDTH, LRU_W)), _full((1, LRU_W)), _full((LRU_W, 2 * LRU_W)),
              _full((1, 2 * LRU_W)), _full((1, LRU_W))]
    main, halo = specs(False)
    hf = pl.pallas_call(
        functools.partial(_lru_body, False),
        grid=(bsz, nc),
        in_specs=[main] + halo + common,
        out_specs=main,
        out_shape=jax.ShapeDtypeStruct((bsz, s, LRU_W), F32),
        scratch_shapes=[pltpu.VMEM((SUBLANES, LRU_W), F32)],
        compiler_params=_params(("parallel", "arbitrary")),
        name="lru_fwd",
    )(lx, lx, lx, cw, cb, wcat[0], bcat[0], cdec[0])
    main, halo = specs(True)
    return pl.pallas_call(
        functools.partial(_lru_body, True),
        grid=(bsz, nc),
        in_specs=[main] + halo + [main, main] + common + [_full((1, LRU_W))],
        out_specs=main,
        out_shape=jax.ShapeDtypeStruct((bsz, s, LRU_W), F32),
        scratch_shapes=[pltpu.VMEM((SUBLANES, LRU_W), F32)],
        compiler_params=_params(("parallel", "arbitrary")),
        name="lru_rev",
    )(lx, lx, lx, lgate, hf, cw, cb, wcat[1], bcat[1], cdec[1], ng)


def _ret_log_gamma():
    return np.log1p(-np.exp2(-5.0 - np.arange(RET_HEADS, dtype=np.float64)))


@functools.lru_cache(maxsize=None)
def _ret_tables(c):
    lg = np.repeat(_ret_log_gamma(), HEAD_DIM)[None, :]
    idx = np.arange(c, dtype=np.float64)[:, None]
    dec = np.stack([np.exp((idx + 1.0) * lg),
                    np.exp((c - 1.0 - idx) * lg),
                    np.exp((c - idx) * lg),
                    np.exp(idx * lg)])
    chunk = np.exp(c * lg)
    dist = np.abs(idx - idx.T)
    intra = np.exp(dist[None] * _ret_log_gamma()[:, None, None])
    lane_head = np.arange(RET_W) // HEAD_DIM
    hmask = (lane_head[None, :] == np.arange(RET_HEADS)[:, None]).astype(np.float32)[:, None, :]
    bd = (lane_head[:, None] == lane_head[None, :]).astype(np.float32)
    return (dec.astype(np.float32), chunk.astype(np.float32), intra.astype(np.float32), hmask, bd)


def _rope_tables(s):
    half = HEAD_DIM // 2
    freqs = ROPE_BASE ** (-jnp.arange(half, dtype=F32) / half)
    ang = jnp.arange(s, dtype=F32)[:, None] * freqs[None, :]
    cos = jnp.cos(ang)
    sin = jnp.sin(ang)
    cos_t = jnp.tile(jnp.concatenate([cos, cos], axis=1), (1, RET_W // HEAD_DIM))
    sin_t = jnp.tile(jnp.concatenate([-sin, sin], axis=1), (1, RET_W // HEAD_DIM))
    return cos_t, sin_t


def _rope(x, cos, sin_signed):
    lane = lax.broadcasted_iota(I32, x.shape, 1)
    w = x.shape[1]
    half = HEAD_DIM // 2
    swapped = jnp.where(lane % HEAD_DIM < half,
                        pltpu.roll(x, w - half, axis=1), pltpu.roll(x, half, axis=1))
    return x * cos + swapped * sin_signed


def _ret_body(rev, *refs):
    if rev:
        (q_ref, k_ref, v_ref, cos_ref, sin_ref, dec_ref, chunk_ref, bd_ref,
         of_ref, g_ref, ng_ref, o_ref, state_ref) = refs
    else:
        (q_ref, k_ref, v_ref, cos_ref, sin_ref, dec_ref, chunk_ref, bd_ref,
         intra_ref, hmask_ref, o_ref, state_ref) = refs

    @pl.when(pl.program_id(1) == 0)
    def _():
        state_ref[...] = jnp.zeros_like(state_ref)

    cos = cos_ref[...]
    sin = sin_ref[...]
    q = _rope(q_ref[0], cos, sin)
    k = _rope(k_ref[0], cos, sin) * (HEAD_DIM ** -0.5)
    vb = v_ref[0].astype(BF16)
    qd, kd = (2, 3) if rev else (0, 1)
    state = state_ref[...]
    cross = jnp.dot((q * dec_ref[qd]).astype(BF16), state.astype(BF16), preferred_element_type=F32)
    kv = lax.dot_general((k * dec_ref[kd]).astype(BF16), vb, (((0,), (0,)), ((), ())),
                         preferred_element_type=F32)
    state_ref[...] = state * chunk_ref[...] + kv * bd_ref[...]
    if rev:
        o = of_ref[0] + cross
        ms = _group_sum(o * o, bd_ref[...].astype(BF16)) * (1.0 / HEAD_DIM)
        o = o * lax.rsqrt(ms + EPS) * ng_ref[...]
        o_ref[0] = jax.nn.silu(g_ref[0]) * o
    else:
        kb = k.astype(BF16)
        out = cross
        for h in range(RET_HEADS):
            hm = hmask_ref[h]
            s = lax.dot_general((q * hm).astype(BF16), kb, (((1,), (1,)), ((), ())),
                                preferred_element_type=F32)
            s = (s * intra_ref[h]).astype(BF16)
            out = out + jnp.dot(s, (v_ref[0] * hm).astype(BF16), preferred_element_type=F32)
        o_ref[0] = out


def ret_mixer(rq, rk, rv, rg, ng, c=RET_CHUNK):
    bsz, s, _ = rq.shape
    c = min(c, s)
    nc = s // c
    dec, chunk, intra, hmask, bd = _ret_tables(c)
    cos_t, sin_t = _rope_tables(s)

    def specs(rev):
        cmap = (lambda b, i: (b, nc - 1 - i, 0)) if rev else (lambda b, i: (b, i, 0))
        tmap = (lambda b, i: (nc - 1 - i, 0)) if rev else (lambda b, i: (i, 0))
        main = pl.BlockSpec((1, c, RET_W), cmap)
        tab = pl.BlockSpec((c, RET_W), tmap)
        return main, tab

    consts = [_full((4, c, RET_W)), _full((1, RET_W)), _full((RET_W, RET_W))]
    main, tab = specs(False)
    of = pl.pallas_call(
        functools.partial(_ret_body, False),
        grid=(bsz, nc),
        in_specs=[main, main, main, tab, tab] + consts + [_full((RET_HEADS, c, c)), _full((RET_HEADS, 1, RET_W))],
        out_specs=main,
        out_shape=jax.ShapeDtypeStruct((bsz, s, RET_W), F32),
        scratch_shapes=[pltpu.VMEM((RET_W, RET_W), F32)],
        compiler_params=_params(("parallel", "arbitrary")),
        name="ret_fwd",
    )(rq, rk, rv, cos_t, sin_t, dec, chunk, bd, intra, hmask)
    main, tab = specs(True)
    return pl.pallas_call(
        functools.partial(_ret_body, True),
        grid=(bsz, nc),
        in_specs=[main, main, main, tab, tab] + consts + [main, main, _full((1, RET_W))],
        out_specs=main,
        out_shape=jax.ShapeDtypeStruct((bsz, s, RET_W), F32),
        scratch_shapes=[pltpu.VMEM((RET_W, RET_W), F32)],
        compiler_params=_params(("parallel", "arbitrary")),
        name="ret_rev",
    )(rq, rk, rv, cos_t, sin_t, dec, chunk, bd, of, rg, ng.reshape(1, RET_W))


HEAD_SLOT = 128
LOG2E = 1.4426950408889634


@functools.lru_cache(maxsize=None)
def _attn_consts():
    lane = np.arange(DIFF_W)
    grp = lane // DIFF_HALF
    bd32 = (grp[:, None] == grp[None, :]).astype(np.float32)
    place = np.zeros((DIFF_W, DIFF_HEADS * HEAD_SLOT), np.float32)
    place[lane, (lane // HEAD_DIM) * HEAD_SLOT + lane % HEAD_DIM] = 1.0
    ones_col = np.zeros((1, DIFF_HEADS * HEAD_SLOT), np.float32)
    ones_col[0, np.arange(DIFF_HEADS) * HEAD_SLOT + HEAD_DIM] = 1.0
    return bd32, place, ones_col


def _attn_prep_body(q_ref, k_ref, v_ref, qg_ref, kg_ref, bd_ref, place_ref, ones_ref,
                    qn_ref, kn_ref, vv_ref):
    def qk_norm(x, g):
        ms = _group_sum(x * x, bd_ref[...]) * (1.0 / DIFF_HALF)
        return x * lax.rsqrt(ms + EPS) * g

    qn_ref[0] = (qk_norm(q_ref[0], qg_ref[...]) * (DIFF_HALF ** -0.5 * LOG2E)).astype(BF16)
    kn_ref[0] = qk_norm(k_ref[0], kg_ref[...]).astype(BF16)
    vb = v_ref[0].astype(BF16)
    vv_ref[0] = (jnp.dot(vb, place_ref[...], preferred_element_type=F32) + ones_ref[...]).astype(BF16)


def attn_prep(dq, dk, dv, qg, kg, tc=ROW_TILE):
    bsz, s, _ = dq.shape
    tc = min(tc, s)
    bd32, place, ones_col = _attn_consts()
    wide = DIFF_HEADS * HEAD_SLOT
    main = pl.BlockSpec((1, tc, DIFF_W), lambda b, c: (b, c, 0))
    outb = pl.BlockSpec((1, tc, wide), lambda b, c: (b, c, 0))
    rep = DIFF_W // DIFF_HALF
    return pl.pallas_call(
        _attn_prep_body,
        grid=(bsz, s // tc),
        in_specs=[main, main, main, _full((1, DIFF_W)), _full((1, DIFF_W)),
                  _full((DIFF_W, DIFF_W)), _full((DIFF_W, wide)), _full((1, wide))],
        out_specs=[main, main, outb],
        out_shape=[jax.ShapeDtypeStruct((bsz, s, DIFF_W), BF16)] * 2 + [jax.ShapeDtypeStruct((bsz, s, wide), BF16)],
        compiler_params=_params(("parallel", "parallel")),
        name="attn_prep",
    )(dq, dk, dv, jnp.tile(qg, rep).reshape(1, DIFF_W), jnp.tile(kg, rep).reshape(1, DIFF_W),
      jnp.asarray(bd32, BF16), jnp.asarray(place, BF16), ones_col)


def _t5_bucket_np(rel):
    nb = NUM_BUCKETS // 2
    max_exact = nb // 2
    n = np.abs(rel)
    nf = np.maximum(n, 1).astype(np.float64)
    large = max_exact + np.floor(2.0 * np.log2(nf / max_exact)).astype(np.int64)
    large = np.minimum(large, nb - 1)
    return (np.where(rel > 0, nb, 0) + np.where(n < max_exact, n, large)).astype(np.int32)


def _bias_tiles(rel_bias, t, tk):
    period = t + tk
    x = np.arange(period)[None, :]
    d = np.arange(-(tk // t), 2)[:, None]
    diag = rel_bias[_t5_bucket_np(x - (t - 1) + d * t)]
    diag = jnp.transpose(diag, (2, 0, 1))
    hankel = jnp.tile(diag, (1, 1, t + 1))[:, :, :t * (period + 1)].reshape(DIFF_HEADS, d.shape[0], t, period + 1)
    near = hankel[:, :, ::-1, :tk]
    nb = NUM_BUCKETS // 2
    left = jnp.broadcast_to(rel_bias[nb - 1][:, None, None, None], (DIFF_HEADS, 1, t, tk))
    right = jnp.broadcast_to(rel_bias[NUM_BUCKETS - 1][:, None, None, None], (DIFF_HEADS, 1, t, tk))
    return jnp.concatenate([left, near, right], axis=1) * LOG2E


def _attn_body(online, q_ref, k_ref, v_ref, bias_ref, lam_ref, linit_ref, g_ref, o_ref):
    t = q_ref.shape[1]
    tk = bias_ref.shape[3]
    kq = tk // t
    nk = k_ref.shape[1] // tk
    qi = pl.program_id(2)
    lam = (jnp.exp(jnp.sum(lam_ref[0:1, :] * lam_ref[1:2, :], axis=1, keepdims=True))
           - jnp.exp(jnp.sum(lam_ref[2:3, :] * lam_ref[3:4, :], axis=1, keepdims=True))
           + linit_ref[...])
    lane = lax.broadcasted_iota(I32, (t, HEAD_SLOT), 1)
    qf = q_ref[0].astype(F32)
    half = lane // DIFF_HALF
    qs = [jnp.concatenate([jnp.where(half == 2 * hh + m, qf, 0.0) for m in range(2)], axis=0).astype(BF16)
          for hh in range(2)]

    def body(j, carry):
        rows = pl.ds(pl.multiple_of(j * tk, tk), tk)
        bidx = jnp.clip(j * kq - qi, -kq - 1, 2) + kq + 1
        kb = k_ref[0, rows, :]
        new = []
        for hh in range(2):
            lo = hh * HEAD_SLOT
            m_i, acc = carry[hh]
            vb = v_ref[0, rows, lo:lo + HEAD_SLOT]
            s = lax.dot_general(qs[hh], kb, (((1,), (1,)), ((), ())), preferred_element_type=F32)
            bt = bias_ref[hh, bidx]
            s = s + jnp.concatenate([bt, bt], axis=0)
            if online:
                m_new = jnp.maximum(m_i, jnp.max(s, axis=1, keepdims=True))
                p = jnp.exp2(s - m_new)
                acc = jnp.exp2(m_i - m_new) * acc
            else:
                m_new = m_i
                p = jnp.exp2(s)
            acc = acc + jnp.dot(p.astype(BF16), vb, preferred_element_type=F32)
            new.append((m_new, acc))
        return tuple(new)

    init = (jnp.full((2 * t, 1) if online else (1, 1), -1e30, F32), jnp.zeros((2 * t, HEAD_SLOT), F32))
    res = lax.fori_loop(0, nk, body, (init, init), unroll=1 if online else 2)

    outs = []
    for hh in range(2):
        acc = res[hh][1]
        sm = acc / acc[:, HEAD_DIM:HEAD_DIM + 1]
        o = sm[:t] - lam * sm[t:]
        o = jnp.where(lane < HEAD_DIM, o, 0.0)
        ms = jnp.sum(o * o, axis=1, keepdims=True) * (1.0 / HEAD_DIM)
        outs.append(o * lax.rsqrt(ms + EPS))
    both = jnp.where(lane < HEAD_DIM, outs[0], pltpu.roll(outs[1], HEAD_DIM, axis=1))
    o_ref[0] = both * g_ref[...] * (1.0 - linit_ref[...])


MAX_SCORE_RANGE = 96.0


def _score_bound(qg, kg, rel_bias):
    qk = DIFF_HALF * (DIFF_HALF ** -0.5 * LOG2E) * jnp.max(jnp.abs(qg)) * jnp.max(jnp.abs(kg)) * 1.02
    hi = jnp.max(rel_bias) * LOG2E
    lo = jnp.min(rel_bias) * LOG2E
    return qk + hi, 2.0 * qk + (hi - lo)


def diff_attn(qn, kn, vv, bias, bound, spread, lam_vecs, lam_init, ng):
    bsz, s, _ = qn.shape
    _, n_tiles, t, tk = bias.shape
    assert t >= MAX_DISTANCE and tk % t == 0
    qspec = pl.BlockSpec((1, t, 2 * HEAD_DIM), lambda b, h, i: (b, i, h))
    kspec = pl.BlockSpec((1, s, 2 * HEAD_DIM), lambda b, h, i: (b, 0, h))
    vspec = pl.BlockSpec((1, s, 2 * HEAD_SLOT), lambda b, h, i: (b, 0, h))
    lam_pad = jnp.zeros((4, LANES), F32).at[:, :DIFF_HALF].set(lam_vecs)
    linit = jnp.full((1, LANES), lam_init, F32)
    g2 = jnp.tile(ng, 2).reshape(1, LANES)

    def call(online, bias_tiles):
        return pl.pallas_call(
            functools.partial(_attn_body, online),
            grid=(bsz, DIFF_HEADS // 2, s // t),
            in_specs=[qspec, kspec, vspec,
                      pl.BlockSpec((2, n_tiles, t, tk), lambda b, h, i: (h, 0, 0, 0)),
                      _full((4, LANES)), _full((1, LANES)), _full((1, LANES))],
            out_specs=pl.BlockSpec((1, t, 2 * HEAD_DIM), lambda b, h, i: (b, i, h)),
            out_shape=jax.ShapeDtypeStruct((bsz, s, DIFF_W), F32),
            compiler_params=_params(("parallel", "parallel", "arbitrary")),
            name="attn_online" if online else "attn",
        )(qn, kn, vv, bias_tiles, lam_pad, linit, g2)

    return lax.cond(spread <= MAX_SCORE_RANGE,
                    lambda: call(False, bias - bound), lambda: call(True, bias))


def _out_proj_body(x_ref, yl_ref, yr_ref, yd_ref, wl_ref, wr_ref, wd_ref, g_ref, whi_ref, wlo_ref,
                   h_ref, hn_ref, pt_ref):
    h = (x_ref[...]
         + jnp.dot(yl_ref[...].astype(BF16), wl_ref[...], preferred_element_type=F32)
         + jnp.dot(yr_ref[...].astype(BF16), wr_ref[...], preferred_element_type=F32)
         + jnp.dot(yd_ref[...].astype(BF16), wd_ref[...], preferred_element_type=F32))
    h_ref[...] = h
    hn = _rms(h, g_ref[...])
    hn_hi, hn_lo = _split_bf16(hn)
    hn_ref[...] = hn_hi
    nt = (((1,), (1,)), ((), ()))
    logits_t = (lax.dot_general(whi_ref[...], hn_hi, nt, preferred_element_type=F32)
                + lax.dot_general(wlo_ref[...], hn_hi, nt, preferred_element_type=F32)
                + lax.dot_general(whi_ref[...], hn_lo, nt, preferred_element_type=F32))
    et = jnp.exp(logits_t - jnp.max(logits_t, axis=0, keepdims=True))
    pt_ref[...] = et / jnp.sum(et, axis=0, keepdims=True)


def out_proj(x2d, yl, yr, yd, w_out_bf16, g, w_router, tm=ROW_TILE):
    n = x2d.shape[0]
    tm = min(tm, n)
    row = lambda w: pl.BlockSpec((tm, w), lambda i: (i, 0))
    wr_hi, wr_lo = _split_bf16(w_router.T)
    return pl.pallas_call(
        _out_proj_body,
        grid=(n // tm,),
        in_specs=[row(D_MODEL), row(LRU_W), row(RET_W), row(DIFF_W),
                  _full((LRU_W, D_MODEL)), _full((RET_W, D_MODEL)), _full((DIFF_W, D_MODEL)),
                  _full((1, D_MODEL)), _full((N_EXPERTS, D_MODEL)), _full((N_EXPERTS, D_MODEL))],
        out_specs=[row(D_MODEL), row(D_MODEL), pl.BlockSpec((N_EXPERTS, tm), lambda i: (0, i))],
        out_shape=[jax.ShapeDtypeStruct((n, D_MODEL), F32), jax.ShapeDtypeStruct((n, D_MODEL), BF16),
                   jax.ShapeDtypeStruct((N_EXPERTS, n), F32)],
        compiler_params=_params(("parallel",)),
        name="out_proj",
    )(x2d, yl, yr, yd, w_out_bf16[:LRU_W], w_out_bf16[LRU_W:LRU_W + RET_W], w_out_bf16[LRU_W + RET_W:],
      g.reshape(1, D_MODEL), wr_hi, wr_lo)


def _row_cumsum(x01, tri_ref, nr):
    within = jnp.dot(x01.astype(F32).astype(BF16), tri_ref[...], preferred_element_type=F32).astype(I32)
    tot = jnp.broadcast_to(within[:, SEL_ROW - 1:SEL_ROW], (nr, LANES))
    r = lax.broadcasted_iota(I32, (nr, LANES), 0)
    inc = tot
    d = 1
    while d < nr:
        inc = inc + jnp.where(r >= d, pltpu.roll(inc, d, axis=0), 0)
        d *= 2
    return within + (inc - tot)[:, 0:1]


def _select_body(cap, p_ref, tri_ref, cnt_ref, sel_ref):
    p = p_ref[0]
    nr = p.shape[0]
    bits = pltpu.bitcast(p, I32)

    def body(i, prefix):
        cand = prefix | (jnp.int32(1) << (30 - i))
        cnt = jnp.sum((bits >= cand).astype(I32), keepdims=True)
        return jnp.where(cnt >= cap, cand, prefix)

    thr = lax.fori_loop(0, 31, body, jnp.zeros((1, 1), I32))
    gt = bits > thr
    eq = bits == thr
    need = cap - jnp.sum(gt.astype(I32), keepdims=True)
    eq01 = eq.astype(I32)
    rank_eq = _row_cumsum(eq01, tri_ref, nr) - eq01
    sel = jnp.where(gt, 1, jnp.where(eq & (rank_eq < need), 1, 0))
    sel_ref[0] = sel
    cnt_ref[0] = _row_cumsum(sel, tri_ref, nr)


def select(probs_t, cap):
    e, n = probs_t.shape
    nr = n // SEL_ROW
    tri = np.triu(np.ones((SEL_ROW, SEL_ROW), np.float32))
    blk = pl.BlockSpec((1, nr, SEL_ROW), lambda i: (i, 0, 0))
    return pl.pallas_call(
        functools.partial(_select_body, cap),
        grid=(e,),
        in_specs=[blk, _full((SEL_ROW, SEL_ROW))],
        out_specs=[blk, blk],
        out_shape=[jax.ShapeDtypeStruct((e, nr, SEL_ROW), I32)] * 2,
        compiler_params=_params(("parallel",)),
        name="select",
    )(probs_t.reshape(e, nr, SEL_ROW), jnp.asarray(tri, BF16))


def _ffn_body(sb_ref, row_ref, flag_ref, pos_ref, x_ref, wg_ref, wu_ref, wd_ref, o_ref, xe_ref):
    e = pl.program_id(0)
    s = pl.program_id(1)
    flags = flag_ref[e, s]
    w = xe_ref.shape[0]

    @pl.when((flags & 2) != 0)
    def _():
        xe_ref[...] = jnp.zeros_like(xe_ref)

    @pl.when((flags & 1) != 0)
    def _():
        slot = sb_ref[e, s] * w + lax.broadcasted_iota(I32, (w, 1), 0)
        onehot = jnp.where(pos_ref[0, 0] == slot, 1.0, 0.0).astype(BF16)
        xe_ref[...] += jnp.dot(onehot, x_ref[...], preferred_element_type=F32)

    @pl.when((flags & 4) != 0)
    def _():
        xe = xe_ref[...].astype(BF16)
        acc = jnp.zeros((w, D_MODEL), F32)
        for c0 in range(0, D_FF, FF_CHUNK):
            g = jnp.dot(xe, wg_ref[0, :, c0:c0 + FF_CHUNK], preferred_element_type=F32)
            u = jnp.dot(xe, wu_ref[0, :, c0:c0 + FF_CHUNK], preferred_element_type=F32)
            mid = (jax.nn.silu(g) * u).astype(BF16)
            acc = acc + jnp.dot(mid, wd_ref[0, c0:c0 + FF_CHUNK, :], preferred_element_type=F32)
        o_ref[0] = acc.astype(BF16)


def _ffn_schedule(row_end, cap, w):
    e, nr = row_end.shape
    n_sb = cap // w
    edges = jnp.broadcast_to(jnp.arange(1, n_sb + 1, dtype=I32) * w, (e, n_sb))
    stops = jnp.sort(jnp.concatenate([edges, row_end], axis=1), axis=1)
    starts = jnp.concatenate([jnp.zeros((e, 1), I32), stops[:, :-1]], axis=1)
    valid = stops > starts
    sb = jnp.minimum(starts // w, n_sb - 1)
    row = jnp.minimum(jnp.sum(row_end[:, None, :] <= starts[:, :, None], axis=2), nr - 1)
    flags = (valid.astype(I32) + 2 * (valid & (starts % w == 0)).astype(I32)
             + 4 * (valid & (stops % w == 0)).astype(I32))
    return sb.astype(I32), row.astype(I32), flags


def expert_ffn(hn_bf16, posm, row_end, wg, wu, wd, cap, w=SLOT_BLOCK):
    n = hn_bf16.shape[0]
    e, nr, _ = posm.shape
    w = min(w, cap)
    sb, row, flags = _ffn_schedule(row_end, cap, w)
    steps = sb.shape[1]
    grid_spec = pltpu.PrefetchScalarGridSpec(
        num_scalar_prefetch=3,
        grid=(e, steps),
        in_specs=[
            pl.BlockSpec((1, 1, 1, SEL_ROW), lambda i, s, sb, row, fl: (i, row[i, s], 0, 0)),
            pl.BlockSpec((SEL_ROW, D_MODEL), lambda i, s, sb, row, fl: (row[i, s], 0)),
            pl.BlockSpec((1, D_MODEL, D_FF), lambda i, s, sb, row, fl: (i, 0, 0)),
            pl.BlockSpec((1, D_MODEL, D_FF), lambda i, s, sb, row, fl: (i, 0, 0)),
            pl.BlockSpec((1, D_FF, D_MODEL), lambda i, s, sb, row, fl: (i, 0, 0)),
        ],
        out_specs=pl.BlockSpec((1, w, D_MODEL), lambda i, s, sb, row, fl: (i, sb[i, s], 0)),
        scratch_shapes=[pltpu.VMEM((w, D_MODEL), F32)],
    )
    return pl.pallas_call(
        _ffn_body,
        grid_spec=grid_spec,
        out_shape=jax.ShapeDtypeStruct((e, cap, D_MODEL), BF16),
        compiler_params=_params(("arbitrary", "arbitrary"), vmem_mb=58),
        name="ffn",
    )(sb, row, flags, posm.reshape(e, nr, 1, SEL_ROW), hn_bf16, wg, wu, wd)


def _combine_body(ns_ref, ws_ref, wide_ref, h_ref, pos_ref, gate_ref, *refs):
    narrow_refs = refs[:N_EXPERTS]
    wide_refs = refs[N_EXPERTS:2 * N_EXPERTS]
    o_ref = refs[2 * N_EXPERTS]
    i = pl.program_id(0)
    pos = pos_ref[...]
    gate = gate_ref[...]
    wide = wide_ref[i]

    @pl.when(wide == 0)
    def _():
        lane = lax.broadcasted_iota(I32, (1, COMB_NARROW), 1)
        blocks = []
        for e in range(N_EXPERTS):
            slots = ns_ref[e, i] * COMB_ALIGN + lane
            blocks.append(jnp.where(pos[:, e:e + 1] == slots, gate[:, e:e + 1], 0.0).astype(BF16))
        g_all = jnp.concatenate(blocks, axis=1)
        rows = jnp.concatenate([r[...] for r in narrow_refs], axis=0)
        o_ref[...] = h_ref[...] + jnp.dot(g_all, rows, preferred_element_type=F32)

    @pl.when(wide != 0)
    def _():
        acc = h_ref[...]
        lane = lax.broadcasted_iota(I32, (1, COMB_WIN), 1)
        for e in range(N_EXPERTS):
            slots = ws_ref[e, i] * COMB_ALIGN + lane
            g = jnp.where(pos[:, e:e + 1] == slots, gate[:, e:e + 1], 0.0).astype(BF16)
            acc = acc + jnp.dot(g, wide_refs[e][...], preferred_element_type=F32)
        o_ref[...] = acc


def combine(h2d, pos_t, gates, ye, narrow_start, wide_start, wide_flag):
    n = h2d.shape[0]
    t = COMB_TILE
    row = lambda w: pl.BlockSpec((t, w), lambda i, ns, ws, fl: (i, 0))

    def narrow_spec(e):
        return pl.BlockSpec((None, pl.Element(COMB_NARROW), pl.Element(D_MODEL)),
                            lambda i, ns, ws, fl: (e, ns[e, i] * COMB_ALIGN, 0))

    def wide_spec(e):
        return pl.BlockSpec((None, pl.Element(COMB_WIN), pl.Element(D_MODEL)),
                            lambda i, ns, ws, fl: (e, ws[e, i] * COMB_ALIGN, 0))

    grid_spec = pltpu.PrefetchScalarGridSpec(
        num_scalar_prefetch=3,
        grid=(n // t,),
        in_specs=([row(D_MODEL), row(N_EXPERTS), row(N_EXPERTS)]
                  + [narrow_spec(e) for e in range(N_EXPERTS)] + [wide_spec(e) for e in range(N_EXPERTS)]),
        out_specs=row(D_MODEL),
    )
    return pl.pallas_call(
        _combine_body,
        grid_spec=grid_spec,
        out_shape=jax.ShapeDtypeStruct((n, D_MODEL), F32),
        compiler_params=_params(("arbitrary",)),
        name="combine",
    )(narrow_start, wide_start, wide_flag, h2d, pos_t, gates, *([ye] * (2 * N_EXPERTS)))


def ec_moe(h2d, hn_bf16, probs_t, wg, wu, wd):
    n = h2d.shape[0]
    probs = probs_t.T
    cap = EC_FACTOR * n // N_EXPERTS
    cnt, sel = select(probs_t, cap)
    posm = jnp.where(sel > 0, cnt - 1, -1)
    row_end = cnt[:, :, SEL_ROW - 1]
    ye = expert_ffn(hn_bf16, posm, row_end, wg, wu, wd, cap)
    pos_t = posm.reshape(N_EXPERTS, n).T
    base = (cnt - sel).reshape(N_EXPERTS, n)[:, ::COMB_TILE]
    stop = jnp.concatenate([base[:, 1:], jnp.full((N_EXPERTS, 1), cap, I32)], axis=1)
    narrow_start = jnp.minimum(base // COMB_ALIGN, (cap - COMB_NARROW) // COMB_ALIGN).astype(I32)
    wide_flag = jnp.any(stop > narrow_start * COMB_ALIGN + COMB_NARROW, axis=0)
    wide_start = jnp.where(wide_flag[None, :], jnp.minimum(base // COMB_ALIGN, (cap - COMB_WIN) // COMB_ALIGN), 0)
    return combine(h2d, pos_t, probs, ye, narrow_start, wide_start.astype(I32), wide_flag.astype(I32))


def _block_diag(w):
    h, d, _ = w.shape
    eye = jnp.eye(h, dtype=w.dtype)
    return jnp.einsum("hde,hg->hdge", w, eye).reshape(h * d, h * d)


def _prep_layer(l, p):
    wcat = jnp.stack([jnp.concatenate([_block_diag(p["lru_wa"][l, d]), _block_diag(p["lru_wx"][l, d])], axis=1)
                      for d in range(2)]).astype(BF16)
    bcat = jnp.stack([jnp.concatenate([p["lru_ba"][l, d], p["lru_bx"][l, d]])[None, :] for d in range(2)])
    cdec = (-LRU_C * jax.nn.softplus(-p["lru_lambda"][l]))[:, None, :]
    return dict(
        w_in=p["w_in"][l].astype(BF16), w_out=p["w_out"][l].astype(BF16),
        wcat=wcat, bcat=bcat, cdec=cdec,
        wg=p["w_gate"][l].astype(BF16), wu=p["w_up"][l].astype(BF16), wd=p["w_down"][l].astype(BF16))


def _trunk(x, p, prepped):
    bsz, s, _ = x.shape
    n = bsz * s
    x2d = x.reshape(n, D_MODEL)
    bias = _bias_tiles(p["rel_bias"], min(ATTN_TILE, s), min(ATTN_KEY_TILE, s))
    for l, w in enumerate(prepped):
        lam_init = 0.8 - 0.6 * math.exp(-0.3 * l)
        pieces = in_proj(x2d, p["ln1_g"][l], w["w_in"])
        lx, lgate, rq, rk, rv, rg, dq, dk, dv = [a.reshape(bsz, s, a.shape[1]) for a in pieces]
        y_lru = lru_mixer(lx, lgate, p["conv_w"][l], p["conv_b"][l].reshape(1, LRU_W), w["wcat"], w["bcat"],
                          w["cdec"], p["lru_norm_g"][l].reshape(1, LRU_W))
        y_ret = ret_mixer(rq, rk, rv, rg, p["ret_norm_g"][l])
        qn, kn, vv = attn_prep(dq, dk, dv, p["q_norm_g"][l], p["k_norm_g"][l])
        bound, spread = _score_bound(p["q_norm_g"][l], p["k_norm_g"][l], p["rel_bias"])
        y_diff = diff_attn(qn, kn, vv, bias, bound, spread, p["diff_lambda"][l], lam_init, p["diff_norm_g"][l])
        h2d, hn, probs_t = out_proj(x2d, y_lru.reshape(n, LRU_W), y_ret.reshape(n, RET_W),
                                    y_diff.reshape(n, DIFF_W), w["w_out"], p["ln2_g"][l], p["w_router"][l])
        x2d = ec_moe(h2d, hn, probs_t, w["wg"], w["wu"], w["wd"])
    return x2d.reshape(bsz, s, D_MODEL)


def kernel(x_prompt, x_sample, rel_bias, ln1_g, ln2_g, w_in, conv_w, conv_b, lru_wa, lru_ba, lru_wx, lru_bx,
           lru_lambda, lru_norm_g, ret_norm_g, q_norm_g, k_norm_g, diff_lambda, diff_norm_g, w_out, w_router,
           w_gate, w_up, w_down):
    p = dict(rel_bias=rel_bias, ln1_g=ln1_g, ln2_g=ln2_g, w_in=w_in, conv_w=conv_w, conv_b=conv_b,
             lru_wa=lru_wa, lru_ba=lru_ba, lru_wx=lru_wx, lru_bx=lru_bx, lru_lambda=lru_lambda,
             lru_norm_g=lru_norm_g, ret_norm_g=ret_norm_g, q_norm_g=q_norm_g, k_norm_g=k_norm_g,
             diff_lambda=diff_lambda, diff_norm_g=diff_norm_g, w_out=w_out, w_router=w_router,
             w_gate=w_gate, w_up=w_up, w_down=w_down)
    prepped = [_prep_layer(l, p) for l in range(w_in.shape[0])]
    return _trunk(x_prompt, p, prepped), _trunk(x_sample, p, prepped)
```

```python
import functools
import math

import numpy as np
import jax
import jax.numpy as jnp
from jax import lax
from jax.experimental import pallas as pl
from jax.experimental.pallas import tpu as pltpu

F32 = jnp.float32
BF16 = jnp.bfloat16
I32 = jnp.int32
HIGHEST = lax.Precision.HIGHEST

D_MODEL = 1024
HEAD_DIM = 64
LRU_W = 256
LRU_HEADS = 4
RET_W = 384
RET_HEADS = 6
DIFF_W = 384
DIFF_HEADS = 6
DIFF_HALF = 32
IN_SIZES = (LRU_W, LRU_W, RET_W, RET_W, RET_W, RET_W, DIFF_W, DIFF_W, DIFF_W)
IN_WIDTH = sum(IN_SIZES)
CONV_WIDTH = 4
LRU_C = 8.0
ROPE_BASE = 10000.0
NUM_BUCKETS = 32
MAX_DISTANCE = 128
N_EXPERTS = 16
EC_FACTOR = 2
D_FF = 2816
EPS = 1e-6

V7X_VMEM_BYTES = 64 * 1024 * 1024
SUBLANES = 8
LANES = 128

ROW_TILE = 512
SCAN_CHUNK = 256
RET_CHUNK = 256
ATTN_TILE = 512
ATTN_KEY_TILE = 512
SEL_ROW = 512
SLOT_BLOCK = 256
FF_CHUNK = 1408
COMB_TILE = 256
COMB_ALIGN = 16
COMB_WIN = COMB_TILE + COMB_ALIGN
COMB_NARROW = LANES


def _params(sem, vmem_mb=48):
    return pltpu.CompilerParams(dimension_semantics=sem,
                                vmem_limit_bytes=vmem_mb * 1024 * 1024)


def _full(shape):
    nd = len(shape)
    return pl.BlockSpec(shape, lambda *_: (0,) * nd)


def _rms(x, g):
    return x * lax.rsqrt(jnp.mean(x * x, axis=-1, keepdims=True) + EPS) * g


def _split_bf16(x):
    hi = x.astype(BF16)
    return hi, (x - hi.astype(F32)).astype(BF16)


def _group_sum(x, ones_bf16):
    hi, lo = _split_bf16(x)
    return (jnp.dot(hi, ones_bf16, preferred_element_type=F32)
            + jnp.dot(lo, ones_bf16, preferred_element_type=F32))


def _in_proj_body(x_ref, g_ref, w_ref, *o_refs):
    xn = _rms(x_ref[...], g_ref[...]).astype(BF16)
    off = 0
    for o_ref, width in zip(o_refs, IN_SIZES):
        o_ref[...] = jnp.dot(xn, w_ref[:, off:off + width], preferred_element_type=F32)
        off += width


def in_proj(x2d, g, w_bf16, tm=ROW_TILE):
    n = x2d.shape[0]
    tm = min(tm, n)
    return pl.pallas_call(
        _in_proj_body,
        grid=(n // tm,),
        in_specs=[pl.BlockSpec((tm, D_MODEL), lambda i: (i, 0)),
                  _full((1, D_MODEL)), _full((D_MODEL, IN_WIDTH))],
        out_specs=[pl.BlockSpec((tm, w), lambda i: (i, 0)) for w in IN_SIZES],
        out_shape=[jax.ShapeDtypeStruct((n, w), F32) for w in IN_SIZES],
        compiler_params=_params(("parallel",)),
        name="in_proj",
    )(x2d, g.reshape(1, D_MODEL), w_bf16)


def _shift_rows(ext, s, tc):
    n = ext.shape[0]
    return pltpu.roll(ext, (-s) % n, axis=0)[SUBLANES:SUBLANES + tc]


def _neg_expm1(y):
    series = -y * (1.0 + y * (1.0 / 2) * (1.0 + y * (1.0 / 3) * (1.0 + y * (1.0 / 4) * (1.0 + y * (1.0 / 5)))))
    return jnp.where(y > -1.0 / 64, series, 1.0 - jnp.exp(y))


def _lru_scan(a, b, rev):
    tc = a.shape[0]
    t = lax.broadcasted_iota(I32, a.shape, 0)
    d = 1
    while d < tc:
        if rev:
            keep = t < tc - d
            a_o = pltpu.roll(a, tc - d, axis=0)
            b_o = pltpu.roll(b, tc - d, axis=0)
        else:
            keep = t >= d
            a_o = pltpu.roll(a, d, axis=0)
            b_o = pltpu.roll(b, d, axis=0)
        b = jnp.where(keep, a * b_o + b, b)
        a = jnp.where(keep, a * a_o, a)
        d *= 2
    return a, b


def _lru_body(rev, *refs):
    if rev:
        (x_ref, xp_ref, xn_ref, gate_ref, hf_ref, cw_ref, cb_ref, w_ref, b_ref, c_ref,
         ng_ref, o_ref, carry_ref) = refs
    else:
        (x_ref, xp_ref, xn_ref, cw_ref, cb_ref, w_ref, b_ref, c_ref, o_ref, carry_ref) = refs
    step = pl.program_id(1)
    nc = pl.num_programs(1)
    ci = nc - 1 - step if rev else step

    @pl.when(step == 0)
    def _():
        carry_ref[...] = jnp.zeros_like(carry_ref)

    x = x_ref[0]
    tc = x.shape[0]
    prev = xp_ref[0] * (ci > 0).astype(F32)
    nxt = xn_ref[0] * (ci < nc - 1).astype(F32)
    ext = jnp.concatenate([prev, x, nxt], axis=0)
    xc = cb_ref[...] + sum(cw_ref[j:j + 1, :] * _shift_rows(ext, j - CONV_WIDTH // 2, tc)
                           for j in range(CONV_WIDTH))
    z = jnp.dot(xc.astype(BF16), w_ref[...], preferred_element_type=F32) + b_ref[...]
    r = jax.nn.sigmoid(z[:, :LRU_W])
    i = jax.nn.sigmoid(z[:, LRU_W:])
    log_a = c_ref[...] * r
    a = jnp.exp(log_a)
    b = jnp.sqrt(_neg_expm1(2.0 * log_a)) * (i * xc)
    a_cum, h_loc = _lru_scan(a, b, rev)
    h = h_loc + a_cum * carry_ref[0:1, :]
    carry_ref[0:1, :] = h[0:1, :] if rev else h[tc - 1:tc, :]
    if rev:
        y = (hf_ref[0] + h) * jax.nn.gelu(gate_ref[0])
        o_ref[0] = _rms(y, ng_ref[...])
    else:
        o_ref[0] = h


def lru_mixer(lx, lgate, cw, cb, wcat, bcat, cdec, ng, tc=SCAN_CHUNK):
    bsz, s, _ = lx.shape
    tc = min(tc, s)
    nc = s // tc
    r8 = tc // SUBLANES
    nb8 = s // SUBLANES

    def specs(rev):
        cmap = (lambda b, c: (b, nc - 1 - c, 0)) if rev else (lambda b, c: (b, c, 0))
        if rev:
            pmap = lambda b, c: (b, jnp.maximum((nc - 1 - c) * r8 - 1, 0), 0)
            nmap = lambda b, c: (b, jnp.minimum((nc - c) * r8, nb8 - 1), 0)
        else:
            pmap = lambda b, c: (b, jnp.maximum(c * r8 - 1, 0), 0)
            nmap = lambda b, c: (b, jnp.minimum((c + 1) * r8, nb8 - 1), 0)
        main = pl.BlockSpec((1, tc, LRU_W), cmap)
        halo = [pl.BlockSpec((1, SUBLANES, LRU_W), pmap), pl.BlockSpec((1, SUBLANES, LRU_W), nmap)]
        return main, halo

    common = [_full((CONV_WIDTH, LRU_W)), _full((1, LRU_W)), _full((LRU_W, 2 * LRU_W)),
              _full((1, 2 * LRU_W)), _full((1, LRU_W))]
    main, halo = specs(False)
    hf = pl.pallas_call(
        functools.partial(_lru_body, False),
        grid=(bsz, nc),
        in_specs=[main] + halo + common,
        out_specs=main,
        out_shape=jax.ShapeDtypeStruct((bsz, s, LRU_W), F32),
        scratch_shapes=[pltpu.VMEM((SUBLANES, LRU_W), F32)],
        compiler_params=_params(("parallel", "arbitrary")),
        name="lru_fwd",
    )(lx, lx, lx, cw, cb, wcat[0], bcat[0], cdec[0])
    main, halo = specs(True)
    return pl.pallas_call(
        functools.partial(_lru_body, True),
        grid=(bsz, nc),
        in_specs=[main] + halo + [main, main] + common + [_full((1, LRU_W))],
        out_specs=main,
        out_shape=jax.ShapeDtypeStruct((bsz, s, LRU_W), F32),
        scratch_shapes=[pltpu.VMEM((SUBLANES, LRU_W), F32)],
        compiler_params=_params(("parallel", "arbitrary")),
        name="lru_rev",
    )(lx, lx, lx, lgate, hf, cw, cb, wcat[1], bcat[1], cdec[1], ng)


def _ret_log_gamma():
    return np.log1p(-np.exp2(-5.0 - np.arange(RET_HEADS, dtype=np.float64)))


@functools.lru_cache(maxsize=None)
def _ret_tables(c):
    lg = np.repeat(_ret_log_gamma(), HEAD_DIM)[None, :]
    idx = np.arange(c, dtype=np.float64)[:, None]
    dec = np.stack([np.exp((idx + 1.0) * lg),
                    np.exp((c - 1.0 - idx) * lg),
                    np.exp((c - idx) * lg),
                    np.exp(idx * lg)])
    chunk = np.exp(c * lg)
    dist = np.abs(idx - idx.T)
    intra = np.exp(dist[None] * _ret_log_gamma()[:, None, None])
    lane_head = np.arange(RET_W) // HEAD_DIM
    hmask = (lane_head[None, :] == np.arange(RET_HEADS)[:, None]).astype(np.float32)[:, None, :]
    bd = (lane_head[:, None] == lane_head[None, :]).astype(np.float32)
    return (dec.astype(np.float32), chunk.astype(np.float32), intra.astype(np.float32), hmask, bd)


def _rope_tables(s):
    half = HEAD_DIM // 2
    freqs = ROPE_BASE ** (-jnp.arange(half, dtype=F32) / half)
    ang = jnp.arange(s, dtype=F32)[:, None] * freqs[None, :]
    cos = jnp.cos(ang)
    sin = jnp.sin(ang)
    cos_t = jnp.tile(jnp.concatenate([cos, cos], axis=1), (1, RET_W // HEAD_DIM))
    sin_t = jnp.tile(jnp.concatenate([-sin, sin], axis=1), (1, RET_W // HEAD_DIM))
    return cos_t, sin_t


def _rope(x, cos, sin_signed):
    lane = lax.broadcasted_iota(I32, x.shape, 1)
    w = x.shape[1]
    half = HEAD_DIM // 2
    swapped = jnp.where(lane % HEAD_DIM < half,
                        pltpu.roll(x, w - half, axis=1), pltpu.roll(x, half, axis=1))
    return x * cos + swapped * sin_signed


def _ret_body(rev, *refs):
    if rev:
        (q_ref, k_ref, v_ref, cos_ref, sin_ref, dec_ref, chunk_ref, bd_ref,
         of_ref, g_ref, ng_ref, o_ref, state_ref) = refs
    else:
        (q_ref, k_ref, v_ref, cos_ref, sin_ref, dec_ref, chunk_ref, bd_ref,
         intra_ref, hmask_ref, o_ref, state_ref) = refs

    @pl.when(pl.program_id(1) == 0)
    def _():
        state_ref[...] = jnp.zeros_like(state_ref)

    cos = cos_ref[...]
    sin = sin_ref[...]
    q = _rope(q_ref[0], cos, sin)
    k = _rope(k_ref[0], cos, sin) * (HEAD_DIM ** -0.5)
    vb = v_ref[0].astype(BF16)
    qd, kd = (2, 3) if rev else (0, 1)
    state = state_ref[...]
    cross = jnp.dot((q * dec_ref[qd]).astype(BF16), state.astype(BF16), preferred_element_type=F32)
    kv = lax.dot_general((k * dec_ref[kd]).astype(BF16), vb, (((0,), (0,)), ((), ())),
                         preferred_element_type=F32)
    state_ref[...] = state * chunk_ref[...] + kv * bd_ref[...]
    if rev:
        o = of_ref[0] + cross
        ms = _group_sum(o * o, bd_ref[...].astype(BF16)) * (1.0 / HEAD_DIM)
        o = o * lax.rsqrt(ms + EPS) * ng_ref[...]
        o_ref[0] = jax.nn.silu(g_ref[0]) * o
    else:
        kb = k.astype(BF16)
        out = cross
        for h in range(RET_HEADS):
            hm = hmask_ref[h]
            s = lax.dot_general((q * hm).astype(BF16), kb, (((1,), (1,)), ((), ())),
                                preferred_element_type=F32)
            s = (s * intra_ref[h]).astype(BF16)
            out = out + jnp.dot(s, (v_ref[0] * hm).astype(BF16), preferred_element_type=F32)
        o_ref[0] = out


def ret_mixer(rq, rk, rv, rg, ng, c=RET_CHUNK):
    bsz, s, _ = rq.shape
    c = min(c, s)
    nc = s // c
    dec, chunk, intra, hmask, bd = _ret_tables(c)
    cos_t, sin_t = _rope_tables(s)

    def specs(rev):
        cmap = (lambda b, i: (b, nc - 1 - i, 0)) if rev else (lambda b, i: (b, i, 0))
        tmap = (lambda b, i: (nc - 1 - i, 0)) if rev else (lambda b, i: (i, 0))
        main = pl.BlockSpec((1, c, RET_W), cmap)
        tab = pl.BlockSpec((c, RET_W), tmap)
        return main, tab

    consts = [_full((4, c, RET_W)), _full((1, RET_W)), _full((RET_W, RET_W))]
    main, tab = specs(False)
    of = pl.pallas_call(
        functools.partial(_ret_body, False),
        grid=(bsz, nc),
        in_specs=[main, main, main, tab, tab] + consts + [_full((RET_HEADS, c, c)), _full((RET_HEADS, 1, RET_W))],
        out_specs=main,
        out_shape=jax.ShapeDtypeStruct((bsz, s, RET_W), F32),
        scratch_shapes=[pltpu.VMEM((RET_W, RET_W), F32)],
        compiler_params=_params(("parallel", "arbitrary")),
        name="ret_fwd",
    )(rq, rk, rv, cos_t, sin_t, dec, chunk, bd, intra, hmask)
    main, tab = specs(True)
    return pl.pallas_call(
        functools.partial(_ret_body, True),
        grid=(bsz, nc),
        in_specs=[main, main, main, tab, tab] + consts + [main, main, _full((1, RET_W))],
        out_specs=main,
        out_shape=jax.ShapeDtypeStruct((bsz, s, RET_W), F32),
        scratch_shapes=[pltpu.VMEM((RET_W, RET_W), F32)],
        compiler_params=_params(("parallel", "arbitrary")),
        name="ret_rev",
    )(rq, rk, rv, cos_t, sin_t, dec, chunk, bd, of, rg, ng.reshape(1, RET_W))


HEAD_SLOT = 128
LOG2E = 1.4426950408889634


@functools.lru_cache(maxsize=None)
def _attn_consts():
    lane = np.arange(DIFF_W)
    grp = lane // DIFF_HALF
    bd32 = (grp[:, None] == grp[None, :]).astype(np.float32)
    place = np.zeros((DIFF_W, DIFF_HEADS * HEAD_SLOT), np.float32)
    place[lane, (lane // HEAD_DIM) * HEAD_SLOT + lane % HEAD_DIM] = 1.0
    ones_col = np.zeros((1, DIFF_HEADS * HEAD_SLOT), np.float32)
    ones_col[0, np.arange(DIFF_HEADS) * HEAD_SLOT + HEAD_DIM] = 1.0
    return bd32, place, ones_col


def _attn_prep_body(q_ref, k_ref, v_ref, qg_ref, kg_ref, bd_ref, place_ref, ones_ref,
                    qn_ref, kn_ref, vv_ref):
    def qk_norm(x, g):
        ms = _group_sum(x * x, bd_ref[...]) * (1.0 / DIFF_HALF)
        return x * lax.rsqrt(ms + EPS) * g

    qn_ref[0] = (qk_norm(q_ref[0], qg_ref[...]) * (DIFF_HALF ** -0.5 * LOG2E)).astype(BF16)
    kn_ref[0] = qk_norm(k_ref[0], kg_ref[...]).astype(BF16)
    vb = v_ref[0].astype(BF16)
    vv_ref[0] = (jnp.dot(vb, place_ref[...], preferred_element_type=F32) + ones_ref[...]).astype(BF16)


def attn_prep(dq, dk, dv, qg, kg, tc=ROW_TILE):
    bsz, s, _ = dq.shape
    tc = min(tc, s)
    bd32, place, ones_col = _attn_consts()
    wide = DIFF_HEADS * HEAD_SLOT
    main = pl.BlockSpec((1, tc, DIFF_W), lambda b, c: (b, c, 0))
    outb = pl.BlockSpec((1, tc, wide), lambda b, c: (b, c, 0))
    rep = DIFF_W // DIFF_HALF
    return pl.pallas_call(
        _attn_prep_body,
        grid=(bsz, s // tc),
        in_specs=[main, main, main, _full((1, DIFF_W)), _full((1, DIFF_W)),
                  _full((DIFF_W, DIFF_W)), _full((DIFF_W, wide)), _full((1, wide))],
        out_specs=[main, main, outb],
        out_shape=[jax.ShapeDtypeStruct((bsz, s, DIFF_W), BF16)] * 2 + [jax.ShapeDtypeStruct((bsz, s, wide), BF16)],
        compiler_params=_params(("parallel", "parallel")),
        name="attn_prep",
    )(dq, dk, dv, jnp.tile(qg, rep).reshape(1, DIFF_W), jnp.tile(kg, rep).reshape(1, DIFF_W),
      jnp.asarray(bd32, BF16), jnp.asarray(place, BF16), ones_col)


def _t5_bucket_np(rel):
    nb = NUM_BUCKETS // 2
    max_exact = nb // 2
    n = np.abs(rel)
    nf = np.maximum(n, 1).astype(np.float64)
    large = max_exact + np.floor(2.0 * np.log2(nf / max_exact)).astype(np.int64)
    large = np.minimum(large, nb - 1)
    return (np.where(rel > 0, nb, 0) + np.where(n < max_exact, n, large)).astype(np.int32)


def _bias_tiles(rel_bias, t, tk):
    period = t + tk
    x = np.arange(period)[None, :]
    d = np.arange(-(tk // t), 2)[:, None]
    diag = rel_bias[_t5_bucket_np(x - (t - 1) + d * t)]
    diag = jnp.transpose(diag, (2, 0, 1))
    hankel = jnp.tile(diag, (1, 1, t + 1))[:, :, :t * (period + 1)].reshape(DIFF_HEADS, d.shape[0], t, period + 1)
    near = hankel[:, :, ::-1, :tk]
    nb = NUM_BUCKETS // 2
    left = jnp.broadcast_to(rel_bias[nb - 1][:, None, None, None], (DIFF_HEADS, 1, t, tk))
    right = jnp.broadcast_to(rel_bias[NUM_BUCKETS - 1][:, None, None, None], (DIFF_HEADS, 1, t, tk))
    return jnp.concatenate([left, near, right], axis=1) * LOG2E


def _attn_body(online, q_ref, k_ref, v_ref, bias_ref, lam_ref, linit_ref, g_ref, o_ref):
    t = q_ref.shape[1]
    tk = bias_ref.shape[3]
    kq = tk // t
    nk = k_ref.shape[1] // tk
    qi = pl.program_id(2)
    lam = (jnp.exp(jnp.sum(lam_ref[0:1, :] * lam_ref[1:2, :], axis=1, keepdims=True))
           - jnp.exp(jnp.sum(lam_ref[2:3, :] * lam_ref[3:4, :], axis=1, keepdims=True))
           + linit_ref[...])
    lane = lax.broadcasted_iota(I32, (t, HEAD_SLOT), 1)
    qf = q_ref[0].astype(F32)
    half = lane // DIFF_HALF
    qs = [jnp.concatenate([jnp.where(half == 2 * hh + m, qf, 0.0) for m in range(2)], axis=0).astype(BF16)
          for hh in range(2)]

    def body(j, carry):
        rows = pl.ds(pl.multiple_of(j * tk, tk), tk)
        bidx = jnp.clip(j * kq - qi, -kq - 1, 2) + kq + 1
        kb = k_ref[0, rows, :]
        new = []
        for hh in range(2):
            lo = hh * HEAD_SLOT
            m_i, acc = carry[hh]
            vb = v_ref[0, rows, lo:lo + HEAD_SLOT]
            s = lax.dot_general(qs[hh], kb, (((1,), (1,)), ((), ())), preferred_element_type=F32)
            bt = bias_ref[hh, bidx]
            s = s + jnp.concatenate([bt, bt], axis=0)
            if online:
                m_new = jnp.maximum(m_i, jnp.max(s, axis=1, keepdims=True))
                p = jnp.exp2(s - m_new)
                acc = jnp.exp2(m_i - m_new) * acc
            else:
                m_new = m_i
                p = jnp.exp2(s)
            acc = acc + jnp.dot(p.astype(BF16), vb, preferred_element_type=F32)
            new.append((m_new, acc))
        return tuple(new)

    init = (jnp.full((2 * t, 1) if online else (1, 1), -1e30, F32), jnp.zeros((2 * t, HEAD_SLOT), F32))
    res = lax.fori_loop(0, nk, body, (init, init), unroll=1 if online else 2)

    outs = []
    for hh in range(2):
        acc = res[hh][1]
        sm = acc / acc[:, HEAD_DIM:HEAD_DIM + 1]
        o = sm[:t] - lam * sm[t:]
        o = jnp.where(lane < HEAD_DIM, o, 0.0)
        ms = jnp.sum(o * o, axis=1, keepdims=True) * (1.0 / HEAD_DIM)
        outs.append(o * lax.rsqrt(ms + EPS))
    both = jnp.where(lane < HEAD_DIM, outs[0], pltpu.roll(outs[1], HEAD_DIM, axis=1))
    o_ref[0] = both * g_ref[...] * (1.0 - linit_ref[...])


MAX_SCORE_RANGE = 96.0


def _score_bound(qg, kg, rel_bias):
    qk = DIFF_HALF * (DIFF_HALF ** -0.5 * LOG2E) * jnp.max(jnp.abs(qg)) * jnp.max(jnp.abs(kg)) * 1.02
    hi = jnp.max(rel_bias) * LOG2E
    lo = jnp.min(rel_bias) * LOG2E
    return qk + hi, 2.0 * qk + (hi - lo)


def diff_attn(qn, kn, vv, bias, bound, spread, lam_vecs, lam_init, ng):
    bsz, s, _ = qn.shape
    _, n_tiles, t, tk = bias.shape
    assert t >= MAX_DISTANCE and tk % t == 0
    qspec = pl.BlockSpec((1, t, 2 * HEAD_DIM), lambda b, h, i: (b, i, h))
    kspec = pl.BlockSpec((1, s, 2 * HEAD_DIM), lambda b, h, i: (b, 0, h))
    vspec = pl.BlockSpec((1, s, 2 * HEAD_SLOT), lambda b, h, i: (b, 0, h))
    lam_pad = jnp.zeros((4, LANES), F32).at[:, :DIFF_HALF].set(lam_vecs)
    linit = jnp.full((1, LANES), lam_init, F32)
    g2 = jnp.tile(ng, 2).reshape(1, LANES)

    def call(online, bias_tiles):
        return pl.pallas_call(
            functools.partial(_attn_body, online),
            grid=(bsz, DIFF_HEADS // 2, s // t),
            in_specs=[qspec, kspec, vspec,
                      pl.BlockSpec((2, n_tiles, t, tk), lambda b, h, i: (h, 0, 0, 0)),
                      _full((4, LANES)), _full((1, LANES)), _full((1, LANES))],
            out_specs=pl.BlockSpec((1, t, 2 * HEAD_DIM), lambda b, h, i: (b, i, h)),
            out_shape=jax.ShapeDtypeStruct((bsz, s, DIFF_W), F32),
            compiler_params=_params(("parallel", "parallel", "arbitrary")),
            name="attn_online" if online else "attn",
        )(qn, kn, vv, bias_tiles, lam_pad, linit, g2)

    return lax.cond(spread <= MAX_SCORE_RANGE,
                    lambda: call(False, bias - bound), lambda: call(True, bias))


def _out_proj_body(x_ref, yl_ref, yr_ref, yd_ref, wl_ref, wr_ref, wd_ref, g_ref, whi_ref, wlo_ref,
                   h_ref, hn_ref, pt_ref):
    h = (x_ref[...]
         + jnp.dot(yl_ref[...].astype(BF16), wl_ref[...], preferred_element_type=F32)
         + jnp.dot(yr_ref[...].astype(BF16), wr_ref[...], preferred_element_type=F32)
         + jnp.dot(yd_ref[...].astype(BF16), wd_ref[...], preferred_element_type=F32))
    h_ref[...] = h
    hn = _rms(h, g_ref[...])
    hn_ref[...] = hn
    hn_hi, hn_lo = _split_bf16(hn)
    nt = (((1,), (1,)), ((), ()))
    logits_t = (lax.dot_general(whi_ref[...], hn_hi, nt, preferred_element_type=F32)
                + lax.dot_general(wlo_ref[...], hn_hi, nt, preferred_element_type=F32)
                + lax.dot_general(whi_ref[...], hn_lo, nt, preferred_element_type=F32))
    et = jnp.exp(logits_t - jnp.max(logits_t, axis=0, keepdims=True))
    pt_ref[...] = et / jnp.sum(et, axis=0, keepdims=True)


def out_proj(x2d, yl, yr, yd, w_out_bf16, g, w_router, tm=ROW_TILE):
    n = x2d.shape[0]
    tm = min(tm, n)
    row = lambda w: pl.BlockSpec((tm, w), lambda i: (i, 0))
    wr_hi, wr_lo = _split_bf16(w_router.T)
    return pl.pallas_call(
        _out_proj_body,
        grid=(n // tm,),
        in_specs=[row(D_MODEL), row(LRU_W), row(RET_W), row(DIFF_W),
                  _full((LRU_W, D_MODEL)), _full((RET_W, D_MODEL)), _full((DIFF_W, D_MODEL)),
                  _full((1, D_MODEL)), _full((N_EXPERTS, D_MODEL)), _full((N_EXPERTS, D_MODEL))],
        out_specs=[row(D_MODEL), row(D_MODEL), pl.BlockSpec((N_EXPERTS, tm), lambda i: (0, i))],
        out_shape=[jax.ShapeDtypeStruct((n, D_MODEL), F32), jax.ShapeDtypeStruct((n, D_MODEL), F32),
                   jax.ShapeDtypeStruct((N_EXPERTS, n), F32)],
        compiler_params=_params(("parallel",)),
        name="out_proj",
    )(x2d, yl, yr, yd, w_out_bf16[:LRU_W], w_out_bf16[LRU_W:LRU_W + RET_W], w_out_bf16[LRU_W + RET_W:],
      g.reshape(1, D_MODEL), wr_hi, wr_lo)


def _row_cumsum(x01, tri_ref, nr):
    within = jnp.dot(x01.astype(F32).astype(BF16), tri_ref[...], preferred_element_type=F32).astype(I32)
    tot = jnp.broadcast_to(within[:, SEL_ROW - 1:SEL_ROW], (nr, LANES))
    r = lax.broadcasted_iota(I32, (nr, LANES), 0)
    inc = tot
    d = 1
    while d < nr:
        inc = inc + jnp.where(r >= d, pltpu.roll(inc, d, axis=0), 0)
        d *= 2
    return within + (inc - tot)[:, 0:1]


def _select_body(cap, p_ref, tri_ref, cnt_ref, sel_ref):
    p = p_ref[0]
    nr = p.shape[0]
    bits = pltpu.bitcast(p, I32)

    def body(i, prefix):
        cand = prefix | (jnp.int32(1) << (30 - i))
        cnt = jnp.sum((bits >= cand).astype(I32), keepdims=True)
        return jnp.where(cnt >= cap, cand, prefix)

    thr = lax.fori_loop(0, 31, body, jnp.zeros((1, 1), I32))
    gt = bits > thr
    eq = bits == thr
    need = cap - jnp.sum(gt.astype(I32), keepdims=True)
    eq01 = eq.astype(I32)
    rank_eq = _row_cumsum(eq01, tri_ref, nr) - eq01
    sel = jnp.where(gt, 1, jnp.where(eq & (rank_eq < need), 1, 0))
    sel_ref[0] = sel
    cnt_ref[0] = _row_cumsum(sel, tri_ref, nr)


def select(probs_t, cap):
    e, n = probs_t.shape
    nr = n // SEL_ROW
    tri = np.triu(np.ones((SEL_ROW, SEL_ROW), np.float32))
    blk = pl.BlockSpec((1, nr, SEL_ROW), lambda i: (i, 0, 0))
    return pl.pallas_call(
        functools.partial(_select_body, cap),
        grid=(e,),
        in_specs=[blk, _full((SEL_ROW, SEL_ROW))],
        out_specs=[blk, blk],
        out_shape=[jax.ShapeDtypeStruct((e, nr, SEL_ROW), I32)] * 2,
        compiler_params=_params(("parallel",)),
        name="select",
    )(probs_t.reshape(e, nr, SEL_ROW), jnp.asarray(tri, BF16))


def _slot_index_body(lo_ref, hi_ref, pos_ref, o_ref):
    e = pl.program_id(0)
    n_sb, parts, w = o_ref.shape[1:]
    tok = lax.broadcasted_iota(I32, (parts, SEL_ROW), 1)
    part = lax.broadcasted_iota(I32, (parts, SEL_ROW), 0)
    base = jnp.where(part == 0, tok // 256, jnp.where(part == 1, tok % 256, 0))
    slot0 = lax.broadcasted_iota(I32, (w, 1), 0)

    def block(sb, carry):
        def visit(r, acc):
            onehot = jnp.where(pos_ref[0, pl.ds(r, 1), :] == slot0 + sb * w, 1.0, 0.0).astype(BF16)
            payload = jnp.where(part == 2, r, base).astype(F32).astype(BF16)
            return acc + lax.dot_general(payload, onehot, (((1,), (1,)), ((), ())), preferred_element_type=F32)

        o_ref[0, sb] = lax.fori_loop(lo_ref[e, sb], hi_ref[e, sb] + 1, visit, jnp.zeros((parts, w), F32))
        return carry

    lax.fori_loop(0, n_sb, block, 0)


def slot_index(posm, row_end, cap, w):
    e, nr, _ = posm.shape
    assert nr <= 256 and SEL_ROW == 512
    n_sb = cap // w
    edges = jnp.arange(n_sb + 1, dtype=I32) * w
    lo = jnp.sum(row_end[:, None, :] <= edges[None, :-1, None], axis=2)
    hi = jnp.minimum(jnp.sum(row_end[:, None, :] < edges[None, 1:, None], axis=2), nr - 1)
    parts = pl.pallas_call(
        _slot_index_body,
        grid_spec=pltpu.PrefetchScalarGridSpec(
            num_scalar_prefetch=2,
            grid=(e,),
            in_specs=[pl.BlockSpec((1, nr, SEL_ROW), lambda i, lo, hi: (i, 0, 0))],
            out_specs=pl.BlockSpec((1, n_sb, SUBLANES, w), lambda i, lo, hi: (i, 0, 0, 0)),
        ),
        out_shape=jax.ShapeDtypeStruct((e, n_sb, SUBLANES, w), F32),
        compiler_params=_params(("arbitrary",)),
        name="slot_index",
    )(lo.astype(I32), hi.astype(I32), posm)
    idx = parts[:, :, 2] * SEL_ROW + parts[:, :, 0] * 256 + parts[:, :, 1]
    return idx.astype(I32).reshape(e * n_sb, 1, w)


def _ffn_body(idx_ref, idx_next_ref, x_hbm, wg_ref, wu_ref, wd_ref, o_ref, xbuf, sem):
    n_sb = pl.num_programs(1)
    step = pl.program_id(0) * n_sb + pl.program_id(1)
    n_steps = pl.num_programs(0) * n_sb
    half = step % 2
    w = xbuf.shape[1]

    def issue(idx_block, dst_half):
        def body(r, carry):
            pltpu.make_async_copy(x_hbm.at[pl.ds(idx_block[0, 0, r], 1)], xbuf.at[dst_half, pl.ds(r, 1)],
                                  sem.at[dst_half]).start()
            return carry

        lax.fori_loop(0, w, body, 0, unroll=8)

    @pl.when(step == 0)
    def _():
        issue(idx_ref, 0)

    @pl.when(step + 1 < n_steps)
    def _():
        issue(idx_next_ref, 1 - half)

    for r in range(w):
        pltpu.make_async_copy(x_hbm.at[pl.ds(0, 1)], xbuf.at[half, pl.ds(r, 1)], sem.at[half]).wait()

    xe = xbuf[half].astype(BF16)
    acc = jnp.zeros((w, D_MODEL), F32)
    for c0 in range(0, D_FF, FF_CHUNK):
        g = jnp.dot(xe, wg_ref[0, :, c0:c0 + FF_CHUNK], preferred_element_type=F32)
        u = jnp.dot(xe, wu_ref[0, :, c0:c0 + FF_CHUNK], preferred_element_type=F32)
        mid = (jax.nn.silu(g) * u).astype(BF16)
        acc = acc + jnp.dot(mid, wd_ref[0, c0:c0 + FF_CHUNK, :], preferred_element_type=F32)
    o_ref[0] = acc.astype(BF16)


def expert_ffn(hn, idx, wg, wu, wd, cap, w):
    e = wg.shape[0]
    n_sb = cap // w
    last = e * n_sb - 1
    smem_block = lambda index_map: pl.BlockSpec((1, 1, w), index_map, memory_space=pltpu.SMEM)
    return pl.pallas_call(
        _ffn_body,
        grid=(e, n_sb),
        in_specs=[
            smem_block(lambda i, s: (i * n_sb + s, 0, 0)),
            smem_block(lambda i, s: (jnp.minimum(i * n_sb + s + 1, last), 0, 0)),
            pl.BlockSpec(memory_space=pl.ANY),
            pl.BlockSpec((1, D_MODEL, D_FF), lambda i, s: (i, 0, 0)),
            pl.BlockSpec((1, D_MODEL, D_FF), lambda i, s: (i, 0, 0)),
            pl.BlockSpec((1, D_FF, D_MODEL), lambda i, s: (i, 0, 0)),
        ],
        out_specs=pl.BlockSpec((1, w, D_MODEL), lambda i, s: (i, s, 0)),
        out_shape=jax.ShapeDtypeStruct((e, cap, D_MODEL), BF16),
        scratch_shapes=[pltpu.VMEM((2, w, D_MODEL), F32), pltpu.SemaphoreType.DMA((2,))],
        compiler_params=_params(("arbitrary", "arbitrary"), vmem_mb=58),
        name="ffn",
    )(idx, idx, hn, wg, wu, wd)


def _combine_body(ns_ref, ws_ref, wide_ref, h_ref, pos_ref, gate_ref, *refs):
    narrow_refs = refs[:N_EXPERTS]
    wide_refs = refs[N_EXPERTS:2 * N_EXPERTS]
    o_ref = refs[2 * N_EXPERTS]
    i = pl.program_id(0)
    pos = pos_ref[...]
    gate = gate_ref[...]
    wide = wide_ref[i]

    @pl.when(wide == 0)
    def _():
        lane = lax.broadcasted_iota(I32, (1, COMB_NARROW), 1)
        blocks = []
        for e in range(N_EXPERTS):
            slots = ns_ref[e, i] * COMB_ALIGN + lane
            blocks.append(jnp.where(pos[:, e:e + 1] == slots, gate[:, e:e + 1], 0.0).astype(BF16))
        g_all = jnp.concatenate(blocks, axis=1)
        rows = jnp.concatenate([r[...] for r in narrow_refs], axis=0)
        o_ref[...] = h_ref[...] + jnp.dot(g_all, rows, preferred_element_type=F32)

    @pl.when(wide != 0)
    def _():
        acc = h_ref[...]
        lane = lax.broadcasted_iota(I32, (1, COMB_WIN), 1)
        for e in range(N_EXPERTS):
            slots = ws_ref[e, i] * COMB_ALIGN + lane
            g = jnp.where(pos[:, e:e + 1] == slots, gate[:, e:e + 1], 0.0).astype(BF16)
            acc = acc + jnp.dot(g, wide_refs[e][...], preferred_element_type=F32)
        o_ref[...] = acc


def combine(h2d, pos_t, gates, ye, narrow_start, wide_start, wide_flag):
    n = h2d.shape[0]
    t = COMB_TILE
    row = lambda w: pl.BlockSpec((t, w), lambda i, ns, ws, fl: (i, 0))

    def narrow_spec(e):
        return pl.BlockSpec((None, pl.Element(COMB_NARROW), pl.Element(D_MODEL)),
                            lambda i, ns, ws, fl: (e, ns[e, i] * COMB_ALIGN, 0))

    def wide_spec(e):
        return pl.BlockSpec((None, pl.Element(COMB_WIN), pl.Element(D_MODEL)),
                            lambda i, ns, ws, fl: (e, ws[e, i] * COMB_ALIGN, 0))

    grid_spec = pltpu.PrefetchScalarGridSpec(
        num_scalar_prefetch=3,
        grid=(n // t,),
        in_specs=([row(D_MODEL), row(N_EXPERTS), row(N_EXPERTS)]
                  + [narrow_spec(e) for e in range(N_EXPERTS)] + [wide_spec(e) for e in range(N_EXPERTS)]),
        out_specs=row(D_MODEL),
    )
    return pl.pallas_call(
        _combine_body,
        grid_spec=grid_spec,
        out_shape=jax.ShapeDtypeStruct((n, D_MODEL), F32),
        compiler_params=_params(("arbitrary",)),
        name="combine",
    )(narrow_start, wide_start, wide_flag, h2d, pos_t, gates, *([ye] * (2 * N_EXPERTS)))


def ec_moe(h2d, hn, probs_t, wg, wu, wd):
    n = h2d.shape[0]
    probs = probs_t.T
    cap = EC_FACTOR * n // N_EXPERTS
    cnt, sel = select(probs_t, cap)
    posm = jnp.where(sel > 0, cnt - 1, -1)
    row_end = cnt[:, :, SEL_ROW - 1]
    w = min(SLOT_BLOCK, cap)
    ye = expert_ffn(hn, slot_index(posm, row_end, cap, w), wg, wu, wd, cap, w)
    pos_t = posm.reshape(N_EXPERTS, n).T
    base = (cnt - sel).reshape(N_EXPERTS, n)[:, ::COMB_TILE]
    stop = jnp.concatenate([base[:, 1:], jnp.full((N_EXPERTS, 1), cap, I32)], axis=1)
    narrow_start = jnp.minimum(base // COMB_ALIGN, (cap - COMB_NARROW) // COMB_ALIGN).astype(I32)
    wide_flag = jnp.any(stop > narrow_start * COMB_ALIGN + COMB_NARROW, axis=0)
    wide_start = jnp.where(wide_flag[None, :], jnp.minimum(base // COMB_ALIGN, (cap - COMB_WIN) // COMB_ALIGN), 0)
    return combine(h2d, pos_t, probs, ye, narrow_start, wide_start.astype(I32), wide_flag.astype(I32))


def _block_diag(w):
    h, d, _ = w.shape
    eye = jnp.eye(h, dtype=w.dtype)
    return jnp.einsum("hde,hg->hdge", w, eye).reshape(h * d, h * d)


def _prep_layer(l, p):
    wcat = jnp.stack([jnp.concatenate([_block_diag(p["lru_wa"][l, d]), _block_diag(p["lru_wx"][l, d])], axis=1)
                      for d in range(2)]).astype(BF16)
    bcat = jnp.stack([jnp.concatenate([p["lru_ba"][l, d], p["lru_bx"][l, d]])[None, :] for d in range(2)])
    cdec = (-LRU_C * jax.nn.softplus(-p["lru_lambda"][l]))[:, None, :]
    return dict(
        w_in=p["w_in"][l].astype(BF16), w_out=p["w_out"][l].astype(BF16),
        wcat=wcat, bcat=bcat, cdec=cdec,
        wg=p["w_gate"][l].astype(BF16), wu=p["w_up"][l].astype(BF16), wd=p["w_down"][l].astype(BF16))


def _trunk(x, p, prepped):
    bsz, s, _ = x.shape
    n = bsz * s
    x2d = x.reshape(n, D_MODEL)
    bias = _bias_tiles(p["rel_bias"], min(ATTN_TILE, s), min(ATTN_KEY_TILE, s))
    for l, w in enumerate(prepped):
        lam_init = 0.8 - 0.6 * math.exp(-0.3 * l)
        pieces = in_proj(x2d, p["ln1_g"][l], w["w_in"])
        lx, lgate, rq, rk, rv, rg, dq, dk, dv = [a.reshape(bsz, s, a.shape[1]) for a in pieces]
        y_lru = lru_mixer(lx, lgate, p["conv_w"][l], p["conv_b"][l].reshape(1, LRU_W), w["wcat"], w["bcat"],
                          w["cdec"], p["lru_norm_g"][l].reshape(1, LRU_W))
        y_ret = ret_mixer(rq, rk, rv, rg, p["ret_norm_g"][l])
        qn, kn, vv = attn_prep(dq, dk, dv, p["q_norm_g"][l], p["k_norm_g"][l])
        bound, spread = _score_bound(p["q_norm_g"][l], p["k_norm_g"][l], p["rel_bias"])
        y_diff = diff_attn(qn, kn, vv, bias, bound, spread, p["diff_lambda"][l], lam_init, p["diff_norm_g"][l])
        h2d, hn, probs_t = out_proj(x2d, y_lru.reshape(n, LRU_W), y_ret.reshape(n, RET_W),
                                    y_diff.reshape(n, DIFF_W), w["w_out"], p["ln2_g"][l], p["w_router"][l])
        x2d = ec_moe(h2d, hn, probs_t, w["wg"], w["wu"], w["wd"])
    return x2d.reshape(bsz, s, D_MODEL)


def kernel(x_prompt, x_sample, rel_bias, ln1_g, ln2_g, w_in, conv_w, conv_b, lru_wa, lru_ba, lru_wx, lru_bx,
           lru_lambda, lru_norm_g, ret_norm_g, q_norm_g, k_norm_g, diff_lambda, diff_norm_g, w_out, w_router,
           w_gate, w_up, w_down):
    p = dict(rel_bias=rel_bias, ln1_g=ln1_g, ln2_g=ln2_g, w_in=w_in, conv_w=conv_w, conv_b=conv_b,
             lru_wa=lru_wa, lru_ba=lru_ba, lru_wx=lru_wx, lru_bx=lru_bx, lru_lambda=lru_lambda,
             lru_norm_g=lru_norm_g, ret_norm_g=ret_norm_g, q_norm_g=q_norm_g, k_norm_g=k_norm_g,
             diff_lambda=diff_lambda, diff_norm_g=diff_norm_g, w_out=w_out, w_router=w_router,
             w_gate=w_gate, w_up=w_up, w_down=w_down)
    prepped = [_prep_layer(l, p) for l in range(w_in.shape[0])]
    return _trunk(x_prompt, p, prepped), _trunk(x_sample, p, prepped)
```

```python
import functools
import math

import numpy as np
import jax
import jax.numpy as jnp
from jax import lax
from jax.experimental import pallas as pl
from jax.experimental.pallas import tpu as pltpu

F32 = jnp.float32
BF16 = jnp.bfloat16
I32 = jnp.int32
HIGHEST = lax.Precision.HIGHEST

D_MODEL = 1024
HEAD_DIM = 64
LRU_W = 256
LRU_HEADS = 4
RET_W = 384
RET_HEADS = 6
DIFF_W = 384
DIFF_HEADS = 6
DIFF_HALF = 32
IN_SIZES = (LRU_W, LRU_W, RET_W, RET_W, RET_W, RET_W, DIFF_W, DIFF_W, DIFF_W)
IN_WIDTH = sum(IN_SIZES)
CONV_WIDTH = 4
LRU_C = 8.0
ROPE_BASE = 10000.0
NUM_BUCKETS = 32
MAX_DISTANCE = 128
N_EXPERTS = 16
EC_FACTOR = 2
D_FF = 2816
EPS = 1e-6

V7X_VMEM_BYTES = 64 * 1024 * 1024
SUBLANES = 8
LANES = 128

ROW_TILE = 512
SCAN_CHUNK = 256
RET_CHUNK = 256
ATTN_TILE = 512
ATTN_KEY_TILE = 512
SEL_ROW = 512
SLOT_BLOCK = 256
SLOT_INDEX_ROWS = 4
FF_CHUNK = 1408
COMB_TILE = 256
COMB_ALIGN = 16
COMB_WIN = COMB_TILE + COMB_ALIGN
COMB_NARROW = LANES


def _params(sem, vmem_mb=48):
    return pltpu.CompilerParams(dimension_semantics=sem,
                                vmem_limit_bytes=vmem_mb * 1024 * 1024)


def _full(shape):
    nd = len(shape)
    return pl.BlockSpec(shape, lambda *_: (0,) * nd)


def _rms(x, g):
    return x * lax.rsqrt(jnp.mean(x * x, axis=-1, keepdims=True) + EPS) * g


def _split_bf16(x):
    hi = x.astype(BF16)
    return hi, (x - hi.astype(F32)).astype(BF16)


def _group_sum(x, ones_bf16):
    hi, lo = _split_bf16(x)
    return (jnp.dot(hi, ones_bf16, preferred_element_type=F32)
            + jnp.dot(lo, ones_bf16, preferred_element_type=F32))


def _in_proj_body(x_ref, g_ref, w_ref, *o_refs):
    xn = _rms(x_ref[...], g_ref[...]).astype(BF16)
    off = 0
    for o_ref, width in zip(o_refs, IN_SIZES):
        o_ref[...] = jnp.dot(xn, w_ref[:, off:off + width], preferred_element_type=F32)
        off += width


def in_proj(x2d, g, w_bf16, tm=ROW_TILE):
    n = x2d.shape[0]
    tm = min(tm, n)
    return pl.pallas_call(
        _in_proj_body,
        grid=(n // tm,),
        in_specs=[pl.BlockSpec((tm, D_MODEL), lambda i: (i, 0)),
                  _full((1, D_MODEL)), _full((D_MODEL, IN_WIDTH))],
        out_specs=[pl.BlockSpec((tm, w), lambda i: (i, 0)) for w in IN_SIZES],
        out_shape=[jax.ShapeDtypeStruct((n, w), F32) for w in IN_SIZES],
        compiler_params=_params(("parallel",)),
        name="in_proj",
    )(x2d, g.reshape(1, D_MODEL), w_bf16)


def _shift_rows(ext, s, tc):
    n = ext.shape[0]
    return pltpu.roll(ext, (-s) % n, axis=0)[SUBLANES:SUBLANES + tc]


def _neg_expm1(y):
    series = -y * (1.0 + y * (1.0 / 2) * (1.0 + y * (1.0 / 3) * (1.0 + y * (1.0 / 4) * (1.0 + y * (1.0 / 5)))))
    return jnp.where(y > -1.0 / 64, series, 1.0 - jnp.exp(y))


def _lru_scan(a, b, rev):
    tc = a.shape[0]
    t = lax.broadcasted_iota(I32, a.shape, 0)
    d = 1
    while d < tc:
        if rev:
            keep = t < tc - d
            a_o = pltpu.roll(a, tc - d, axis=0)
            b_o = pltpu.roll(b, tc - d, axis=0)
        else:
            keep = t >= d
            a_o = pltpu.roll(a, d, axis=0)
            b_o = pltpu.roll(b, d, axis=0)
        b = jnp.where(keep, a * b_o + b, b)
        a = jnp.where(keep, a * a_o, a)
        d *= 2
    return a, b


def _lru_body(rev, *refs):
    if rev:
        (x_ref, xp_ref, xn_ref, gate_ref, hf_ref, cw_ref, cb_ref, w_ref, b_ref, c_ref,
         ng_ref, o_ref, carry_ref) = refs
    else:
        (x_ref, xp_ref, xn_ref, cw_ref, cb_ref, w_ref, b_ref, c_ref, o_ref, carry_ref) = refs
    step = pl.program_id(1)
    nc = pl.num_programs(1)
    ci = nc - 1 - step if rev else step

    @pl.when(step == 0)
    def _():
        carry_ref[...] = jnp.zeros_like(carry_ref)

    x = x_ref[0]
    tc = x.shape[0]
    prev = xp_ref[0] * (ci > 0).astype(F32)
    nxt = xn_ref[0] * (ci < nc - 1).astype(F32)
    ext = jnp.concatenate([prev, x, nxt], axis=0)
    xc = cb_ref[...] + sum(cw_ref[j:j + 1, :] * _shift_rows(ext, j - CONV_WIDTH // 2, tc)
                           for j in range(CONV_WIDTH))
    z = jnp.dot(xc.astype(BF16), w_ref[...], preferred_element_type=F32) + b_ref[...]
    r = jax.nn.sigmoid(z[:, :LRU_W])
    i = jax.nn.sigmoid(z[:, LRU_W:])
    log_a = c_ref[...] * r
    a = jnp.exp(log_a)
    b = jnp.sqrt(_neg_expm1(2.0 * log_a)) * (i * xc)
    a_cum, h_loc = _lru_scan(a, b, rev)
    h = h_loc + a_cum * carry_ref[0:1, :]
    carry_ref[0:1, :] = h[0:1, :] if rev else h[tc - 1:tc, :]
    if rev:
        y = (hf_ref[0] + h) * jax.nn.gelu(gate_ref[0])
        o_ref[0] = _rms(y, ng_ref[...])
    else:
        o_ref[0] = h


def lru_mixer(lx, lgate, cw, cb, wcat, bcat, cdec, ng, tc=SCAN_CHUNK):
    bsz, s, _ = lx.shape
    tc = min(tc, s)
    nc = s // tc
    r8 = tc // SUBLANES
    nb8 = s // SUBLANES

    def specs(rev):
        cmap = (lambda b, c: (b, nc - 1 - c, 0)) if rev else (lambda b, c: (b, c, 0))
        if rev:
            pmap = lambda b, c: (b, jnp.maximum((nc - 1 - c) * r8 - 1, 0), 0)
            nmap = lambda b, c: (b, jnp.minimum((nc - c) * r8, nb8 - 1), 0)
        else:
            pmap = lambda b, c: (b, jnp.maximum(c * r8 - 1, 0), 0)
            nmap = lambda b, c: (b, jnp.minimum((c + 1) * r8, nb8 - 1), 0)
        main = pl.BlockSpec((1, tc, LRU_W), cmap)
        halo = [pl.BlockSpec((1, SUBLANES, LRU_W), pmap), pl.BlockSpec((1, SUBLANES, LRU_W), nmap)]
        return main, halo

    common = [_full((CONV_WIDTH, LRU_W)), _full((1, LRU_W)), _full((LRU_W, 2 * LRU_W)),
              _full((1, 2 * LRU_W)), _full((1, LRU_W))]
    main, halo = specs(False)
    hf = pl.pallas_call(
        functools.partial(_lru_body, False),
        grid=(bsz, nc),
        in_specs=[main] + halo + common,
        out_specs=main,
        out_shape=jax.ShapeDtypeStruct((bsz, s, LRU_W), F32),
        scratch_shapes=[pltpu.VMEM((SUBLANES, LRU_W), F32)],
        compiler_params=_params(("parallel", "arbitrary")),
        name="lru_fwd",
    )(lx, lx, lx, cw, cb, wcat[0], bcat[0], cdec[0])
    main, halo = specs(True)
    return pl.pallas_call(
        functools.partial(_lru_body, True),
        grid=(bsz, nc),
        in_specs=[main] + halo + [main, main] + common + [_full((1, LRU_W))],
        out_specs=main,
        out_shape=jax.ShapeDtypeStruct((bsz, s, LRU_W), F32),
        scratch_shapes=[pltpu.VMEM((SUBLANES, LRU_W), F32)],
        compiler_params=_params(("parallel", "arbitrary")),
        name="lru_rev",
    )(lx, lx, lx, lgate, hf, cw, cb, wcat[1], bcat[1], cdec[1], ng)


def _ret_log_gamma():
    return np.log1p(-np.exp2(-5.0 - np.arange(RET_HEADS, dtype=np.float64)))


@functools.lru_cache(maxsize=None)
def _ret_tables(c):
    lg = np.repeat(_ret_log_gamma(), HEAD_DIM)[None, :]
    idx = np.arange(c, dtype=np.float64)[:, None]
    dec = np.stack([np.exp((idx + 1.0) * lg),
                    np.exp((c - 1.0 - idx) * lg),
                    np.exp((c - idx) * lg),
                    np.exp(idx * lg)])
    chunk = np.exp(c * lg)
    dist = np.abs(idx - idx.T)
    intra = np.exp(dist[None] * _ret_log_gamma()[:, None, None])
    lane_head = np.arange(RET_W) // HEAD_DIM
    bd = (lane_head[:, None] == lane_head[None, :]).astype(np.float32)
    return (dec.astype(np.float32), chunk.astype(np.float32), intra.astype(np.float32), bd)


def _rope_tables(s):
    half = HEAD_DIM // 2
    freqs = ROPE_BASE ** (-jnp.arange(half, dtype=F32) / half)
    ang = jnp.arange(s, dtype=F32)[:, None] * freqs[None, :]
    cos = jnp.cos(ang)
    sin = jnp.sin(ang)
    cos_t = jnp.tile(jnp.concatenate([cos, cos], axis=1), (1, RET_W // HEAD_DIM))
    sin_t = jnp.tile(jnp.concatenate([-sin, sin], axis=1), (1, RET_W // HEAD_DIM))
    return cos_t, sin_t


def _rope(x, cos, sin_signed):
    lane = lax.broadcasted_iota(I32, x.shape, 1)
    w = x.shape[1]
    half = HEAD_DIM // 2
    swapped = jnp.where(lane % HEAD_DIM < half,
                        pltpu.roll(x, w - half, axis=1), pltpu.roll(x, half, axis=1))
    return x * cos + swapped * sin_signed


def _ret_body(rev, *refs):
    if rev:
        (q_ref, k_ref, v_ref, cos_ref, sin_ref, dec_ref, chunk_ref, bd_ref,
         of_ref, g_ref, ng_ref, o_ref, state_ref) = refs
    else:
        (q_ref, k_ref, v_ref, cos_ref, sin_ref, dec_ref, chunk_ref, bd_ref,
         intra_ref, o_ref, state_ref) = refs

    @pl.when(pl.program_id(1) == 0)
    def _():
        state_ref[...] = jnp.zeros_like(state_ref)

    cos = cos_ref[...]
    sin = sin_ref[...]
    q = _rope(q_ref[0], cos, sin)
    k = _rope(k_ref[0], cos, sin) * (HEAD_DIM ** -0.5)
    vb = v_ref[0].astype(BF16)
    qd, kd = (2, 3) if rev else (0, 1)
    state = state_ref[...]
    cross = jnp.dot((q * dec_ref[qd]).astype(BF16), state.astype(BF16), preferred_element_type=F32)
    kv = lax.dot_general((k * dec_ref[kd]).astype(BF16), vb, (((0,), (0,)), ((), ())),
                         preferred_element_type=F32)
    state_ref[...] = state * chunk_ref[...] + kv * bd_ref[...]
    if rev:
        o = of_ref[0] + cross
        ms = _group_sum(o * o, bd_ref[...].astype(BF16)) * (1.0 / HEAD_DIM)
        o = o * lax.rsqrt(ms + EPS) * ng_ref[...]
        o_ref[0] = jax.nn.silu(g_ref[0]) * o
    else:
        v = v_ref[0]
        upper = lax.broadcasted_iota(I32, (1, LANES), 1) >= HEAD_DIM
        pairs = []
        for pr in range(RET_W // LANES):
            lanes = slice(pr * LANES, (pr + 1) * LANES)
            kb = k[:, lanes].astype(BF16)
            acc = jnp.zeros((q.shape[0], LANES), F32)
            for hh in range(2):
                keep = upper if hh else jnp.logical_not(upper)
                s = lax.dot_general(jnp.where(keep, q[:, lanes], 0.0).astype(BF16), kb, (((1,), (1,)), ((), ())),
                                    preferred_element_type=F32)
                s = (s * intra_ref[2 * pr + hh]).astype(BF16)
                acc = acc + jnp.dot(s, jnp.where(keep, v[:, lanes], 0.0).astype(BF16), preferred_element_type=F32)
            pairs.append(acc)
        o_ref[0] = cross + jnp.concatenate(pairs, axis=1)


def ret_mixer(rq, rk, rv, rg, ng, c=RET_CHUNK):
    bsz, s, _ = rq.shape
    c = min(c, s)
    nc = s // c
    dec, chunk, intra, bd = _ret_tables(c)
    cos_t, sin_t = _rope_tables(s)

    def specs(rev):
        cmap = (lambda b, i: (b, nc - 1 - i, 0)) if rev else (lambda b, i: (b, i, 0))
        tmap = (lambda b, i: (nc - 1 - i, 0)) if rev else (lambda b, i: (i, 0))
        main = pl.BlockSpec((1, c, RET_W), cmap)
        tab = pl.BlockSpec((c, RET_W), tmap)
        return main, tab

    consts = [_full((4, c, RET_W)), _full((1, RET_W)), _full((RET_W, RET_W))]
    main, tab = specs(False)
    of = pl.pallas_call(
        functools.partial(_ret_body, False),
        grid=(bsz, nc),
        in_specs=[main, main, main, tab, tab] + consts + [_full((RET_HEADS, c, c))],
        out_specs=main,
        out_shape=jax.ShapeDtypeStruct((bsz, s, RET_W), F32),
        scratch_shapes=[pltpu.VMEM((RET_W, RET_W), F32)],
        compiler_params=_params(("parallel", "arbitrary")),
        name="ret_fwd",
    )(rq, rk, rv, cos_t, sin_t, dec, chunk, bd, intra)
    main, tab = specs(True)
    return pl.pallas_call(
        functools.partial(_ret_body, True),
        grid=(bsz, nc),
        in_specs=[main, main, main, tab, tab] + consts + [main, main, _full((1, RET_W))],
        out_specs=main,
        out_shape=jax.ShapeDtypeStruct((bsz, s, RET_W), F32),
        scratch_shapes=[pltpu.VMEM((RET_W, RET_W), F32)],
        compiler_params=_params(("parallel", "arbitrary")),
        name="ret_rev",
    )(rq, rk, rv, cos_t, sin_t, dec, chunk, bd, of, rg, ng.reshape(1, RET_W))


HEAD_SLOT = 128
LOG2E = 1.4426950408889634


@functools.lru_cache(maxsize=None)
def _attn_consts():
    lane = np.arange(DIFF_W)
    grp = lane // DIFF_HALF
    bd32 = (grp[:, None] == grp[None, :]).astype(np.float32)
    place = np.zeros((DIFF_W, DIFF_HEADS * HEAD_SLOT), np.float32)
    place[lane, (lane // HEAD_DIM) * HEAD_SLOT + lane % HEAD_DIM] = 1.0
    ones_col = np.zeros((1, DIFF_HEADS * HEAD_SLOT), np.float32)
    ones_col[0, np.arange(DIFF_HEADS) * HEAD_SLOT + HEAD_DIM] = 1.0
    return bd32, place, ones_col


def _attn_prep_body(q_ref, k_ref, v_ref, qg_ref, kg_ref, bd_ref, place_ref, ones_ref,
                    qn_ref, kn_ref, vv_ref):
    def qk_norm(x, g):
        ms = _group_sum(x * x, bd_ref[...]) * (1.0 / DIFF_HALF)
        return x * lax.rsqrt(ms + EPS) * g

    qn_ref[0] = (qk_norm(q_ref[0], qg_ref[...]) * (DIFF_HALF ** -0.5 * LOG2E)).astype(BF16)
    kn_ref[0] = qk_norm(k_ref[0], kg_ref[...]).astype(BF16)
    vb = v_ref[0].astype(BF16)
    vv_ref[0] = (jnp.dot(vb, place_ref[...], preferred_element_type=F32) + ones_ref[...]).astype(BF16)


def attn_prep(dq, dk, dv, qg, kg, tc=ROW_TILE):
    bsz, s, _ = dq.shape
    tc = min(tc, s)
    bd32, place, ones_col = _attn_consts()
    wide = DIFF_HEADS * HEAD_SLOT
    main = pl.BlockSpec((1, tc, DIFF_W), lambda b, c: (b, c, 0))
    outb = pl.BlockSpec((1, tc, wide), lambda b, c: (b, c, 0))
    rep = DIFF_W // DIFF_HALF
    return pl.pallas_call(
        _attn_prep_body,
        grid=(bsz, s // tc),
        in_specs=[main, main, main, _full((1, DIFF_W)), _full((1, DIFF_W)),
                  _full((DIFF_W, DIFF_W)), _full((DIFF_W, wide)), _full((1, wide))],
        out_specs=[main, main, outb],
        out_shape=[jax.ShapeDtypeStruct((bsz, s, DIFF_W), BF16)] * 2 + [jax.ShapeDtypeStruct((bsz, s, wide), BF16)],
        compiler_params=_params(("parallel", "parallel")),
        name="attn_prep",
    )(dq, dk, dv, jnp.tile(qg, rep).reshape(1, DIFF_W), jnp.tile(kg, rep).reshape(1, DIFF_W),
      jnp.asarray(bd32, BF16), jnp.asarray(place, BF16), ones_col)


def _t5_bucket_np(rel):
    nb = NUM_BUCKETS // 2
    max_exact = nb // 2
    n = np.abs(rel)
    nf = np.maximum(n, 1).astype(np.float64)
    large = max_exact + np.floor(2.0 * np.log2(nf / max_exact)).astype(np.int64)
    large = np.minimum(large, nb - 1)
    return (np.where(rel > 0, nb, 0) + np.where(n < max_exact, n, large)).astype(np.int32)


def _bias_tiles(rel_bias, t, tk):
    period = t + tk
    x = np.arange(period)[None, :]
    d = np.arange(-(tk // t), 2)[:, None]
    diag = rel_bias[_t5_bucket_np(x - (t - 1) + d * t)]
    diag = jnp.transpose(diag, (2, 0, 1))
    hankel = jnp.tile(diag, (1, 1, t + 1))[:, :, :t * (period + 1)].reshape(DIFF_HEADS, d.shape[0], t, period + 1)
    near = hankel[:, :, ::-1, :tk]
    nb = NUM_BUCKETS // 2
    left = jnp.broadcast_to(rel_bias[nb - 1][:, None, None, None], (DIFF_HEADS, 1, t, tk))
    right = jnp.broadcast_to(rel_bias[NUM_BUCKETS - 1][:, None, None, None], (DIFF_HEADS, 1, t, tk))
    return jnp.concatenate([left, near, right], axis=1) * LOG2E


def _attn_body(online, q_ref, k_ref, v_ref, bias_ref, lam_ref, linit_ref, g_ref, o_ref):
    t = q_ref.shape[1]
    tk = bias_ref.shape[3]
    kq = tk // t
    nk = k_ref.shape[1] // tk
    qi = pl.program_id(2)
    lam = (jnp.exp(jnp.sum(lam_ref[0:1, :] * lam_ref[1:2, :], axis=1, keepdims=True))
           - jnp.exp(jnp.sum(lam_ref[2:3, :] * lam_ref[3:4, :], axis=1, keepdims=True))
           + linit_ref[...])
    lane = lax.broadcasted_iota(I32, (t, HEAD_SLOT), 1)
    qf = q_ref[0].astype(F32)
    half = lane // DIFF_HALF
    qs = [jnp.concatenate([jnp.where(half == 2 * hh + m, qf, 0.0) for m in range(2)], axis=0).astype(BF16)
          for hh in range(2)]

    def body(j, carry):
        rows = pl.ds(pl.multiple_of(j * tk, tk), tk)
        bidx = jnp.clip(j * kq - qi, -kq - 1, 2) + kq + 1
        kb = k_ref[0, rows, :]
        new = []
        for hh in range(2):
            lo = hh * HEAD_SLOT
            m_i, acc = carry[hh]
            vb = v_ref[0, rows, lo:lo + HEAD_SLOT]
            s = lax.dot_general(qs[hh], kb, (((1,), (1,)), ((), ())), preferred_element_type=F32)
            bt = bias_ref[hh, bidx]
            s = s + jnp.concatenate([bt, bt], axis=0)
            if online:
                m_new = jnp.maximum(m_i, jnp.max(s, axis=1, keepdims=True))
                p = jnp.exp2(s - m_new)
                acc = jnp.exp2(m_i - m_new) * acc
            else:
                m_new = m_i
                p = jnp.exp2(s)
            acc = acc + jnp.dot(p.astype(BF16), vb, preferred_element_type=F32)
            new.append((m_new, acc))
        return tuple(new)

    init = (jnp.full((2 * t, 1) if online else (1, 1), -1e30, F32), jnp.zeros((2 * t, HEAD_SLOT), F32))
    res = lax.fori_loop(0, nk, body, (init, init), unroll=1 if online else 2)

    outs = []
    for hh in range(2):
        acc = res[hh][1]
        sm = acc / acc[:, HEAD_DIM:HEAD_DIM + 1]
        o = sm[:t] - lam * sm[t:]
        o = jnp.where(lane < HEAD_DIM, o, 0.0)
        ms = jnp.sum(o * o, axis=1, keepdims=True) * (1.0 / HEAD_DIM)
        outs.append(o * lax.rsqrt(ms + EPS))
    both = jnp.where(lane < HEAD_DIM, outs[0], pltpu.roll(outs[1], HEAD_DIM, axis=1))
    o_ref[0] = both * g_ref[...] * (1.0 - linit_ref[...])


MAX_SCORE_RANGE = 96.0


def _score_bound(qg, kg, rel_bias):
    qk = DIFF_HALF * (DIFF_HALF ** -0.5 * LOG2E) * jnp.max(jnp.abs(qg)) * jnp.max(jnp.abs(kg)) * 1.02
    hi = jnp.max(rel_bias) * LOG2E
    lo = jnp.min(rel_bias) * LOG2E
    return qk + hi, 2.0 * qk + (hi - lo)


def diff_attn(qn, kn, vv, bias, bound, spread, lam_vecs, lam_init, ng):
    bsz, s, _ = qn.shape
    _, n_tiles, t, tk = bias.shape
    assert t >= MAX_DISTANCE and tk % t == 0
    qspec = pl.BlockSpec((1, t, 2 * HEAD_DIM), lambda b, h, i: (b, i, h))
    kspec = pl.BlockSpec((1, s, 2 * HEAD_DIM), lambda b, h, i: (b, 0, h))
    vspec = pl.BlockSpec((1, s, 2 * HEAD_SLOT), lambda b, h, i: (b, 0, h))
    lam_pad = jnp.zeros((4, LANES), F32).at[:, :DIFF_HALF].set(lam_vecs)
    linit = jnp.full((1, LANES), lam_init, F32)
    g2 = jnp.tile(ng, 2).reshape(1, LANES)

    def call(online, bias_tiles):
        return pl.pallas_call(
            functools.partial(_attn_body, online),
            grid=(bsz, DIFF_HEADS // 2, s // t),
            in_specs=[qspec, kspec, vspec,
                      pl.BlockSpec((2, n_tiles, t, tk), lambda b, h, i: (h, 0, 0, 0)),
                      _full((4, LANES)), _full((1, LANES)), _full((1, LANES))],
            out_specs=pl.BlockSpec((1, t, 2 * HEAD_DIM), lambda b, h, i: (b, i, h)),
            out_shape=jax.ShapeDtypeStruct((bsz, s, DIFF_W), F32),
            compiler_params=_params(("parallel", "parallel", "arbitrary")),
            name="attn_online" if online else "attn",
        )(qn, kn, vv, bias_tiles, lam_pad, linit, g2)

    return lax.cond(spread <= MAX_SCORE_RANGE,
                    lambda: call(False, bias - bound), lambda: call(True, bias))


def _out_proj_body(x_ref, yl_ref, yr_ref, yd_ref, wl_ref, wr_ref, wd_ref, g_ref, whi_ref, wlo_ref,
                   h_ref, hn_ref, pt_ref):
    h = (x_ref[...]
         + jnp.dot(yl_ref[...].astype(BF16), wl_ref[...], preferred_element_type=F32)
         + jnp.dot(yr_ref[...].astype(BF16), wr_ref[...], preferred_element_type=F32)
         + jnp.dot(yd_ref[...].astype(BF16), wd_ref[...], preferred_element_type=F32))
    h_ref[...] = h
    hn = _rms(h, g_ref[...])
    hn_ref[...] = hn
    hn_hi, hn_lo = _split_bf16(hn)
    nt = (((1,), (1,)), ((), ()))
    logits_t = (lax.dot_general(whi_ref[...], hn_hi, nt, preferred_element_type=F32)
                + lax.dot_general(wlo_ref[...], hn_hi, nt, preferred_element_type=F32)
                + lax.dot_general(whi_ref[...], hn_lo, nt, preferred_element_type=F32))
    et = jnp.exp(logits_t - jnp.max(logits_t, axis=0, keepdims=True))
    pt_ref[...] = et / jnp.sum(et, axis=0, keepdims=True)


def out_proj(x2d, yl, yr, yd, w_out_bf16, g, w_router, tm=ROW_TILE):
    n = x2d.shape[0]
    tm = min(tm, n)
    row = lambda w: pl.BlockSpec((tm, w), lambda i: (i, 0))
    wr_hi, wr_lo = _split_bf16(w_router.T)
    return pl.pallas_call(
        _out_proj_body,
        grid=(n // tm,),
        in_specs=[row(D_MODEL), row(LRU_W), row(RET_W), row(DIFF_W),
                  _full((LRU_W, D_MODEL)), _full((RET_W, D_MODEL)), _full((DIFF_W, D_MODEL)),
                  _full((1, D_MODEL)), _full((N_EXPERTS, D_MODEL)), _full((N_EXPERTS, D_MODEL))],
        out_specs=[row(D_MODEL), row(D_MODEL), pl.BlockSpec((N_EXPERTS, tm), lambda i: (0, i))],
        out_shape=[jax.ShapeDtypeStruct((n, D_MODEL), F32), jax.ShapeDtypeStruct((n, D_MODEL), F32),
                   jax.ShapeDtypeStruct((N_EXPERTS, n), F32)],
        compiler_params=_params(("parallel",)),
        name="out_proj",
    )(x2d, yl, yr, yd, w_out_bf16[:LRU_W], w_out_bf16[LRU_W:LRU_W + RET_W], w_out_bf16[LRU_W + RET_W:],
      g.reshape(1, D_MODEL), wr_hi, wr_lo)


def _row_cumsum(x01, tri_ref, nr):
    within = jnp.dot(x01.astype(F32).astype(BF16), tri_ref[...], preferred_element_type=F32).astype(I32)
    tot = jnp.broadcast_to(within[:, SEL_ROW - 1:SEL_ROW], (nr, LANES))
    r = lax.broadcasted_iota(I32, (nr, LANES), 0)
    inc = tot
    d = 1
    while d < nr:
        inc = inc + jnp.where(r >= d, pltpu.roll(inc, d, axis=0), 0)
        d *= 2
    return within + (inc - tot)[:, 0:1]


def _select_body(cap, p_ref, tri_ref, cnt_ref, sel_ref):
    p = p_ref[0]
    nr = p.shape[0]
    bits = pltpu.bitcast(p, I32)

    def body(i, prefix):
        cand = prefix | (jnp.int32(1) << (30 - i))
        cnt = jnp.sum((bits >= cand).astype(I32), keepdims=True)
        return jnp.where(cnt >= cap, cand, prefix)

    thr = lax.fori_loop(0, 31, body, jnp.zeros((1, 1), I32))
    gt = bits > thr
    eq = bits == thr
    need = cap - jnp.sum(gt.astype(I32), keepdims=True)
    eq01 = eq.astype(I32)
    rank_eq = _row_cumsum(eq01, tri_ref, nr) - eq01
    sel = jnp.where(gt, 1, jnp.where(eq & (rank_eq < need), 1, 0))
    sel_ref[0] = sel
    cnt_ref[0] = _row_cumsum(sel, tri_ref, nr)


def select(probs_t, cap):
    e, n = probs_t.shape
    nr = n // SEL_ROW
    tri = np.triu(np.ones((SEL_ROW, SEL_ROW), np.float32))
    blk = pl.BlockSpec((1, nr, SEL_ROW), lambda i: (i, 0, 0))
    return pl.pallas_call(
        functools.partial(_select_body, cap),
        grid=(e,),
        in_specs=[blk, _full((SEL_ROW, SEL_ROW))],
        out_specs=[blk, blk],
        out_shape=[jax.ShapeDtypeStruct((e, nr, SEL_ROW), I32)] * 2,
        compiler_params=_params(("parallel",)),
        name="select",
    )(probs_t.reshape(e, nr, SEL_ROW), jnp.asarray(tri, BF16))


def _slot_index_body(lo_ref, hi_ref, pos_ref, o_ref):
    e = pl.program_id(0)
    n_sb, parts, w = o_ref.shape[1:]
    tok = lax.broadcasted_iota(I32, (parts, SEL_ROW), 1)
    part = lax.broadcasted_iota(I32, (parts, SEL_ROW), 0)
    base = jnp.where(part == 0, tok // 256, jnp.where(part == 1, tok % 256, 0))
    slot0 = lax.broadcasted_iota(I32, (w, 1), 0)

    nr = pos_ref.shape[1]
    group = min(SLOT_INDEX_ROWS, nr)

    def block(sb, carry):
        lo = lo_ref[e, sb]
        slots = slot0 + sb * w

        def visit(v, acc):
            first = lo + v * group
            start = jnp.minimum(first, nr - group)
            onehots, payloads = [], []
            for k in range(group):
                r = start + k
                onehots.append(jnp.where(pos_ref[0, pl.ds(r, 1), :] == slots, 1.0, 0.0).astype(BF16))
                row_part = jnp.where(part == 2, r, base)
                payloads.append(jnp.where(r >= first, row_part, 0).astype(F32).astype(BF16))
            return acc + lax.dot_general(jnp.concatenate(payloads, axis=1), jnp.concatenate(onehots, axis=1),
                                         (((1,), (1,)), ((), ())), preferred_element_type=F32)

        visits = (hi_ref[e, sb] - lo) // group + 1
        o_ref[0, sb] = lax.fori_loop(0, visits, visit, jnp.zeros((parts, w), F32))
        return carry

    lax.fori_loop(0, n_sb, block, 0)


def slot_index(posm, row_end, cap, w):
    e, nr, _ = posm.shape
    assert nr <= 256 and SEL_ROW == 512
    n_sb = cap // w
    edges = jnp.arange(n_sb + 1, dtype=I32) * w
    lo = jnp.sum(row_end[:, None, :] <= edges[None, :-1, None], axis=2)
    hi = jnp.minimum(jnp.sum(row_end[:, None, :] < edges[None, 1:, None], axis=2), nr - 1)
    parts = pl.pallas_call(
        _slot_index_body,
        grid_spec=pltpu.PrefetchScalarGridSpec(
            num_scalar_prefetch=2,
            grid=(e,),
            in_specs=[pl.BlockSpec((1, nr, SEL_ROW), lambda i, lo, hi: (i, 0, 0))],
            out_specs=pl.BlockSpec((1, n_sb, SUBLANES, w), lambda i, lo, hi: (i, 0, 0, 0)),
        ),
        out_shape=jax.ShapeDtypeStruct((e, n_sb, SUBLANES, w), F32),
        compiler_params=_params(("arbitrary",)),
        name="slot_index",
    )(lo.astype(I32), hi.astype(I32), posm)
    idx = parts[:, :, 2] * SEL_ROW + parts[:, :, 0] * 256 + parts[:, :, 1]
    return idx.astype(I32).reshape(e * n_sb, 1, w)


def _ffn_body(idx_ref, idx_next_ref, x_hbm, wg_ref, wu_ref, wd_ref, o_ref, xbuf, sem):
    n_sb = pl.num_programs(1)
    step = pl.program_id(0) * n_sb + pl.program_id(1)
    n_steps = pl.num_programs(0) * n_sb
    half = step % 2
    w = xbuf.shape[1]

    def row_copy(idx_block, r, dst_half):
        return pltpu.make_async_copy(x_hbm.at[pl.ds(idx_block[0, 0, r], 1)], xbuf.at[dst_half, pl.ds(r, 1)],
                                     sem.at[dst_half])

    def wait_block(dst_half):
        for r in range(w):
            pltpu.make_async_copy(x_hbm.at[pl.ds(0, 1)], xbuf.at[dst_half, pl.ds(r, 1)], sem.at[dst_half]).wait()

    @pl.when(step == 0)
    def _():
        def body(r, carry):
            row_copy(idx_ref, r, 0).start()
            return carry

        lax.fori_loop(0, w, body, 0, unroll=8)

    wait_block(half)
    for r in range(w):
        row_copy(idx_next_ref, r, 1 - half).start()

    xe = xbuf[half].astype(BF16)
    acc = jnp.zeros((w, D_MODEL), F32)
    for c0 in range(0, D_FF, FF_CHUNK):
        g = jnp.dot(xe, wg_ref[0, :, c0:c0 + FF_CHUNK], preferred_element_type=F32)
        u = jnp.dot(xe, wu_ref[0, :, c0:c0 + FF_CHUNK], preferred_element_type=F32)
        mid = (jax.nn.silu(g) * u).astype(BF16)
        acc = acc + jnp.dot(mid, wd_ref[0, c0:c0 + FF_CHUNK, :], preferred_element_type=F32)
    o_ref[0] = acc.astype(BF16)

    @pl.when(step + 1 == n_steps)
    def _():
        wait_block(1 - half)


def expert_ffn(hn, idx, wg, wu, wd, cap, w):
    e = wg.shape[0]
    n_sb = cap // w
    last = e * n_sb - 1
    smem_block = lambda index_map: pl.BlockSpec((1, 1, w), index_map, memory_space=pltpu.SMEM)
    return pl.pallas_call(
        _ffn_body,
        grid=(e, n_sb),
        in_specs=[
            smem_block(lambda i, s: (i * n_sb + s, 0, 0)),
            smem_block(lambda i, s: (jnp.minimum(i * n_sb + s + 1, last), 0, 0)),
            pl.BlockSpec(memory_space=pl.ANY),
            pl.BlockSpec((1, D_MODEL, D_FF), lambda i, s: (i, 0, 0)),
            pl.BlockSpec((1, D_MODEL, D_FF), lambda i, s: (i, 0, 0)),
            pl.BlockSpec((1, D_FF, D_MODEL), lambda i, s: (i, 0, 0)),
        ],
        out_specs=pl.BlockSpec((1, w, D_MODEL), lambda i, s: (i, s, 0)),
        out_shape=jax.ShapeDtypeStruct((e, cap, D_MODEL), BF16),
        scratch_shapes=[pltpu.VMEM((2, w, D_MODEL), F32), pltpu.SemaphoreType.DMA((2,))],
        compiler_params=_params(("arbitrary", "arbitrary"), vmem_mb=58),
        name="ffn",
    )(idx, idx, hn, wg, wu, wd)


def _combine_body(ns_ref, ws_ref, wide_ref, h_ref, pos_ref, gate_ref, *refs):
    narrow_refs = refs[:N_EXPERTS]
    wide_refs = refs[N_EXPERTS:2 * N_EXPERTS]
    o_ref = refs[2 * N_EXPERTS]
    i = pl.program_id(0)
    pos = pos_ref[...]
    gate = gate_ref[...]
    wide = wide_ref[i]

    @pl.when(wide == 0)
    def _():
        lane = lax.broadcasted_iota(I32, (1, COMB_NARROW), 1)
        blocks = []
        for e in range(N_EXPERTS):
            slots = ns_ref[e, i] * COMB_ALIGN + lane
            blocks.append(jnp.where(pos[:, e:e + 1] == slots, gate[:, e:e + 1], 0.0).astype(BF16))
        g_all = jnp.concatenate(blocks, axis=1)
        rows = jnp.concatenate([r[...] for r in narrow_refs], axis=0)
        o_ref[...] = h_ref[...] + jnp.dot(g_all, rows, preferred_element_type=F32)

    @pl.when(wide != 0)
    def _():
        acc = h_ref[...]
        lane = lax.broadcasted_iota(I32, (1, COMB_WIN), 1)
        for e in range(N_EXPERTS):
            slots = ws_ref[e, i] * COMB_ALIGN + lane
            g = jnp.where(pos[:, e:e + 1] == slots, gate[:, e:e + 1], 0.0).astype(BF16)
            acc = acc + jnp.dot(g, wide_refs[e][...], preferred_element_type=F32)
        o_ref[...] = acc


def combine(h2d, pos_t, gates, ye, narrow_start, wide_start, wide_flag):
    n = h2d.shape[0]
    t = COMB_TILE
    row = lambda w: pl.BlockSpec((t, w), lambda i, ns, ws, fl: (i, 0))

    def narrow_spec(e):
        return pl.BlockSpec((None, pl.Element(COMB_NARROW), pl.Element(D_MODEL)),
                            lambda i, ns, ws, fl: (e, ns[e, i] * COMB_ALIGN, 0))

    def wide_spec(e):
        return pl.BlockSpec((None, pl.Element(COMB_WIN), pl.Element(D_MODEL)),
                            lambda i, ns, ws, fl: (e, ws[e, i] * COMB_ALIGN, 0))

    grid_spec = pltpu.PrefetchScalarGridSpec(
        num_scalar_prefetch=3,
        grid=(n // t,),
        in_specs=([row(D_MODEL), row(N_EXPERTS), row(N_EXPERTS)]
                  + [narrow_spec(e) for e in range(N_EXPERTS)] + [wide_spec(e) for e in range(N_EXPERTS)]),
        out_specs=row(D_MODEL),
    )
    return pl.pallas_call(
        _combine_body,
        grid_spec=grid_spec,
        out_shape=jax.ShapeDtypeStruct((n, D_MODEL), F32),
        compiler_params=_params(("arbitrary",)),
        name="combine",
    )(narrow_start, wide_start, wide_flag, h2d, pos_t, gates, *([ye] * (2 * N_EXPERTS)))


def ec_moe(h2d, hn, probs_t, wg, wu, wd):
    n = h2d.shape[0]
    probs = probs_t.T
    cap = EC_FACTOR * n // N_EXPERTS
    cnt, sel = select(probs_t, cap)
    posm = jnp.where(sel > 0, cnt - 1, -1)
    row_end = cnt[:, :, SEL_ROW - 1]
    w = min(SLOT_BLOCK, cap)
    ye = expert_ffn(hn, slot_index(posm, row_end, cap, w), wg, wu, wd, cap, w)
    pos_t = posm.reshape(N_EXPERTS, n).T
    base = (cnt - sel).reshape(N_EXPERTS, n)[:, ::COMB_TILE]
    stop = jnp.concatenate([base[:, 1:], jnp.full((N_EXPERTS, 1), cap, I32)], axis=1)
    narrow_start = jnp.minimum(base // COMB_ALIGN, (cap - COMB_NARROW) // COMB_ALIGN).astype(I32)
    wide_flag = jnp.any(stop > narrow_start * COMB_ALIGN + COMB_NARROW, axis=0)
    wide_start = jnp.where(wide_flag[None, :], jnp.minimum(base // COMB_ALIGN, (cap - COMB_WIN) // COMB_ALIGN), 0)
    return combine(h2d, pos_t, probs, ye, narrow_start, wide_start.astype(I32), wide_flag.astype(I32))


def _block_diag(w):
    h, d, _ = w.shape
    eye = jnp.eye(h, dtype=w.dtype)
    return jnp.einsum("hde,hg->hdge", w, eye).reshape(h * d, h * d)


def _prep_layer(l, p):
    wcat = jnp.stack([jnp.concatenate([_block_diag(p["lru_wa"][l, d]), _block_diag(p["lru_wx"][l, d])], axis=1)
                      for d in range(2)]).astype(BF16)
    bcat = jnp.stack([jnp.concatenate([p["lru_ba"][l, d], p["lru_bx"][l, d]])[None, :] for d in range(2)])
    cdec = (-LRU_C * jax.nn.softplus(-p["lru_lambda"][l]))[:, None, :]
    return dict(
        w_in=p["w_in"][l].astype(BF16), w_out=p["w_out"][l].astype(BF16),
        wcat=wcat, bcat=bcat, cdec=cdec,
        wg=p["w_gate"][l].astype(BF16), wu=p["w_up"][l].astype(BF16), wd=p["w_down"][l].astype(BF16))


def _trunk(x, p, prepped):
    bsz, s, _ = x.shape
    n = bsz * s
    x2d = x.reshape(n, D_MODEL)
    bias = _bias_tiles(p["rel_bias"], min(ATTN_TILE, s), min(ATTN_KEY_TILE, s))
    for l, w in enumerate(prepped):
        lam_init = 0.8 - 0.6 * math.exp(-0.3 * l)
        pieces = in_proj(x2d, p["ln1_g"][l], w["w_in"])
        lx, lgate, rq, rk, rv, rg, dq, dk, dv = [a.reshape(bsz, s, a.shape[1]) for a in pieces]
        y_lru = lru_mixer(lx, lgate, p["conv_w"][l], p["conv_b"][l].reshape(1, LRU_W), w["wcat"], w["bcat"],
                          w["cdec"], p["lru_norm_g"][l].reshape(1, LRU_W))
        y_ret = ret_mixer(rq, rk, rv, rg, p["ret_norm_g"][l])
        qn, kn, vv = attn_prep(dq, dk, dv, p["q_norm_g"][l], p["k_norm_g"][l])
        bound, spread = _score_bound(p["q_norm_g"][l], p["k_norm_g"][l], p["rel_bias"])
        y_diff = diff_attn(qn, kn, vv, bias, bound, spread, p["diff_lambda"][l], lam_init, p["diff_norm_g"][l])
        h2d, hn, probs_t = out_proj(x2d, y_lru.reshape(n, LRU_W), y_ret.reshape(n, RET_W),
                                    y_diff.reshape(n, DIFF_W), w["w_out"], p["ln2_g"][l], p["w_router"][l])
        x2d = ec_moe(h2d, hn, probs_t, w["wg"], w["wu"], w["wd"])
    return x2d.reshape(bsz, s, D_MODEL)


def kernel(x_prompt, x_sample, rel_bias, ln1_g, ln2_g, w_in, conv_w, conv_b, lru_wa, lru_ba, lru_wx, lru_bx,
           lru_lambda, lru_norm_g, ret_norm_g, q_norm_g, k_norm_g, diff_lambda, diff_norm_g, w_out, w_router,
           w_gate, w_up, w_down):
    p = dict(rel_bias=rel_bias, ln1_g=ln1_g, ln2_g=ln2_g, w_in=w_in, conv_w=conv_w, conv_b=conv_b,
             lru_wa=lru_wa, lru_ba=lru_ba, lru_wx=lru_wx, lru_bx=lru_bx, lru_lambda=lru_lambda,
             lru_norm_g=lru_norm_g, ret_norm_g=ret_norm_g, q_norm_g=q_norm_g, k_norm_g=k_norm_g,
             diff_lambda=diff_lambda, diff_norm_g=diff_norm_g, w_out=w_out, w_router=w_router,
             w_gate=w_gate, w_up=w_up, w_down=w_down)
    prepped = [_prep_layer(l, p) for l in range(w_in.shape[0])]
    return _trunk(x_prompt, p, prepped), _trunk(x_sample, p, prepped)
```

```python
import functools
import math

import numpy as np
import jax
import jax.numpy as jnp
from jax import lax
from jax.experimental import pallas as pl
from jax.experimental.pallas import tpu as pltpu

F32 = jnp.float32
BF16 = jnp.bfloat16
I32 = jnp.int32
HIGHEST = lax.Precision.HIGHEST

D_MODEL = 1024
HEAD_DIM = 64
LRU_W = 256
LRU_HEADS = 4
RET_W = 384
RET_HEADS = 6
DIFF_W = 384
DIFF_HEADS = 6
DIFF_HALF = 32
IN_SIZES = (LRU_W, LRU_W, RET_W, RET_W, RET_W, RET_W, DIFF_W, DIFF_W, DIFF_W)
IN_WIDTH = sum(IN_SIZES)
CONV_WIDTH = 4
LRU_C = 8.0
ROPE_BASE = 10000.0
NUM_BUCKETS = 32
MAX_DISTANCE = 128
N_EXPERTS = 16
EC_FACTOR = 2
D_FF = 2816
EPS = 1e-6

V7X_VMEM_BYTES = 64 * 1024 * 1024
SUBLANES = 8
LANES = 128

ROW_TILE = 512
SCAN_CHUNK = 256
RET_CHUNK = 256
ATTN_TILE = 512
ATTN_KEY_TILE = 512
SEL_ROW = 512
SLOT_BLOCK = 256
SLOT_INDEX_ROWS = 6
FF_CHUNK = 1408
COMB_TILE = 256
COMB_ALIGN = 16
COMB_WIN = COMB_TILE + COMB_ALIGN
COMB_NARROW = LANES


def _params(sem, vmem_mb=48):
    return pltpu.CompilerParams(dimension_semantics=sem,
                                vmem_limit_bytes=vmem_mb * 1024 * 1024)


def _full(shape):
    nd = len(shape)
    return pl.BlockSpec(shape, lambda *_: (0,) * nd)


def _rms(x, g):
    return x * lax.rsqrt(jnp.mean(x * x, axis=-1, keepdims=True) + EPS) * g


def _split_bf16(x):
    hi = x.astype(BF16)
    return hi, (x - hi.astype(F32)).astype(BF16)


def _group_sum(x, ones_bf16):
    hi, lo = _split_bf16(x)
    return (jnp.dot(hi, ones_bf16, preferred_element_type=F32)
            + jnp.dot(lo, ones_bf16, preferred_element_type=F32))


def _in_proj_body(x_ref, g_ref, w_ref, *o_refs):
    xn = _rms(x_ref[...], g_ref[...]).astype(BF16)
    off = 0
    for o_ref, width in zip(o_refs, IN_SIZES):
        o_ref[...] = jnp.dot(xn, w_ref[:, off:off + width], preferred_element_type=F32)
        off += width


def in_proj(x2d, g, w_bf16, tm=ROW_TILE):
    n = x2d.shape[0]
    tm = min(tm, n)
    return pl.pallas_call(
        _in_proj_body,
        grid=(n // tm,),
        in_specs=[pl.BlockSpec((tm, D_MODEL), lambda i: (i, 0)),
                  _full((1, D_MODEL)), _full((D_MODEL, IN_WIDTH))],
        out_specs=[pl.BlockSpec((tm, w), lambda i: (i, 0)) for w in IN_SIZES],
        out_shape=[jax.ShapeDtypeStruct((n, w), F32) for w in IN_SIZES],
        compiler_params=_params(("parallel",)),
        name="in_proj",
    )(x2d, g.reshape(1, D_MODEL), w_bf16)


def _shift_rows(ext, s, tc):
    n = ext.shape[0]
    return pltpu.roll(ext, (-s) % n, axis=0)[SUBLANES:SUBLANES + tc]


def _neg_expm1(y):
    series = -y * (1.0 + y * (1.0 / 2) * (1.0 + y * (1.0 / 3) * (1.0 + y * (1.0 / 4) * (1.0 + y * (1.0 / 5)))))
    return jnp.where(y > -1.0 / 64, series, 1.0 - jnp.exp(y))


def _lru_scan(a, b, rev):
    tc = a.shape[0]
    t = lax.broadcasted_iota(I32, a.shape, 0)
    d = 1
    while d < tc:
        if d % SUBLANES:
            if rev:
                keep = t < tc - d
                a_o = pltpu.roll(a, tc - d, axis=0)
                b_o = pltpu.roll(b, tc - d, axis=0)
            else:
                keep = t >= d
                a_o = pltpu.roll(a, d, axis=0)
                b_o = pltpu.roll(b, d, axis=0)
            b = jnp.where(keep, a * b_o + b, b)
            a = jnp.where(keep, a * a_o, a)
        elif rev:
            b = jnp.concatenate([a[:tc - d] * b[d:] + b[:tc - d], b[tc - d:]], axis=0)
            a = jnp.concatenate([a[:tc - d] * a[d:], a[tc - d:]], axis=0)
        else:
            b = jnp.concatenate([b[:d], a[d:] * b[:tc - d] + b[d:]], axis=0)
            a = jnp.concatenate([a[:d], a[d:] * a[:tc - d]], axis=0)
        d *= 2
    return a, b


def _lru_body(rev, *refs):
    if rev:
        (x_ref, xp_ref, xn_ref, gate_ref, hf_ref, cw_ref, cb_ref, w_ref, b_ref, c_ref,
         ng_ref, o_ref, carry_ref) = refs
    else:
        (x_ref, xp_ref, xn_ref, cw_ref, cb_ref, w_ref, b_ref, c_ref, o_ref, carry_ref) = refs
    step = pl.program_id(1)
    nc = pl.num_programs(1)
    ci = nc - 1 - step if rev else step

    @pl.when(step == 0)
    def _():
        carry_ref[...] = jnp.zeros_like(carry_ref)

    x = x_ref[0]
    tc = x.shape[0]
    prev = xp_ref[0] * (ci > 0).astype(F32)
    nxt = xn_ref[0] * (ci < nc - 1).astype(F32)
    ext = jnp.concatenate([prev, x, nxt], axis=0)
    xc = cb_ref[...] + sum(cw_ref[j:j + 1, :] * _shift_rows(ext, j - CONV_WIDTH // 2, tc)
                           for j in range(CONV_WIDTH))
    z = jnp.dot(xc.astype(BF16), w_ref[...], preferred_element_type=F32) + b_ref[...]
    r = jax.nn.sigmoid(z[:, :LRU_W])
    i = jax.nn.sigmoid(z[:, LRU_W:])
    log_a = c_ref[...] * r
    a = jnp.exp(log_a)
    b = jnp.sqrt(_neg_expm1(2.0 * log_a)) * (i * xc)
    a_cum, h_loc = _lru_scan(a, b, rev)
    h = h_loc + a_cum * carry_ref[0:1, :]
    carry_ref[0:1, :] = h[0:1, :] if rev else h[tc - 1:tc, :]
    if rev:
        y = (hf_ref[0] + h) * jax.nn.gelu(gate_ref[0])
        o_ref[0] = _rms(y, ng_ref[...])
    else:
        o_ref[0] = h


def lru_mixer(lx, lgate, cw, cb, wcat, bcat, cdec, ng, tc=SCAN_CHUNK):
    bsz, s, _ = lx.shape
    tc = min(tc, s)
    nc = s // tc
    r8 = tc // SUBLANES
    nb8 = s // SUBLANES

    def specs(rev):
        cmap = (lambda b, c: (b, nc - 1 - c, 0)) if rev else (lambda b, c: (b, c, 0))
        if rev:
            pmap = lambda b, c: (b, jnp.maximum((nc - 1 - c) * r8 - 1, 0), 0)
            nmap = lambda b, c: (b, jnp.minimum((nc - c) * r8, nb8 - 1), 0)
        else:
            pmap = lambda b, c: (b, jnp.maximum(c * r8 - 1, 0), 0)
            nmap = lambda b, c: (b, jnp.minimum((c + 1) * r8, nb8 - 1), 0)
        main = pl.BlockSpec((1, tc, LRU_W), cmap)
        halo = [pl.BlockSpec((1, SUBLANES, LRU_W), pmap), pl.BlockSpec((1, SUBLANES, LRU_W), nmap)]
        return main, halo

    common = [_full((CONV_WIDTH, LRU_W)), _full((1, LRU_W)), _full((LRU_W, 2 * LRU_W)),
              _full((1, 2 * LRU_W)), _full((1, LRU_W))]
    main, halo = specs(False)
    hf = pl.pallas_call(
        functools.partial(_lru_body, False),
        grid=(bsz, nc),
        in_specs=[main] + halo + common,
        out_specs=main,
        out_shape=jax.ShapeDtypeStruct((bsz, s, LRU_W), F32),
        scratch_shapes=[pltpu.VMEM((SUBLANES, LRU_W), F32)],
        compiler_params=_params(("parallel", "arbitrary")),
        name="lru_fwd",
    )(lx, lx, lx, cw, cb, wcat[0], bcat[0], cdec[0])
    main, halo = specs(True)
    return pl.pallas_call(
        functools.partial(_lru_body, True),
        grid=(bsz, nc),
        in_specs=[main] + halo + [main, main] + common + [_full((1, LRU_W))],
        out_specs=main,
        out_shape=jax.ShapeDtypeStruct((bsz, s, LRU_W), F32),
        scratch_shapes=[pltpu.VMEM((SUBLANES, LRU_W), F32)],
        compiler_params=_params(("parallel", "arbitrary")),
        name="lru_rev",
    )(lx, lx, lx, lgate, hf, cw, cb, wcat[1], bcat[1], cdec[1], ng)


def _ret_log_gamma():
    return np.log1p(-np.exp2(-5.0 - np.arange(RET_HEADS, dtype=np.float64)))


@functools.lru_cache(maxsize=None)
def _ret_tables(c):
    lg = np.repeat(_ret_log_gamma(), HEAD_DIM)[None, :]
    idx = np.arange(c, dtype=np.float64)[:, None]
    dec = np.stack([np.exp((idx + 1.0) * lg),
                    np.exp((c - 1.0 - idx) * lg),
                    np.exp((c - idx) * lg),
                    np.exp(idx * lg)])
    chunk = np.exp(c * lg)
    dist = np.abs(idx - idx.T)
    intra = np.exp(dist[None] * _ret_log_gamma()[:, None, None])
    lane_head = np.arange(RET_W) // HEAD_DIM
    bd = (lane_head[:, None] == lane_head[None, :]).astype(np.float32)
    return (dec.astype(np.float32), chunk.astype(np.float32), intra.astype(np.float32), bd)


def _rope_tables(s):
    half = HEAD_DIM // 2
    freqs = ROPE_BASE ** (-jnp.arange(half, dtype=F32) / half)
    ang = jnp.arange(s, dtype=F32)[:, None] * freqs[None, :]
    cos = jnp.cos(ang)
    sin = jnp.sin(ang)
    cos_t = jnp.tile(jnp.concatenate([cos, cos], axis=1), (1, RET_W // HEAD_DIM))
    sin_t = jnp.tile(jnp.concatenate([-sin, sin], axis=1), (1, RET_W // HEAD_DIM))
    return cos_t, sin_t


def _rope(x, cos, sin_signed):
    lane = lax.broadcasted_iota(I32, x.shape, 1)
    w = x.shape[1]
    half = HEAD_DIM // 2
    swapped = jnp.where(lane % HEAD_DIM < half,
                        pltpu.roll(x, w - half, axis=1), pltpu.roll(x, half, axis=1))
    return x * cos + swapped * sin_signed


def _ret_body(rev, *refs):
    if rev:
        (q_ref, k_ref, v_ref, cos_ref, sin_ref, dec_ref, chunk_ref, bd_ref,
         of_ref, g_ref, ng_ref, o_ref, state_ref) = refs
    else:
        (q_ref, k_ref, v_ref, cos_ref, sin_ref, dec_ref, chunk_ref, bd_ref,
         intra_ref, o_ref, state_ref) = refs

    @pl.when(pl.program_id(1) == 0)
    def _():
        state_ref[...] = jnp.zeros_like(state_ref)

    cos = cos_ref[...]
    sin = sin_ref[...]
    q = _rope(q_ref[0], cos, sin)
    k = _rope(k_ref[0], cos, sin) * (HEAD_DIM ** -0.5)
    vb = v_ref[0].astype(BF16)
    qd, kd = (2, 3) if rev else (0, 1)
    state = state_ref[...]
    cross = jnp.dot((q * dec_ref[qd]).astype(BF16), state.astype(BF16), preferred_element_type=F32)
    kv = lax.dot_general((k * dec_ref[kd]).astype(BF16), vb, (((0,), (0,)), ((), ())),
                         preferred_element_type=F32)
    state_ref[...] = state * chunk_ref[...] + kv * bd_ref[...]
    if rev:
        o = of_ref[0] + cross
        ms = _group_sum(o * o, bd_ref[...].astype(BF16)) * (1.0 / HEAD_DIM)
        o = o * lax.rsqrt(ms + EPS) * ng_ref[...]
        o_ref[0] = jax.nn.silu(g_ref[0]) * o
    else:
        v = v_ref[0]
        upper = lax.broadcasted_iota(I32, (1, LANES), 1) >= HEAD_DIM
        pairs = []
        for pr in range(RET_W // LANES):
            lanes = slice(pr * LANES, (pr + 1) * LANES)
            kb = k[:, lanes].astype(BF16)
            acc = jnp.zeros((q.shape[0], LANES), F32)
            for hh in range(2):
                keep = upper if hh else jnp.logical_not(upper)
                s = lax.dot_general(jnp.where(keep, q[:, lanes], 0.0).astype(BF16), kb, (((1,), (1,)), ((), ())),
                                    preferred_element_type=F32)
                s = (s * intra_ref[2 * pr + hh]).astype(BF16)
                acc = acc + jnp.dot(s, jnp.where(keep, v[:, lanes], 0.0).astype(BF16), preferred_element_type=F32)
            pairs.append(acc)
        o_ref[0] = cross + jnp.concatenate(pairs, axis=1)


def ret_mixer(rq, rk, rv, rg, ng, c=RET_CHUNK):
    bsz, s, _ = rq.shape
    c = min(c, s)
    nc = s // c
    dec, chunk, intra, bd = _ret_tables(c)
    cos_t, sin_t = _rope_tables(s)

    def specs(rev):
        cmap = (lambda b, i: (b, nc - 1 - i, 0)) if rev else (lambda b, i: (b, i, 0))
        tmap = (lambda b, i: (nc - 1 - i, 0)) if rev else (lambda b, i: (i, 0))
        main = pl.BlockSpec((1, c, RET_W), cmap)
        tab = pl.BlockSpec((c, RET_W), tmap)
        return main, tab

    consts = [_full((4, c, RET_W)), _full((1, RET_W)), _full((RET_W, RET_W))]
    main, tab = specs(False)
    of = pl.pallas_call(
        functools.partial(_ret_body, False),
        grid=(bsz, nc),
        in_specs=[main, main, main, tab, tab] + consts + [_full((RET_HEADS, c, c))],
        out_specs=main,
        out_shape=jax.ShapeDtypeStruct((bsz, s, RET_W), F32),
        scratch_shapes=[pltpu.VMEM((RET_W, RET_W), F32)],
        compiler_params=_params(("parallel", "arbitrary")),
        name="ret_fwd",
    )(rq, rk, rv, cos_t, sin_t, dec, chunk, bd, intra)
    main, tab = specs(True)
    return pl.pallas_call(
        functools.partial(_ret_body, True),
        grid=(bsz, nc),
        in_specs=[main, main, main, tab, tab] + consts + [main, main, _full((1, RET_W))],
        out_specs=main,
        out_shape=jax.ShapeDtypeStruct((bsz, s, RET_W), F32),
        scratch_shapes=[pltpu.VMEM((RET_W, RET_W), F32)],
        compiler_params=_params(("parallel", "arbitrary")),
        name="ret_rev",
    )(rq, rk, rv, cos_t, sin_t, dec, chunk, bd, of, rg, ng.reshape(1, RET_W))


HEAD_SLOT = 128
LOG2E = 1.4426950408889634


@functools.lru_cache(maxsize=None)
def _attn_consts():
    lane = np.arange(DIFF_W)
    grp = lane // DIFF_HALF
    bd32 = (grp[:, None] == grp[None, :]).astype(np.float32)
    place = np.zeros((DIFF_W, DIFF_HEADS * HEAD_SLOT), np.float32)
    place[lane, (lane // HEAD_DIM) * HEAD_SLOT + lane % HEAD_DIM] = 1.0
    ones_col = np.zeros((1, DIFF_HEADS * HEAD_SLOT), np.float32)
    ones_col[0, np.arange(DIFF_HEADS) * HEAD_SLOT + HEAD_DIM] = 1.0
    return bd32, place, ones_col


def _attn_prep_body(q_ref, k_ref, v_ref, qg_ref, kg_ref, bd_ref, place_ref, ones_ref,
                    qn_ref, kn_ref, vv_ref):
    def qk_norm(x, g):
        ms = _group_sum(x * x, bd_ref[...]) * (1.0 / DIFF_HALF)
        return x * lax.rsqrt(ms + EPS) * g

    qn_ref[0] = (qk_norm(q_ref[0], qg_ref[...]) * (DIFF_HALF ** -0.5 * LOG2E)).astype(BF16)
    kn_ref[0] = qk_norm(k_ref[0], kg_ref[...]).astype(BF16)
    vb = v_ref[0].astype(BF16)
    vv_ref[0] = (jnp.dot(vb, place_ref[...], preferred_element_type=F32) + ones_ref[...]).astype(BF16)


def attn_prep(dq, dk, dv, qg, kg, tc=ROW_TILE):
    bsz, s, _ = dq.shape
    tc = min(tc, s)
    bd32, place, ones_col = _attn_consts()
    wide = DIFF_HEADS * HEAD_SLOT
    main = pl.BlockSpec((1, tc, DIFF_W), lambda b, c: (b, c, 0))
    outb = pl.BlockSpec((1, tc, wide), lambda b, c: (b, c, 0))
    rep = DIFF_W // DIFF_HALF
    return pl.pallas_call(
        _attn_prep_body,
        grid=(bsz, s // tc),
        in_specs=[main, main, main, _full((1, DIFF_W)), _full((1, DIFF_W)),
                  _full((DIFF_W, DIFF_W)), _full((DIFF_W, wide)), _full((1, wide))],
        out_specs=[main, main, outb],
        out_shape=[jax.ShapeDtypeStruct((bsz, s, DIFF_W), BF16)] * 2 + [jax.ShapeDtypeStruct((bsz, s, wide), BF16)],
        compiler_params=_params(("parallel", "parallel")),
        name="attn_prep",
    )(dq, dk, dv, jnp.tile(qg, rep).reshape(1, DIFF_W), jnp.tile(kg, rep).reshape(1, DIFF_W),
      jnp.asarray(bd32, BF16), jnp.asarray(place, BF16), ones_col)


def _t5_bucket_np(rel):
    nb = NUM_BUCKETS // 2
    max_exact = nb // 2
    n = np.abs(rel)
    nf = np.maximum(n, 1).astype(np.float64)
    large = max_exact + np.floor(2.0 * np.log2(nf / max_exact)).astype(np.int64)
    large = np.minimum(large, nb - 1)
    return (np.where(rel > 0, nb, 0) + np.where(n < max_exact, n, large)).astype(np.int32)


def _bias_tiles(rel_bias, t, tk):
    period = t + tk
    x = np.arange(period)[None, :]
    d = np.arange(-(tk // t), 2)[:, None]
    diag = rel_bias[_t5_bucket_np(x - (t - 1) + d * t)]
    diag = jnp.transpose(diag, (2, 0, 1))
    hankel = jnp.tile(diag, (1, 1, t + 1))[:, :, :t * (period + 1)].reshape(DIFF_HEADS, d.shape[0], t, period + 1)
    near = hankel[:, :, ::-1, :tk]
    nb = NUM_BUCKETS // 2
    left = jnp.broadcast_to(rel_bias[nb - 1][:, None, None, None], (DIFF_HEADS, 1, t, tk))
    right = jnp.broadcast_to(rel_bias[NUM_BUCKETS - 1][:, None, None, None], (DIFF_HEADS, 1, t, tk))
    return jnp.concatenate([left, near, right], axis=1) * LOG2E


def _attn_body(online, q_ref, k_ref, v_ref, bias_ref, lam_ref, linit_ref, g_ref, o_ref):
    t = q_ref.shape[1]
    tk = bias_ref.shape[3]
    kq = tk // t
    nk = k_ref.shape[1] // tk
    qi = pl.program_id(2)
    lam = (jnp.exp(jnp.sum(lam_ref[0:1, :] * lam_ref[1:2, :], axis=1, keepdims=True))
           - jnp.exp(jnp.sum(lam_ref[2:3, :] * lam_ref[3:4, :], axis=1, keepdims=True))
           + linit_ref[...])
    lane = lax.broadcasted_iota(I32, (t, HEAD_SLOT), 1)
    qf = q_ref[0].astype(F32)
    half = lane // DIFF_HALF
    qs = [jnp.concatenate([jnp.where(half == 2 * hh + m, qf, 0.0) for m in range(2)], axis=0).astype(BF16)
          for hh in range(2)]

    def body(j, carry):
        rows = pl.ds(pl.multiple_of(j * tk, tk), tk)
        bidx = jnp.clip(j * kq - qi, -kq - 1, 2) + kq + 1
        kb = k_ref[0, rows, :]
        new = []
        for hh in range(2):
            lo = hh * HEAD_SLOT
            m_i, acc = carry[hh]
            vb = v_ref[0, rows, lo:lo + HEAD_SLOT]
            s = lax.dot_general(qs[hh], kb, (((1,), (1,)), ((), ())), preferred_element_type=F32)
            bt = bias_ref[hh, bidx]
            s = s + jnp.concatenate([bt, bt], axis=0)
            if online:
                m_new = jnp.maximum(m_i, jnp.max(s, axis=1, keepdims=True))
                p = jnp.exp2(s - m_new)
                acc = jnp.exp2(m_i - m_new) * acc
            else:
                m_new = m_i
                p = jnp.exp2(s)
            acc = acc + jnp.dot(p.astype(BF16), vb, preferred_element_type=F32)
            new.append((m_new, acc))
        return tuple(new)

    init = (jnp.full((2 * t, 1) if online else (1, 1), -1e30, F32), jnp.zeros((2 * t, HEAD_SLOT), F32))
    res = lax.fori_loop(0, nk, body, (init, init), unroll=1 if online else 2)

    outs = []
    for hh in range(2):
        acc = res[hh][1]
        sm = acc / acc[:, HEAD_DIM:HEAD_DIM + 1]
        o = sm[:t] - lam * sm[t:]
        o = jnp.where(lane < HEAD_DIM, o, 0.0)
        ms = jnp.sum(o * o, axis=1, keepdims=True) * (1.0 / HEAD_DIM)
        outs.append(o * lax.rsqrt(ms + EPS))
    both = jnp.where(lane < HEAD_DIM, outs[0], pltpu.roll(outs[1], HEAD_DIM, axis=1))
    o_ref[0] = both * g_ref[...] * (1.0 - linit_ref[...])


MAX_SCORE_RANGE = 96.0


def _score_bound(qg, kg, rel_bias):
    qk = DIFF_HALF * (DIFF_HALF ** -0.5 * LOG2E) * jnp.max(jnp.abs(qg)) * jnp.max(jnp.abs(kg)) * 1.02
    hi = jnp.max(rel_bias) * LOG2E
    lo = jnp.min(rel_bias) * LOG2E
    return qk + hi, 2.0 * qk + (hi - lo)


def diff_attn(qn, kn, vv, bias, bound, spread, lam_vecs, lam_init, ng):
    bsz, s, _ = qn.shape
    _, n_tiles, t, tk = bias.shape
    assert t >= MAX_DISTANCE and tk % t == 0
    qspec = pl.BlockSpec((1, t, 2 * HEAD_DIM), lambda b, h, i: (b, i, h))
    kspec = pl.BlockSpec((1, s, 2 * HEAD_DIM), lambda b, h, i: (b, 0, h))
    vspec = pl.BlockSpec((1, s, 2 * HEAD_SLOT), lambda b, h, i: (b, 0, h))
    lam_pad = jnp.zeros((4, LANES), F32).at[:, :DIFF_HALF].set(lam_vecs)
    linit = jnp.full((1, LANES), lam_init, F32)
    g2 = jnp.tile(ng, 2).reshape(1, LANES)

    def call(online, bias_tiles):
        return pl.pallas_call(
            functools.partial(_attn_body, online),
            grid=(bsz, DIFF_HEADS // 2, s // t),
            in_specs=[qspec, kspec, vspec,
                      pl.BlockSpec((2, n_tiles, t, tk), lambda b, h, i: (h, 0, 0, 0)),
                      _full((4, LANES)), _full((1, LANES)), _full((1, LANES))],
            out_specs=pl.BlockSpec((1, t, 2 * HEAD_DIM), lambda b, h, i: (b, i, h)),
            out_shape=jax.ShapeDtypeStruct((bsz, s, DIFF_W), F32),
            compiler_params=_params(("parallel", "parallel", "arbitrary")),
            name="attn_online" if online else "attn",
        )(qn, kn, vv, bias_tiles, lam_pad, linit, g2)

    return lax.cond(spread <= MAX_SCORE_RANGE,
                    lambda: call(False, bias - bound), lambda: call(True, bias))


def _out_proj_body(x_ref, yl_ref, yr_ref, yd_ref, wl_ref, wr_ref, wd_ref, g_ref, whi_ref, wlo_ref,
                   h_ref, hn_ref, pt_ref):
    h = (x_ref[...]
         + jnp.dot(yl_ref[...].astype(BF16), wl_ref[...], preferred_element_type=F32)
         + jnp.dot(yr_ref[...].astype(BF16), wr_ref[...], preferred_element_type=F32)
         + jnp.dot(yd_ref[...].astype(BF16), wd_ref[...], preferred_element_type=F32))
    h_ref[...] = h
    hn = _rms(h, g_ref[...])
    hn_ref[...] = hn
    hn_hi, hn_lo = _split_bf16(hn)
    nt = (((1,), (1,)), ((), ()))
    logits_t = (lax.dot_general(whi_ref[...], hn_hi, nt, preferred_element_type=F32)
                + lax.dot_general(wlo_ref[...], hn_hi, nt, preferred_element_type=F32)
                + lax.dot_general(whi_ref[...], hn_lo, nt, preferred_element_type=F32))
    et = jnp.exp(logits_t - jnp.max(logits_t, axis=0, keepdims=True))
    pt_ref[...] = et / jnp.sum(et, axis=0, keepdims=True)


def out_proj(x2d, yl, yr, yd, w_out_bf16, g, w_router, tm=ROW_TILE):
    n = x2d.shape[0]
    tm = min(tm, n)
    row = lambda w: pl.BlockSpec((tm, w), lambda i: (i, 0))
    wr_hi, wr_lo = _split_bf16(w_router.T)
    return pl.pallas_call(
        _out_proj_body,
        grid=(n // tm,),
        in_specs=[row(D_MODEL), row(LRU_W), row(RET_W), row(DIFF_W),
                  _full((LRU_W, D_MODEL)), _full((RET_W, D_MODEL)), _full((DIFF_W, D_MODEL)),
                  _full((1, D_MODEL)), _full((N_EXPERTS, D_MODEL)), _full((N_EXPERTS, D_MODEL))],
        out_specs=[row(D_MODEL), row(D_MODEL), pl.BlockSpec((N_EXPERTS, tm), lambda i: (0, i))],
        out_shape=[jax.ShapeDtypeStruct((n, D_MODEL), F32), jax.ShapeDtypeStruct((n, D_MODEL), F32),
                   jax.ShapeDtypeStruct((N_EXPERTS, n), F32)],
        compiler_params=_params(("parallel",)),
        name="out_proj",
    )(x2d, yl, yr, yd, w_out_bf16[:LRU_W], w_out_bf16[LRU_W:LRU_W + RET_W], w_out_bf16[LRU_W + RET_W:],
      g.reshape(1, D_MODEL), wr_hi, wr_lo)


def _row_cumsum(x01, tri_ref, nr):
    within = jnp.dot(x01.astype(F32).astype(BF16), tri_ref[...], preferred_element_type=F32).astype(I32)
    tot = jnp.broadcast_to(within[:, SEL_ROW - 1:SEL_ROW], (nr, LANES))
    r = lax.broadcasted_iota(I32, (nr, LANES), 0)
    inc = tot
    d = 1
    while d < nr:
        inc = inc + jnp.where(r >= d, pltpu.roll(inc, d, axis=0), 0)
        d *= 2
    return within + (inc - tot)[:, 0:1]


def _select_body(cap, p_ref, tri_ref, cnt_ref, sel_ref):
    p = p_ref[0]
    nr = p.shape[0]
    bits = pltpu.bitcast(p, I32)

    def body(i, prefix):
        cand = prefix | (jnp.int32(1) << (30 - i))
        cnt = jnp.sum((bits >= cand).astype(I32), keepdims=True)
        return jnp.where(cnt >= cap, cand, prefix)

    thr = lax.fori_loop(0, 31, body, jnp.zeros((1, 1), I32))
    gt = bits > thr
    eq = bits == thr
    need = cap - jnp.sum(gt.astype(I32), keepdims=True)
    eq01 = eq.astype(I32)
    rank_eq = _row_cumsum(eq01, tri_ref, nr) - eq01
    sel = jnp.where(gt, 1, jnp.where(eq & (rank_eq < need), 1, 0))
    sel_ref[0] = sel
    cnt_ref[0] = _row_cumsum(sel, tri_ref, nr)


def select(probs_t, cap):
    e, n = probs_t.shape
    nr = n // SEL_ROW
    tri = np.triu(np.ones((SEL_ROW, SEL_ROW), np.float32))
    blk = pl.BlockSpec((1, nr, SEL_ROW), lambda i: (i, 0, 0))
    return pl.pallas_call(
        functools.partial(_select_body, cap),
        grid=(e,),
        in_specs=[blk, _full((SEL_ROW, SEL_ROW))],
        out_specs=[blk, blk],
        out_shape=[jax.ShapeDtypeStruct((e, nr, SEL_ROW), I32)] * 2,
        compiler_params=_params(("parallel",)),
        name="select",
    )(probs_t.reshape(e, nr, SEL_ROW), jnp.asarray(tri, BF16))


def _slot_index_body(lo_ref, hi_ref, pos_ref, o_ref):
    e = pl.program_id(0)
    n_sb, parts, w = o_ref.shape[1:]
    tok = lax.broadcasted_iota(I32, (parts, SEL_ROW), 1)
    part = lax.broadcasted_iota(I32, (parts, SEL_ROW), 0)
    base = jnp.where(part == 0, tok // 256, jnp.where(part == 1, tok % 256, 0))
    slot0 = lax.broadcasted_iota(I32, (w, 1), 0)

    nr = pos_ref.shape[1]
    group = min(SLOT_INDEX_ROWS, nr)

    def visit(sb, v):
        first = lo_ref[e, sb] + v * group
        start = jnp.minimum(first, nr - group)
        slots = slot0 + sb * w
        onehots, payloads = [], []
        for k in range(group):
            r = start + k
            onehots.append(jnp.where(pos_ref[0, pl.ds(r, 1), :] == slots, 1.0, 0.0).astype(BF16))
            row_part = jnp.where(part == 2, r, base)
            payloads.append(jnp.where(r >= first, row_part, 0).astype(F32).astype(BF16))
        return lax.dot_general(jnp.concatenate(payloads, axis=1), jnp.concatenate(onehots, axis=1),
                               (((1,), (1,)), ((), ())), preferred_element_type=F32)

    together = 2 if n_sb % 2 == 0 else 1

    def blocks(i, carry):
        sbs = [i * together + j for j in range(together)]
        firsts = [visit(sb, 0) for sb in sbs]
        for sb, acc in zip(sbs, firsts):
            visits = (hi_ref[e, sb] - lo_ref[e, sb]) // group + 1
            o_ref[0, sb] = lax.fori_loop(1, visits, lambda v, a, sb=sb: a + visit(sb, v), acc)
        return carry

    lax.fori_loop(0, n_sb // together, blocks, 0)


def slot_index(posm, row_end, cap, w):
    e, nr, _ = posm.shape
    assert nr <= 256 and SEL_ROW == 512
    n_sb = cap // w
    edges = jnp.arange(n_sb + 1, dtype=I32) * w
    lo = jnp.sum(row_end[:, None, :] <= edges[None, :-1, None], axis=2)
    hi = jnp.minimum(jnp.sum(row_end[:, None, :] < edges[None, 1:, None], axis=2), nr - 1)
    parts = pl.pallas_call(
        _slot_index_body,
        grid_spec=pltpu.PrefetchScalarGridSpec(
            num_scalar_prefetch=2,
            grid=(e,),
            in_specs=[pl.BlockSpec((1, nr, SEL_ROW), lambda i, lo, hi: (i, 0, 0))],
            out_specs=pl.BlockSpec((1, n_sb, SUBLANES, w), lambda i, lo, hi: (i, 0, 0, 0)),
        ),
        out_shape=jax.ShapeDtypeStruct((e, n_sb, SUBLANES, w), F32),
        compiler_params=_params(("arbitrary",)),
        name="slot_index",
    )(lo.astype(I32), hi.astype(I32), posm)
    idx = parts[:, :, 2] * SEL_ROW + parts[:, :, 0] * 256 + parts[:, :, 1]
    return idx.astype(I32).reshape(e * n_sb, 1, w)


def _ffn_body(idx_ref, idx_next_ref, x_hbm, wg_ref, wu_ref, wd_ref, o_ref, xbuf, sem):
    n_sb = pl.num_programs(1)
    step = pl.program_id(0) * n_sb + pl.program_id(1)
    n_steps = pl.num_programs(0) * n_sb
    half = step % 2
    w = xbuf.shape[1]

    def row_copy(idx_block, r, dst_half):
        return pltpu.make_async_copy(x_hbm.at[pl.ds(idx_block[0, 0, r], 1)], xbuf.at[dst_half, pl.ds(r, 1)],
                                     sem.at[dst_half])

    def wait_block(dst_half):
        for r in range(w):
            pltpu.make_async_copy(x_hbm.at[pl.ds(0, 1)], xbuf.at[dst_half, pl.ds(r, 1)], sem.at[dst_half]).wait()

    @pl.when(step == 0)
    def _():
        def body(r, carry):
            row_copy(idx_ref, r, 0).start()
            return carry

        lax.fori_loop(0, w, body, 0, unroll=8)

    wait_block(half)
    xe = xbuf[half].astype(BF16)
    for r in range(w):
        row_copy(idx_next_ref, r, 1 - half).start()

    acc = jnp.zeros((w, D_MODEL), F32)
    for c0 in range(0, D_FF, FF_CHUNK):
        g = jnp.dot(xe, wg_ref[0, :, c0:c0 + FF_CHUNK], preferred_element_type=F32)
        u = jnp.dot(xe, wu_ref[0, :, c0:c0 + FF_CHUNK], preferred_element_type=F32)
        mid = (jax.nn.silu(g) * u).astype(BF16)
        acc = acc + jnp.dot(mid, wd_ref[0, c0:c0 + FF_CHUNK, :], preferred_element_type=F32)
    o_ref[0] = acc.astype(BF16)

    @pl.when(step + 1 == n_steps)
    def _():
        wait_block(1 - half)


def expert_ffn(hn, idx, wg, wu, wd, cap, w):
    e = wg.shape[0]
    n_sb = cap // w
    last = e * n_sb - 1
    smem_block = lambda index_map: pl.BlockSpec((1, 1, w), index_map, memory_space=pltpu.SMEM)
    return pl.pallas_call(
        _ffn_body,
        grid=(e, n_sb),
        in_specs=[
            smem_block(lambda i, s: (i * n_sb + s, 0, 0)),
            smem_block(lambda i, s: (jnp.minimum(i * n_sb + s + 1, last), 0, 0)),
            pl.BlockSpec(memory_space=pl.ANY),
            pl.BlockSpec((1, D_MODEL, D_FF), lambda i, s: (i, 0, 0)),
            pl.BlockSpec((1, D_MODEL, D_FF), lambda i, s: (i, 0, 0)),
            pl.BlockSpec((1, D_FF, D_MODEL), lambda i, s: (i, 0, 0)),
        ],
        out_specs=pl.BlockSpec((1, w, D_MODEL), lambda i, s: (i, s, 0)),
        out_shape=jax.ShapeDtypeStruct((e, cap, D_MODEL), BF16),
        scratch_shapes=[pltpu.VMEM((2, w, D_MODEL), F32), pltpu.SemaphoreType.DMA((2,))],
        compiler_params=_params(("arbitrary", "arbitrary"), vmem_mb=58),
        name="ffn",
    )(idx, idx, hn, wg, wu, wd)


def _combine_body(ns_ref, ws_ref, wide_ref, h_ref, pos_ref, gate_ref, *refs):
    narrow_refs = refs[:N_EXPERTS]
    wide_refs = refs[N_EXPERTS:2 * N_EXPERTS]
    o_ref = refs[2 * N_EXPERTS]
    i = pl.program_id(0)
    pos = pos_ref[...]
    gate = gate_ref[...]
    wide = wide_ref[i]

    @pl.when(wide == 0)
    def _():
        lane = lax.broadcasted_iota(I32, (1, COMB_NARROW), 1)
        blocks = []
        for e in range(N_EXPERTS):
            slots = ns_ref[e, i] * COMB_ALIGN + lane
            blocks.append(jnp.where(pos[:, e:e + 1] == slots, gate[:, e:e + 1], 0.0).astype(BF16))
        g_all = jnp.concatenate(blocks, axis=1)
        rows = jnp.concatenate([r[...] for r in narrow_refs], axis=0)
        o_ref[...] = h_ref[...] + jnp.dot(g_all, rows, preferred_element_type=F32)

    @pl.when(wide != 0)
    def _():
        acc = h_ref[...]
        lane = lax.broadcasted_iota(I32, (1, COMB_WIN), 1)
        for e in range(N_EXPERTS):
            slots = ws_ref[e, i] * COMB_ALIGN + lane
            g = jnp.where(pos[:, e:e + 1] == slots, gate[:, e:e + 1], 0.0).astype(BF16)
            acc = acc + jnp.dot(g, wide_refs[e][...], preferred_element_type=F32)
        o_ref[...] = acc


def combine(h2d, pos_t, gates, ye, narrow_start, wide_start, wide_flag):
    n = h2d.shape[0]
    t = COMB_TILE
    row = lambda w: pl.BlockSpec((t, w), lambda i, ns, ws, fl: (i, 0))

    def narrow_spec(e):
        return pl.BlockSpec((None, pl.Element(COMB_NARROW), pl.Element(D_MODEL)),
                            lambda i, ns, ws, fl: (e, ns[e, i] * COMB_ALIGN, 0))

    def wide_spec(e):
        return pl.BlockSpec((None, pl.Element(COMB_WIN), pl.Element(D_MODEL)),
                            lambda i, ns, ws, fl: (e, ws[e, i] * COMB_ALIGN, 0))

    grid_spec = pltpu.PrefetchScalarGridSpec(
        num_scalar_prefetch=3,
        grid=(n // t,),
        in_specs=([row(D_MODEL), row(N_EXPERTS), row(N_EXPERTS)]
                  + [narrow_spec(e) for e in range(N_EXPERTS)] + [wide_spec(e) for e in range(N_EXPERTS)]),
        out_specs=row(D_MODEL),
    )
    return pl.pallas_call(
        _combine_body,
        grid_spec=grid_spec,
        out_shape=jax.ShapeDtypeStruct((n, D_MODEL), F32),
        compiler_params=_params(("arbitrary",)),
        name="combine",
    )(narrow_start, wide_start, wide_flag, h2d, pos_t, gates, *([ye] * (2 * N_EXPERTS)))


def ec_moe(h2d, hn, probs_t, wg, wu, wd):
    n = h2d.shape[0]
    probs = probs_t.T
    cap = EC_FACTOR * n // N_EXPERTS
    cnt, sel = select(probs_t, cap)
    posm = jnp.where(sel > 0, cnt - 1, -1)
    row_end = cnt[:, :, SEL_ROW - 1]
    w = min(SLOT_BLOCK, cap)
    ye = expert_ffn(hn, slot_index(posm, row_end, cap, w), wg, wu, wd, cap, w)
    pos_t = posm.reshape(N_EXPERTS, n).T
    base = (cnt - sel).reshape(N_EXPERTS, n)[:, ::COMB_TILE]
    stop = jnp.concatenate([base[:, 1:], jnp.full((N_EXPERTS, 1), cap, I32)], axis=1)
    narrow_start = jnp.minimum(base // COMB_ALIGN, (cap - COMB_NARROW) // COMB_ALIGN).astype(I32)
    wide_flag = jnp.any(stop > narrow_start * COMB_ALIGN + COMB_NARROW, axis=0)
    wide_start = jnp.where(wide_flag[None, :], jnp.minimum(base // COMB_ALIGN, (cap - COMB_WIN) // COMB_ALIGN), 0)
    return combine(h2d, pos_t, probs, ye, narrow_start, wide_start.astype(I32), wide_flag.astype(I32))


def _block_diag(w):
    h, d, _ = w.shape
    eye = jnp.eye(h, dtype=w.dtype)
    return jnp.einsum("hde,hg->hdge", w, eye).reshape(h * d, h * d)


def _prep_layer(l, p):
    wcat = jnp.stack([jnp.concatenate([_block_diag(p["lru_wa"][l, d]), _block_diag(p["lru_wx"][l, d])], axis=1)
                      for d in range(2)]).astype(BF16)
    bcat = jnp.stack([jnp.concatenate([p["lru_ba"][l, d], p["lru_bx"][l, d]])[None, :] for d in range(2)])
    cdec = (-LRU_C * jax.nn.softplus(-p["lru_lambda"][l]))[:, None, :]
    return dict(
        w_in=p["w_in"][l].astype(BF16), w_out=p["w_out"][l].astype(BF16),
        wcat=wcat, bcat=bcat, cdec=cdec,
        wg=p["w_gate"][l].astype(BF16), wu=p["w_up"][l].astype(BF16), wd=p["w_down"][l].astype(BF16))


def _trunk(x, p, prepped):
    bsz, s, _ = x.shape
    n = bsz * s
    x2d = x.reshape(n, D_MODEL)
    bias = _bias_tiles(p["rel_bias"], min(ATTN_TILE, s), min(ATTN_KEY_TILE, s))
    for l, w in enumerate(prepped):
        lam_init = 0.8 - 0.6 * math.exp(-0.3 * l)
        pieces = in_proj(x2d, p["ln1_g"][l], w["w_in"])
        lx, lgate, rq, rk, rv, rg, dq, dk, dv = [a.reshape(bsz, s, a.shape[1]) for a in pieces]
        y_lru = lru_mixer(lx, lgate, p["conv_w"][l], p["conv_b"][l].reshape(1, LRU_W), w["wcat"], w["bcat"],
                          w["cdec"], p["lru_norm_g"][l].reshape(1, LRU_W))
        y_ret = ret_mixer(rq, rk, rv, rg, p["ret_norm_g"][l])
        qn, kn, vv = attn_prep(dq, dk, dv, p["q_norm_g"][l], p["k_norm_g"][l])
        bound, spread = _score_bound(p["q_norm_g"][l], p["k_norm_g"][l], p["rel_bias"])
        y_diff = diff_attn(qn, kn, vv, bias, bound, spread, p["diff_lambda"][l], lam_init, p["diff_norm_g"][l])
        h2d, hn, probs_t = out_proj(x2d, y_lru.reshape(n, LRU_W), y_ret.reshape(n, RET_W),
                                    y_diff.reshape(n, DIFF_W), w["w_out"], p["ln2_g"][l], p["w_router"][l])
        x2d = ec_moe(h2d, hn, probs_t, w["wg"], w["wu"], w["wd"])
    return x2d.reshape(bsz, s, D_MODEL)


def kernel(x_prompt, x_sample, rel_bias, ln1_g, ln2_g, w_in, conv_w, conv_b, lru_wa, lru_ba, lru_wx, lru_bx,
           lru_lambda, lru_norm_g, ret_norm_g, q_norm_g, k_norm_g, diff_lambda, diff_norm_g, w_out, w_router,
           w_gate, w_up, w_down):
    p = dict(rel_bias=rel_bias, ln1_g=ln1_g, ln2_g=ln2_g, w_in=w_in, conv_w=conv_w, conv_b=conv_b,
             lru_wa=lru_wa, lru_ba=lru_ba, lru_wx=lru_wx, lru_bx=lru_bx, lru_lambda=lru_lambda,
             lru_norm_g=lru_norm_g, ret_norm_g=ret_norm_g, q_norm_g=q_norm_g, k_norm_g=k_norm_g,
             diff_lambda=diff_lambda, diff_norm_g=diff_norm_g, w_out=w_out, w_router=w_router,
             w_gate=w_gate, w_up=w_up, w_down=w_down)
    prepped = [_prep_layer(l, p) for l in range(w_in.shape[0])]
    return _trunk(x_prompt, p, prepped), _trunk(x_sample, p, prepped)
```

```python
import functools
import math

import numpy as np
import jax
import jax.numpy as jnp
from jax import lax
from jax.experimental import pallas as pl
from jax.experimental.pallas import tpu as pltpu

F32 = jnp.float32
BF16 = jnp.bfloat16
I32 = jnp.int32
HIGHEST = lax.Precision.HIGHEST

D_MODEL = 1024
HEAD_DIM = 64
LRU_W = 256
LRU_HEADS = 4
RET_W = 384
RET_HEADS = 6
DIFF_W = 384
DIFF_HEADS = 6
DIFF_HALF = 32
IN_SIZES = (LRU_W, LRU_W, RET_W, RET_W, RET_W, RET_W, DIFF_W, DIFF_W, DIFF_W)
IN_WIDTH = sum(IN_SIZES)
CONV_WIDTH = 4
LRU_C = 8.0
ROPE_BASE = 10000.0
NUM_BUCKETS = 32
MAX_DISTANCE = 128
N_EXPERTS = 16
EC_FACTOR = 2
D_FF = 2816
EPS = 1e-6

V7X_VMEM_BYTES = 64 * 1024 * 1024
SUBLANES = 8
LANES = 128

ROW_TILE = 512
SCAN_CHUNK = 256
RET_CHUNK = 256
ATTN_TILE = 512
ATTN_KEY_TILE = 512
SEL_ROW = 512
SLOT_BLOCK = 256
SLOT_INDEX_ROWS = 6
FF_CHUNK = 1408
COMB_TILE = 256
COMB_ALIGN = 16
COMB_WIN = COMB_TILE + COMB_ALIGN
COMB_NARROW = 64


def _params(sem, vmem_mb=48):
    return pltpu.CompilerParams(dimension_semantics=sem,
                                vmem_limit_bytes=vmem_mb * 1024 * 1024)


def _full(shape):
    nd = len(shape)
    return pl.BlockSpec(shape, lambda *_: (0,) * nd)


def _rms(x, g):
    return x * lax.rsqrt(jnp.mean(x * x, axis=-1, keepdims=True) + EPS) * g


def _split_bf16(x):
    hi = x.astype(BF16)
    return hi, (x - hi.astype(F32)).astype(BF16)


def _group_sum(x, ones_bf16):
    hi, lo = _split_bf16(x)
    return (jnp.dot(hi, ones_bf16, preferred_element_type=F32)
            + jnp.dot(lo, ones_bf16, preferred_element_type=F32))


def _in_proj_body(x_ref, g_ref, w_ref, *o_refs):
    xn = _rms(x_ref[...], g_ref[...]).astype(BF16)
    off = 0
    for o_ref, width in zip(o_refs, IN_SIZES):
        o_ref[...] = jnp.dot(xn, w_ref[:, off:off + width], preferred_element_type=F32)
        off += width


def in_proj(x2d, g, w_bf16, tm=ROW_TILE):
    n = x2d.shape[0]
    tm = min(tm, n)
    return pl.pallas_call(
        _in_proj_body,
        grid=(n // tm,),
        in_specs=[pl.BlockSpec((tm, D_MODEL), lambda i: (i, 0)),
                  _full((1, D_MODEL)), _full((D_MODEL, IN_WIDTH))],
        out_specs=[pl.BlockSpec((tm, w), lambda i: (i, 0)) for w in IN_SIZES],
        out_shape=[jax.ShapeDtypeStruct((n, w), F32) for w in IN_SIZES],
        compiler_params=_params(("parallel",)),
        name="in_proj",
    )(x2d, g.reshape(1, D_MODEL), w_bf16)


def _shift_rows(ext, s, tc):
    n = ext.shape[0]
    return pltpu.roll(ext, (-s) % n, axis=0)[SUBLANES:SUBLANES + tc]


def _neg_expm1(y):
    series = -y * (1.0 + y * (1.0 / 2) * (1.0 + y * (1.0 / 3) * (1.0 + y * (1.0 / 4) * (1.0 + y * (1.0 / 5)))))
    return jnp.where(y > -1.0 / 64, series, 1.0 - jnp.exp(y))


def _lru_scan(a, b, rev):
    tc = a.shape[0]
    t = lax.broadcasted_iota(I32, a.shape, 0)
    d = 1
    while d < tc:
        if d % SUBLANES:
            if rev:
                keep = t < tc - d
                a_o = pltpu.roll(a, tc - d, axis=0)
                b_o = pltpu.roll(b, tc - d, axis=0)
            else:
                keep = t >= d
                a_o = pltpu.roll(a, d, axis=0)
                b_o = pltpu.roll(b, d, axis=0)
            b = jnp.where(keep, a * b_o + b, b)
            a = jnp.where(keep, a * a_o, a)
        elif rev:
            b = jnp.concatenate([a[:tc - d] * b[d:] + b[:tc - d], b[tc - d:]], axis=0)
            a = jnp.concatenate([a[:tc - d] * a[d:], a[tc - d:]], axis=0)
        else:
            b = jnp.concatenate([b[:d], a[d:] * b[:tc - d] + b[d:]], axis=0)
            a = jnp.concatenate([a[:d], a[d:] * a[:tc - d]], axis=0)
        d *= 2
    return a, b


def _lru_body(rev, *refs):
    if rev:
        (x_ref, xp_ref, xn_ref, gate_ref, hf_ref, cw_ref, cb_ref, w_ref, b_ref, c_ref,
         ng_ref, o_ref, carry_ref) = refs
    else:
        (x_ref, xp_ref, xn_ref, cw_ref, cb_ref, w_ref, b_ref, c_ref, o_ref, carry_ref) = refs
    step = pl.program_id(1)
    nc = pl.num_programs(1)
    ci = nc - 1 - step if rev else step

    @pl.when(step == 0)
    def _():
        carry_ref[...] = jnp.zeros_like(carry_ref)

    x = x_ref[0]
    tc = x.shape[0]
    prev = xp_ref[0] * (ci > 0).astype(F32)
    nxt = xn_ref[0] * (ci < nc - 1).astype(F32)
    ext = jnp.concatenate([prev, x, nxt], axis=0)
    xc = cb_ref[...] + sum(cw_ref[j:j + 1, :] * _shift_rows(ext, j - CONV_WIDTH // 2, tc)
                           for j in range(CONV_WIDTH))
    z = jnp.dot(xc.astype(BF16), w_ref[...], preferred_element_type=F32) + b_ref[...]
    r = jax.nn.sigmoid(z[:, :LRU_W])
    i = jax.nn.sigmoid(z[:, LRU_W:])
    log_a = c_ref[...] * r
    a = jnp.exp(log_a)
    b = jnp.sqrt(_neg_expm1(2.0 * log_a)) * (i * xc)
    a_cum, h_loc = _lru_scan(a, b, rev)
    h = h_loc + a_cum * carry_ref[0:1, :]
    carry_ref[0:1, :] = h[0:1, :] if rev else h[tc - 1:tc, :]
    if rev:
        y = (hf_ref[0] + h) * jax.nn.gelu(gate_ref[0])
        o_ref[0] = _rms(y, ng_ref[...])
    else:
        o_ref[0] = h


def lru_mixer(lx, lgate, cw, cb, wcat, bcat, cdec, ng, tc=SCAN_CHUNK):
    bsz, s, _ = lx.shape
    tc = min(tc, s)
    nc = s // tc
    r8 = tc // SUBLANES
    nb8 = s // SUBLANES

    def specs(rev):
        cmap = (lambda b, c: (b, nc - 1 - c, 0)) if rev else (lambda b, c: (b, c, 0))
        if rev:
            pmap = lambda b, c: (b, jnp.maximum((nc - 1 - c) * r8 - 1, 0), 0)
            nmap = lambda b, c: (b, jnp.minimum((nc - c) * r8, nb8 - 1), 0)
        else:
            pmap = lambda b, c: (b, jnp.maximum(c * r8 - 1, 0), 0)
            nmap = lambda b, c: (b, jnp.minimum((c + 1) * r8, nb8 - 1), 0)
        main = pl.BlockSpec((1, tc, LRU_W), cmap)
        halo = [pl.BlockSpec((1, SUBLANES, LRU_W), pmap), pl.BlockSpec((1, SUBLANES, LRU_W), nmap)]
        return main, halo

    common = [_full((CONV_WIDTH, LRU_W)), _full((1, LRU_W)), _full((LRU_W, 2 * LRU_W)),
              _full((1, 2 * LRU_W)), _full((1, LRU_W))]
    main, halo = specs(False)
    hf = pl.pallas_call(
        functools.partial(_lru_body, False),
        grid=(bsz, nc),
        in_specs=[main] + halo + common,
        out_specs=main,
        out_shape=jax.ShapeDtypeStruct((bsz, s, LRU_W), F32),
        scratch_shapes=[pltpu.VMEM((SUBLANES, LRU_W), F32)],
        compiler_params=_params(("parallel", "arbitrary")),
        name="lru_fwd",
    )(lx, lx, lx, cw, cb, wcat[0], bcat[0], cdec[0])
    main, halo = specs(True)
    return pl.pallas_call(
        functools.partial(_lru_body, True),
        grid=(bsz, nc),
        in_specs=[main] + halo + [main, main] + common + [_full((1, LRU_W))],
        out_specs=main,
        out_shape=jax.ShapeDtypeStruct((bsz, s, LRU_W), F32),
        scratch_shapes=[pltpu.VMEM((SUBLANES, LRU_W), F32)],
        compiler_params=_params(("parallel", "arbitrary")),
        name="lru_rev",
    )(lx, lx, lx, lgate, hf, cw, cb, wcat[1], bcat[1], cdec[1], ng)


def _ret_log_gamma():
    return np.log1p(-np.exp2(-5.0 - np.arange(RET_HEADS, dtype=np.float64)))


@functools.lru_cache(maxsize=None)
def _ret_tables(c):
    lg = np.repeat(_ret_log_gamma(), HEAD_DIM)[None, :]
    idx = np.arange(c, dtype=np.float64)[:, None]
    dec = np.stack([np.exp((idx + 1.0) * lg),
                    np.exp((c - 1.0 - idx) * lg),
                    np.exp((c - idx) * lg),
                    np.exp(idx * lg)])
    chunk = np.exp(c * lg)
    dist = np.abs(idx - idx.T)
    intra = np.exp(dist[None] * _ret_log_gamma()[:, None, None])
    lane_head = np.arange(RET_W) // HEAD_DIM
    bd = (lane_head[:, None] == lane_head[None, :]).astype(np.float32)
    return (dec.astype(np.float32), chunk.astype(np.float32), intra.astype(np.float32), bd)


def _rope_tables(s):
    half = HEAD_DIM // 2
    freqs = ROPE_BASE ** (-jnp.arange(half, dtype=F32) / half)
    ang = jnp.arange(s, dtype=F32)[:, None] * freqs[None, :]
    cos = jnp.cos(ang)
    sin = jnp.sin(ang)
    cos_t = jnp.tile(jnp.concatenate([cos, cos], axis=1), (1, RET_W // HEAD_DIM))
    sin_t = jnp.tile(jnp.concatenate([-sin, sin], axis=1), (1, RET_W // HEAD_DIM))
    return cos_t, sin_t


def _rope(x, cos, sin_signed):
    lane = lax.broadcasted_iota(I32, x.shape, 1)
    w = x.shape[1]
    half = HEAD_DIM // 2
    swapped = jnp.where(lane % HEAD_DIM < half,
                        pltpu.roll(x, w - half, axis=1), pltpu.roll(x, half, axis=1))
    return x * cos + swapped * sin_signed


def _ret_body(rev, *refs):
    if rev:
        (q_ref, k_ref, v_ref, cos_ref, sin_ref, dec_ref, chunk_ref, bd_ref,
         of_ref, g_ref, ng_ref, o_ref, state_ref) = refs
    else:
        (q_ref, k_ref, v_ref, cos_ref, sin_ref, dec_ref, chunk_ref, bd_ref,
         intra_ref, o_ref, state_ref) = refs

    @pl.when(pl.program_id(1) == 0)
    def _():
        state_ref[...] = jnp.zeros_like(state_ref)

    cos = cos_ref[...]
    sin = sin_ref[...]
    q = _rope(q_ref[0], cos, sin)
    k = _rope(k_ref[0], cos, sin) * (HEAD_DIM ** -0.5)
    vb = v_ref[0].astype(BF16)
    qd, kd = (2, 3) if rev else (0, 1)
    state = state_ref[...]
    cross = jnp.dot((q * dec_ref[qd]).astype(BF16), state.astype(BF16), preferred_element_type=F32)
    kv = lax.dot_general((k * dec_ref[kd]).astype(BF16), vb, (((0,), (0,)), ((), ())),
                         preferred_element_type=F32)
    state_ref[...] = state * chunk_ref[...] + kv * bd_ref[...]
    if rev:
        o = of_ref[0] + cross
        ms = _group_sum(o * o, bd_ref[...].astype(BF16)) * (1.0 / HEAD_DIM)
        o = o * lax.rsqrt(ms + EPS) * ng_ref[...]
        o_ref[0] = jax.nn.silu(g_ref[0]) * o
    else:
        v = v_ref[0]
        upper = lax.broadcasted_iota(I32, (1, LANES), 1) >= HEAD_DIM
        pairs = []
        for pr in range(RET_W // LANES):
            lanes = slice(pr * LANES, (pr + 1) * LANES)
            kb = k[:, lanes].astype(BF16)
            acc = jnp.zeros((q.shape[0], LANES), F32)
            for hh in range(2):
                keep = upper if hh else jnp.logical_not(upper)
                s = lax.dot_general(jnp.where(keep, q[:, lanes], 0.0).astype(BF16), kb, (((1,), (1,)), ((), ())),
                                    preferred_element_type=F32)
                s = (s * intra_ref[2 * pr + hh]).astype(BF16)
                acc = acc + jnp.dot(s, jnp.where(keep, v[:, lanes], 0.0).astype(BF16), preferred_element_type=F32)
            pairs.append(acc)
        o_ref[0] = cross + jnp.concatenate(pairs, axis=1)


def ret_mixer(rq, rk, rv, rg, ng, c=RET_CHUNK):
    bsz, s, _ = rq.shape
    c = min(c, s)
    nc = s // c
    dec, chunk, intra, bd = _ret_tables(c)
    cos_t, sin_t = _rope_tables(s)

    def specs(rev):
        cmap = (lambda b, i: (b, nc - 1 - i, 0)) if rev else (lambda b, i: (b, i, 0))
        tmap = (lambda b, i: (nc - 1 - i, 0)) if rev else (lambda b, i: (i, 0))
        main = pl.BlockSpec((1, c, RET_W), cmap)
        tab = pl.BlockSpec((c, RET_W), tmap)
        return main, tab

    consts = [_full((4, c, RET_W)), _full((1, RET_W)), _full((RET_W, RET_W))]
    main, tab = specs(False)
    of = pl.pallas_call(
        functools.partial(_ret_body, False),
        grid=(bsz, nc),
        in_specs=[main, main, main, tab, tab] + consts + [_full((RET_HEADS, c, c))],
        out_specs=main,
        out_shape=jax.ShapeDtypeStruct((bsz, s, RET_W), F32),
        scratch_shapes=[pltpu.VMEM((RET_W, RET_W), F32)],
        compiler_params=_params(("parallel", "arbitrary")),
        name="ret_fwd",
    )(rq, rk, rv, cos_t, sin_t, dec, chunk, bd, intra)
    main, tab = specs(True)
    return pl.pallas_call(
        functools.partial(_ret_body, True),
        grid=(bsz, nc),
        in_specs=[main, main, main, tab, tab] + consts + [main, main, _full((1, RET_W))],
        out_specs=main,
        out_shape=jax.ShapeDtypeStruct((bsz, s, RET_W), F32),
        scratch_shapes=[pltpu.VMEM((RET_W, RET_W), F32)],
        compiler_params=_params(("parallel", "arbitrary")),
        name="ret_rev",
    )(rq, rk, rv, cos_t, sin_t, dec, chunk, bd, of, rg, ng.reshape(1, RET_W))


HEAD_SLOT = 128
LOG2E = 1.4426950408889634


@functools.lru_cache(maxsize=None)
def _attn_consts():
    lane = np.arange(DIFF_W)
    grp = lane // DIFF_HALF
    bd32 = (grp[:, None] == grp[None, :]).astype(np.float32)
    place = np.zeros((DIFF_W, DIFF_HEADS * HEAD_SLOT), np.float32)
    place[lane, (lane // HEAD_DIM) * HEAD_SLOT + lane % HEAD_DIM] = 1.0
    ones_col = np.zeros((1, DIFF_HEADS * HEAD_SLOT), np.float32)
    ones_col[0, np.arange(DIFF_HEADS) * HEAD_SLOT + HEAD_DIM] = 1.0
    return bd32, place, ones_col


def _attn_prep_body(q_ref, k_ref, v_ref, qg_ref, kg_ref, bd_ref, place_ref, ones_ref,
                    qn_ref, kn_ref, vv_ref):
    def qk_norm(x, g):
        ms = _group_sum(x * x, bd_ref[...]) * (1.0 / DIFF_HALF)
        return x * lax.rsqrt(ms + EPS) * g

    qn_ref[0] = (qk_norm(q_ref[0], qg_ref[...]) * (DIFF_HALF ** -0.5 * LOG2E)).astype(BF16)
    kn_ref[0] = qk_norm(k_ref[0], kg_ref[...]).astype(BF16)
    vb = v_ref[0].astype(BF16)
    vv_ref[0] = (jnp.dot(vb, place_ref[...], preferred_element_type=F32) + ones_ref[...]).astype(BF16)


def attn_prep(dq, dk, dv, qg, kg, tc=ROW_TILE):
    bsz, s, _ = dq.shape
    tc = min(tc, s)
    bd32, place, ones_col = _attn_consts()
    wide = DIFF_HEADS * HEAD_SLOT
    main = pl.BlockSpec((1, tc, DIFF_W), lambda b, c: (b, c, 0))
    outb = pl.BlockSpec((1, tc, wide), lambda b, c: (b, c, 0))
    rep = DIFF_W // DIFF_HALF
    return pl.pallas_call(
        _attn_prep_body,
        grid=(bsz, s // tc),
        in_specs=[main, main, main, _full((1, DIFF_W)), _full((1, DIFF_W)),
                  _full((DIFF_W, DIFF_W)), _full((DIFF_W, wide)), _full((1, wide))],
        out_specs=[main, main, outb],
        out_shape=[jax.ShapeDtypeStruct((bsz, s, DIFF_W), BF16)] * 2 + [jax.ShapeDtypeStruct((bsz, s, wide), BF16)],
        compiler_params=_params(("parallel", "parallel")),
        name="attn_prep",
    )(dq, dk, dv, jnp.tile(qg, rep).reshape(1, DIFF_W), jnp.tile(kg, rep).reshape(1, DIFF_W),
      jnp.asarray(bd32, BF16), jnp.asarray(place, BF16), ones_col)


def _t5_bucket_np(rel):
    nb = NUM_BUCKETS // 2
    max_exact = nb // 2
    n = np.abs(rel)
    nf = np.maximum(n, 1).astype(np.float64)
    large = max_exact + np.floor(2.0 * np.log2(nf / max_exact)).astype(np.int64)
    large = np.minimum(large, nb - 1)
    return (np.where(rel > 0, nb, 0) + np.where(n < max_exact, n, large)).astype(np.int32)


def _bias_tiles(rel_bias, t, tk):
    period = t + tk
    x = np.arange(period)[None, :]
    d = np.arange(-(tk // t), 2)[:, None]
    diag = rel_bias[_t5_bucket_np(x - (t - 1) + d * t)]
    diag = jnp.transpose(diag, (2, 0, 1))
    hankel = jnp.tile(diag, (1, 1, t + 1))[:, :, :t * (period + 1)].reshape(DIFF_HEADS, d.shape[0], t, period + 1)
    near = hankel[:, :, ::-1, :tk]
    nb = NUM_BUCKETS // 2
    left = jnp.broadcast_to(rel_bias[nb - 1][:, None, None, None], (DIFF_HEADS, 1, t, tk))
    right = jnp.broadcast_to(rel_bias[NUM_BUCKETS - 1][:, None, None, None], (DIFF_HEADS, 1, t, tk))
    return jnp.concatenate([left, near, right], axis=1) * LOG2E


def _attn_body(online, q_ref, k_ref, v_ref, bias_ref, lam_ref, linit_ref, g_ref, o_ref):
    t = q_ref.shape[1]
    tk = bias_ref.shape[3]
    kq = tk // t
    nk = k_ref.shape[1] // tk
    qi = pl.program_id(2)
    lam = (jnp.exp(jnp.sum(lam_ref[0:1, :] * lam_ref[1:2, :], axis=1, keepdims=True))
           - jnp.exp(jnp.sum(lam_ref[2:3, :] * lam_ref[3:4, :], axis=1, keepdims=True))
           + linit_ref[...])
    lane = lax.broadcasted_iota(I32, (t, HEAD_SLOT), 1)
    qf = q_ref[0].astype(F32)
    half = lane // DIFF_HALF
    qs = [jnp.concatenate([jnp.where(half == 2 * hh + m, qf, 0.0) for m in range(2)], axis=0).astype(BF16)
          for hh in range(2)]

    def body(j, carry):
        rows = pl.ds(pl.multiple_of(j * tk, tk), tk)
        bidx = jnp.clip(j * kq - qi, -kq - 1, 2) + kq + 1
        kb = k_ref[0, rows, :]
        new = []
        for hh in range(2):
            lo = hh * HEAD_SLOT
            m_i, acc = carry[hh]
            vb = v_ref[0, rows, lo:lo + HEAD_SLOT]
            s = lax.dot_general(qs[hh], kb, (((1,), (1,)), ((), ())), preferred_element_type=F32)
            bt = bias_ref[hh, bidx]
            s = s + jnp.concatenate([bt, bt], axis=0)
            if online:
                m_new = jnp.maximum(m_i, jnp.max(s, axis=1, keepdims=True))
                p = jnp.exp2(s - m_new)
                acc = jnp.exp2(m_i - m_new) * acc
            else:
                m_new = m_i
                p = jnp.exp2(s)
            acc = acc + jnp.dot(p.astype(BF16), vb, preferred_element_type=F32)
            new.append((m_new, acc))
        return tuple(new)

    init = (jnp.full((2 * t, 1) if online else (1, 1), -1e30, F32), jnp.zeros((2 * t, HEAD_SLOT), F32))
    res = lax.fori_loop(0, nk, body, (init, init), unroll=1 if online else 2)

    outs = []
    for hh in range(2):
        acc = res[hh][1]
        sm = acc / acc[:, HEAD_DIM:HEAD_DIM + 1]
        o = sm[:t] - lam * sm[t:]
        o = jnp.where(lane < HEAD_DIM, o, 0.0)
        ms = jnp.sum(o * o, axis=1, keepdims=True) * (1.0 / HEAD_DIM)
        outs.append(o * lax.rsqrt(ms + EPS))
    both = jnp.where(lane < HEAD_DIM, outs[0], pltpu.roll(outs[1], HEAD_DIM, axis=1))
    o_ref[0] = both * g_ref[...] * (1.0 - linit_ref[...])


MAX_SCORE_RANGE = 96.0


def _score_bound(qg, kg, rel_bias):
    qk = DIFF_HALF * (DIFF_HALF ** -0.5 * LOG2E) * jnp.max(jnp.abs(qg)) * jnp.max(jnp.abs(kg)) * 1.02
    hi = jnp.max(rel_bias) * LOG2E
    lo = jnp.min(rel_bias) * LOG2E
    return qk + hi, 2.0 * qk + (hi - lo)


def diff_attn(qn, kn, vv, bias, bound, spread, lam_vecs, lam_init, ng):
    bsz, s, _ = qn.shape
    _, n_tiles, t, tk = bias.shape
    assert t >= MAX_DISTANCE and tk % t == 0
    qspec = pl.BlockSpec((1, t, 2 * HEAD_DIM), lambda b, h, i: (b, i, h))
    kspec = pl.BlockSpec((1, s, 2 * HEAD_DIM), lambda b, h, i: (b, 0, h))
    vspec = pl.BlockSpec((1, s, 2 * HEAD_SLOT), lambda b, h, i: (b, 0, h))
    lam_pad = jnp.zeros((4, LANES), F32).at[:, :DIFF_HALF].set(lam_vecs)
    linit = jnp.full((1, LANES), lam_init, F32)
    g2 = jnp.tile(ng, 2).reshape(1, LANES)

    def call(online, bias_tiles):
        return pl.pallas_call(
            functools.partial(_attn_body, online),
            grid=(bsz, DIFF_HEADS // 2, s // t),
            in_specs=[qspec, kspec, vspec,
                      pl.BlockSpec((2, n_tiles, t, tk), lambda b, h, i: (h, 0, 0, 0)),
                      _full((4, LANES)), _full((1, LANES)), _full((1, LANES))],
            out_specs=pl.BlockSpec((1, t, 2 * HEAD_DIM), lambda b, h, i: (b, i, h)),
            out_shape=jax.ShapeDtypeStruct((bsz, s, DIFF_W), F32),
            compiler_params=_params(("parallel", "parallel", "arbitrary")),
            name="attn_online" if online else "attn",
        )(qn, kn, vv, bias_tiles, lam_pad, linit, g2)

    return lax.cond(spread <= MAX_SCORE_RANGE,
                    lambda: call(False, bias - bound), lambda: call(True, bias))


def _out_proj_body(x_ref, yl_ref, yr_ref, yd_ref, wl_ref, wr_ref, wd_ref, g_ref, whi_ref, wlo_ref,
                   h_ref, hn_ref, pt_ref):
    h = (x_ref[...]
         + jnp.dot(yl_ref[...].astype(BF16), wl_ref[...], preferred_element_type=F32)
         + jnp.dot(yr_ref[...].astype(BF16), wr_ref[...], preferred_element_type=F32)
         + jnp.dot(yd_ref[...].astype(BF16), wd_ref[...], preferred_element_type=F32))
    h_ref[...] = h
    hn = _rms(h, g_ref[...])
    hn_ref[...] = hn
    hn_hi, hn_lo = _split_bf16(hn)
    nt = (((1,), (1,)), ((), ()))
    logits_t = (lax.dot_general(whi_ref[...], hn_hi, nt, preferred_element_type=F32)
                + lax.dot_general(wlo_ref[...], hn_hi, nt, preferred_element_type=F32)
                + lax.dot_general(whi_ref[...], hn_lo, nt, preferred_element_type=F32))
    et = jnp.exp(logits_t - jnp.max(logits_t, axis=0, keepdims=True))
    pt_ref[...] = et / jnp.sum(et, axis=0, keepdims=True)


def out_proj(x2d, yl, yr, yd, w_out_bf16, g, w_router, tm=ROW_TILE):
    n = x2d.shape[0]
    tm = min(tm, n)
    row = lambda w: pl.BlockSpec((tm, w), lambda i: (i, 0))
    wr_hi, wr_lo = _split_bf16(w_router.T)
    return pl.pallas_call(
        _out_proj_body,
        grid=(n // tm,),
        in_specs=[row(D_MODEL), row(LRU_W), row(RET_W), row(DIFF_W),
                  _full((LRU_W, D_MODEL)), _full((RET_W, D_MODEL)), _full((DIFF_W, D_MODEL)),
                  _full((1, D_MODEL)), _full((N_EXPERTS, D_MODEL)), _full((N_EXPERTS, D_MODEL))],
        out_specs=[row(D_MODEL), row(D_MODEL), pl.BlockSpec((N_EXPERTS, tm), lambda i: (0, i))],
        out_shape=[jax.ShapeDtypeStruct((n, D_MODEL), F32), jax.ShapeDtypeStruct((n, D_MODEL), F32),
                   jax.ShapeDtypeStruct((N_EXPERTS, n), F32)],
        compiler_params=_params(("parallel",)),
        name="out_proj",
    )(x2d, yl, yr, yd, w_out_bf16[:LRU_W], w_out_bf16[LRU_W:LRU_W + RET_W], w_out_bf16[LRU_W + RET_W:],
      g.reshape(1, D_MODEL), wr_hi, wr_lo)


def _row_cumsum(x01, tri_ref, nr):
    within = jnp.dot(x01.astype(F32).astype(BF16), tri_ref[...], preferred_element_type=F32).astype(I32)
    tot = jnp.broadcast_to(within[:, SEL_ROW - 1:SEL_ROW], (nr, LANES))
    r = lax.broadcasted_iota(I32, (nr, LANES), 0)
    inc = tot
    d = 1
    while d < nr:
        inc = inc + jnp.where(r >= d, pltpu.roll(inc, d, axis=0), 0)
        d *= 2
    return within + (inc - tot)[:, 0:1]


def _select_body(cap, p_ref, tri_ref, cnt_ref, sel_ref):
    p = p_ref[0]
    nr = p.shape[0]
    bits = pltpu.bitcast(p, I32)

    def body(i, prefix):
        cand = prefix | (jnp.int32(1) << (30 - i))
        cnt = jnp.sum((bits >= cand).astype(I32), keepdims=True)
        return jnp.where(cnt >= cap, cand, prefix)

    thr = lax.fori_loop(0, 31, body, jnp.zeros((1, 1), I32))
    gt = bits > thr
    eq = bits == thr
    need = cap - jnp.sum(gt.astype(I32), keepdims=True)
    eq01 = eq.astype(I32)
    rank_eq = _row_cumsum(eq01, tri_ref, nr) - eq01
    sel = jnp.where(gt, 1, jnp.where(eq & (rank_eq < need), 1, 0))
    sel_ref[0] = sel
    cnt_ref[0] = _row_cumsum(sel, tri_ref, nr)


def select(probs_t, cap):
    e, n = probs_t.shape
    nr = n // SEL_ROW
    tri = np.triu(np.ones((SEL_ROW, SEL_ROW), np.float32))
    blk = pl.BlockSpec((1, nr, SEL_ROW), lambda i: (i, 0, 0))
    return pl.pallas_call(
        functools.partial(_select_body, cap),
        grid=(e,),
        in_specs=[blk, _full((SEL_ROW, SEL_ROW))],
        out_specs=[blk, blk],
        out_shape=[jax.ShapeDtypeStruct((e, nr, SEL_ROW), I32)] * 2,
        compiler_params=_params(("parallel",)),
        name="select",
    )(probs_t.reshape(e, nr, SEL_ROW), jnp.asarray(tri, BF16))


def _slot_index_body(lo_ref, hi_ref, pos_ref, o_ref):
    e = pl.program_id(0)
    n_sb, parts, w = o_ref.shape[1:]
    tok = lax.broadcasted_iota(I32, (parts, SEL_ROW), 1)
    part = lax.broadcasted_iota(I32, (parts, SEL_ROW), 0)
    base = jnp.where(part == 0, tok // 256, jnp.where(part == 1, tok % 256, 0))
    slot0 = lax.broadcasted_iota(I32, (w, 1), 0)

    nr = pos_ref.shape[1]
    group = min(SLOT_INDEX_ROWS, nr)

    def visit(sb, v):
        first = lo_ref[e, sb] + v * group
        start = jnp.minimum(first, nr - group)
        slots = slot0 + sb * w
        onehots, payloads = [], []
        for k in range(group):
            r = start + k
            onehots.append(jnp.where(pos_ref[0, pl.ds(r, 1), :] == slots, 1.0, 0.0).astype(BF16))
            row_part = jnp.where(part == 2, r, base)
            payloads.append(jnp.where(r >= first, row_part, 0).astype(F32).astype(BF16))
        return lax.dot_general(jnp.concatenate(payloads, axis=1), jnp.concatenate(onehots, axis=1),
                               (((1,), (1,)), ((), ())), preferred_element_type=F32)

    together = 2 if n_sb % 2 == 0 else 1

    def blocks(i, carry):
        sbs = [i * together + j for j in range(together)]
        firsts = [visit(sb, 0) for sb in sbs]
        for sb, acc in zip(sbs, firsts):
            visits = (hi_ref[e, sb] - lo_ref[e, sb]) // group + 1
            o_ref[0, sb] = lax.fori_loop(1, visits, lambda v, a, sb=sb: a + visit(sb, v), acc)
        return carry

    lax.fori_loop(0, n_sb // together, blocks, 0)


def slot_index(posm, row_end, cap, w):
    e, nr, _ = posm.shape
    assert nr <= 256 and SEL_ROW == 512
    n_sb = cap // w
    edges = jnp.arange(n_sb + 1, dtype=I32) * w
    lo = jnp.sum(row_end[:, None, :] <= edges[None, :-1, None], axis=2)
    hi = jnp.minimum(jnp.sum(row_end[:, None, :] < edges[None, 1:, None], axis=2), nr - 1)
    parts = pl.pallas_call(
        _slot_index_body,
        grid_spec=pltpu.PrefetchScalarGridSpec(
            num_scalar_prefetch=2,
            grid=(e,),
            in_specs=[pl.BlockSpec((1, nr, SEL_ROW), lambda i, lo, hi: (i, 0, 0))],
            out_specs=pl.BlockSpec((1, n_sb, SUBLANES, w), lambda i, lo, hi: (i, 0, 0, 0)),
        ),
        out_shape=jax.ShapeDtypeStruct((e, n_sb, SUBLANES, w), F32),
        compiler_params=_params(("arbitrary",)),
        name="slot_index",
    )(lo.astype(I32), hi.astype(I32), posm)
    idx = parts[:, :, 2] * SEL_ROW + parts[:, :, 0] * 256 + parts[:, :, 1]
    return idx.astype(I32).reshape(e * n_sb, 1, w)


def _ffn_body(idx_ref, idx_next_ref, x_hbm, wg_ref, wu_ref, wd_ref, o_ref, xbuf, sem):
    n_sb = pl.num_programs(1)
    step = pl.program_id(0) * n_sb + pl.program_id(1)
    n_steps = pl.num_programs(0) * n_sb
    half = step % 2
    w = xbuf.shape[1]

    def row_copy(idx_block, r, dst_half):
        return pltpu.make_async_copy(x_hbm.at[pl.ds(idx_block[0, 0, r], 1)], xbuf.at[dst_half, pl.ds(r, 1)],
                                     sem.at[dst_half])

    def wait_block(dst_half):
        for r in range(w):
            pltpu.make_async_copy(x_hbm.at[pl.ds(0, 1)], xbuf.at[dst_half, pl.ds(r, 1)], sem.at[dst_half]).wait()

    @pl.when(step == 0)
    def _():
        def body(r, carry):
            row_copy(idx_ref, r, 0).start()
            return carry

        lax.fori_loop(0, w, body, 0, unroll=8)

    wait_block(half)
    for r in range(w):
        row_copy(idx_next_ref, r, 1 - half).start()

    xe = xbuf[half].astype(BF16)
    acc = jnp.zeros((w, D_MODEL), F32)
    for c0 in range(0, D_FF, FF_CHUNK):
        g = jnp.dot(xe, wg_ref[0, :, c0:c0 + FF_CHUNK], preferred_element_type=F32)
        u = jnp.dot(xe, wu_ref[0, :, c0:c0 + FF_CHUNK], preferred_element_type=F32)
        mid = (jax.nn.silu(g) * u).astype(BF16)
        acc = acc + jnp.dot(mid, wd_ref[0, c0:c0 + FF_CHUNK, :], preferred_element_type=F32)
    o_ref[0] = acc.astype(BF16)

    @pl.when(step + 1 == n_steps)
    def _():
        wait_block(1 - half)


def expert_ffn(hn, idx, wg, wu, wd, cap, w):
    e = wg.shape[0]
    n_sb = cap // w
    last = e * n_sb - 1
    smem_block = lambda index_map: pl.BlockSpec((1, 1, w), index_map, memory_space=pltpu.SMEM)
    return pl.pallas_call(
        _ffn_body,
        grid=(e, n_sb),
        in_specs=[
            smem_block(lambda i, s: (i * n_sb + s, 0, 0)),
            smem_block(lambda i, s: (jnp.minimum(i * n_sb + s + 1, last), 0, 0)),
            pl.BlockSpec(memory_space=pl.ANY),
            pl.BlockSpec((1, D_MODEL, D_FF), lambda i, s: (i, 0, 0)),
            pl.BlockSpec((1, D_MODEL, D_FF), lambda i, s: (i, 0, 0)),
            pl.BlockSpec((1, D_FF, D_MODEL), lambda i, s: (i, 0, 0)),
        ],
        out_specs=pl.BlockSpec((1, w, D_MODEL), lambda i, s: (i, s, 0)),
        out_shape=jax.ShapeDtypeStruct((e, cap, D_MODEL), BF16),
        scratch_shapes=[pltpu.VMEM((2, w, D_MODEL), F32), pltpu.SemaphoreType.DMA((2,))],
        compiler_params=_params(("arbitrary", "arbitrary"), vmem_mb=58),
        name="ffn",
    )(idx, idx, hn, wg, wu, wd)


def _combine_body(ns_ref, ws_ref, wide_ref, h_ref, pos_ref, gate_ref, *refs):
    narrow_refs = refs[:N_EXPERTS]
    wide_refs = refs[N_EXPERTS:2 * N_EXPERTS]
    o_ref = refs[2 * N_EXPERTS]
    i = pl.program_id(0)
    pos = pos_ref[...]
    gate = gate_ref[...]
    wide = wide_ref[i]

    @pl.when(wide == 0)
    def _():
        lane = lax.broadcasted_iota(I32, (COMB_TILE, LANES), 1)
        first = lane < COMB_NARROW
        off = jnp.where(first, lane, lane - COMB_NARROW)
        blocks = []
        for e in range(0, N_EXPERTS, 2):
            slots = jnp.where(first, ns_ref[e, i] * COMB_ALIGN, ns_ref[e + 1, i] * COMB_ALIGN) + off
            p2 = jnp.where(first, pos[:, e:e + 1], pos[:, e + 1:e + 2])
            g2 = jnp.where(first, gate[:, e:e + 1], gate[:, e + 1:e + 2])
            blocks.append(jnp.where(p2 == slots, g2, 0.0).astype(BF16))
        g_all = jnp.concatenate(blocks, axis=1)
        rows = jnp.concatenate([r[...] for r in narrow_refs], axis=0)
        o_ref[...] = h_ref[...] + jnp.dot(g_all, rows, preferred_element_type=F32)

    @pl.when(wide != 0)
    def _():
        acc = h_ref[...]
        lane = lax.broadcasted_iota(I32, (1, COMB_WIN), 1)
        for e in range(N_EXPERTS):
            slots = ws_ref[e, i] * COMB_ALIGN + lane
            g = jnp.where(pos[:, e:e + 1] == slots, gate[:, e:e + 1], 0.0).astype(BF16)
            acc = acc + jnp.dot(g, wide_refs[e][...], preferred_element_type=F32)
        o_ref[...] = acc


def combine(h2d, pos_t, gates, ye, narrow_start, wide_start, wide_flag):
    n = h2d.shape[0]
    t = COMB_TILE
    row = lambda w: pl.BlockSpec((t, w), lambda i, ns, ws, fl: (i, 0))

    def narrow_spec(e):
        return pl.BlockSpec((None, pl.Element(COMB_NARROW), pl.Element(D_MODEL)),
                            lambda i, ns, ws, fl: (e, ns[e, i] * COMB_ALIGN, 0))

    def wide_spec(e):
        return pl.BlockSpec((None, pl.Element(COMB_WIN), pl.Element(D_MODEL)),
                            lambda i, ns, ws, fl: (e, ws[e, i] * COMB_ALIGN, 0))

    grid_spec = pltpu.PrefetchScalarGridSpec(
        num_scalar_prefetch=3,
        grid=(n // t,),
        in_specs=([row(D_MODEL), row(N_EXPERTS), row(N_EXPERTS)]
                  + [narrow_spec(e) for e in range(N_EXPERTS)] + [wide_spec(e) for e in range(N_EXPERTS)]),
        out_specs=row(D_MODEL),
    )
    return pl.pallas_call(
        _combine_body,
        grid_spec=grid_spec,
        out_shape=jax.ShapeDtypeStruct((n, D_MODEL), F32),
        compiler_params=_params(("arbitrary",)),
        name="combine",
    )(narrow_start, wide_start, wide_flag, h2d, pos_t, gates, *([ye] * (2 * N_EXPERTS)))


def ec_moe(h2d, hn, probs_t, wg, wu, wd):
    n = h2d.shape[0]
    probs = probs_t.T
    cap = EC_FACTOR * n // N_EXPERTS
    cnt, sel = select(probs_t, cap)
    posm = jnp.where(sel > 0, cnt - 1, -1)
    row_end = cnt[:, :, SEL_ROW - 1]
    w = min(SLOT_BLOCK, cap)
    ye = expert_ffn(hn, slot_index(posm, row_end, cap, w), wg, wu, wd, cap, w)
    pos_t = posm.reshape(N_EXPERTS, n).T
    base = (cnt - sel).reshape(N_EXPERTS, n)[:, ::COMB_TILE]
    stop = jnp.concatenate([base[:, 1:], jnp.full((N_EXPERTS, 1), cap, I32)], axis=1)
    narrow_start = jnp.minimum(base // COMB_ALIGN, (cap - COMB_NARROW) // COMB_ALIGN).astype(I32)
    wide_flag = jnp.any(stop > narrow_start * COMB_ALIGN + COMB_NARROW, axis=0)
    wide_start = jnp.where(wide_flag[None, :], jnp.minimum(base // COMB_ALIGN, (cap - COMB_WIN) // COMB_ALIGN), 0)
    return combine(h2d, pos_t, probs, ye, narrow_start, wide_start.astype(I32), wide_flag.astype(I32))


def _block_diag(w):
    h, d, _ = w.shape
    eye = jnp.eye(h, dtype=w.dtype)
    return jnp.einsum("hde,hg->hdge", w, eye).reshape(h * d, h * d)


def _prep_layer(l, p):
    wcat = jnp.stack([jnp.concatenate([_block_diag(p["lru_wa"][l, d]), _block_diag(p["lru_wx"][l, d])], axis=1)
                      for d in range(2)]).astype(BF16)
    bcat = jnp.stack([jnp.concatenate([p["lru_ba"][l, d], p["lru_bx"][l, d]])[None, :] for d in range(2)])
    cdec = (-LRU_C * jax.nn.softplus(-p["lru_lambda"][l]))[:, None, :]
    return dict(
        w_in=p["w_in"][l].astype(BF16), w_out=p["w_out"][l].astype(BF16),
        wcat=wcat, bcat=bcat, cdec=cdec,
        wg=p["w_gate"][l].astype(BF16), wu=p["w_up"][l].astype(BF16), wd=p["w_down"][l].astype(BF16))


def _trunk(x, p, prepped):
    bsz, s, _ = x.shape
    n = bsz * s
    x2d = x.reshape(n, D_MODEL)
    bias = _bias_tiles(p["rel_bias"], min(ATTN_TILE, s), min(ATTN_KEY_TILE, s))
    for l, w in enumerate(prepped):
        lam_init = 0.8 - 0.6 * math.exp(-0.3 * l)
        pieces = in_proj(x2d, p["ln1_g"][l], w["w_in"])
        lx, lgate, rq, rk, rv, rg, dq, dk, dv = [a.reshape(bsz, s, a.shape[1]) for a in pieces]
        y_lru = lru_mixer(lx, lgate, p["conv_w"][l], p["conv_b"][l].reshape(1, LRU_W), w["wcat"], w["bcat"],
                          w["cdec"], p["lru_norm_g"][l].reshape(1, LRU_W))
        y_ret = ret_mixer(rq, rk, rv, rg, p["ret_norm_g"][l])
        qn, kn, vv = attn_prep(dq, dk, dv, p["q_norm_g"][l], p["k_norm_g"][l])
        bound, spread = _score_bound(p["q_norm_g"][l], p["k_norm_g"][l], p["rel_bias"])
        y_diff = diff_attn(qn, kn, vv, bias, bound, spread, p["diff_lambda"][l], lam_init, p["diff_norm_g"][l])
        h2d, hn, probs_t = out_proj(x2d, y_lru.reshape(n, LRU_W), y_ret.reshape(n, RET_W),
                                    y_diff.reshape(n, DIFF_W), w["w_out"], p["ln2_g"][l], p["w_router"][l])
        x2d = ec_moe(h2d, hn, probs_t, w["wg"], w["wu"], w["wd"])
    return x2d.reshape(bsz, s, D_MODEL)


def kernel(x_prompt, x_sample, rel_bias, ln1_g, ln2_g, w_in, conv_w, conv_b, lru_wa, lru_ba, lru_wx, lru_bx,
           lru_lambda, lru_norm_g, ret_norm_g, q_norm_g, k_norm_g, diff_lambda, diff_norm_g, w_out, w_router,
           w_gate, w_up, w_down):
    p = dict(rel_bias=rel_bias, ln1_g=ln1_g, ln2_g=ln2_g, w_in=w_in, conv_w=conv_w, conv_b=conv_b,
             lru_wa=lru_wa, lru_ba=lru_ba, lru_wx=lru_wx, lru_bx=lru_bx, lru_lambda=lru_lambda,
             lru_norm_g=lru_norm_g, ret_norm_g=ret_norm_g, q_norm_g=q_norm_g, k_norm_g=k_norm_g,
             diff_lambda=diff_lambda, diff_norm_g=diff_norm_g, w_out=w_out, w_router=w_router,
             w_gate=w_gate, w_up=w_up, w_down=w_down)
    prepped = [_prep_layer(l, p) for l in range(w_in.shape[0])]
    return _trunk(x_prompt, p, prepped), _trunk(x_sample, p, prepped)
```

```python
import functools
import math

import numpy as np
import jax
import jax.numpy as jnp
from jax import lax
from jax.experimental import pallas as pl
from jax.experimental.pallas import tpu as pltpu

F32 = jnp.float32
BF16 = jnp.bfloat16
I32 = jnp.int32
HIGHEST = lax.Precision.HIGHEST

D_MODEL = 1024
HEAD_DIM = 64
LRU_W = 256
LRU_HEADS = 4
RET_W = 384
RET_HEADS = 6
DIFF_W = 384
DIFF_HEADS = 6
DIFF_HALF = 32
IN_SIZES = (LRU_W, LRU_W, RET_W, RET_W, RET_W, RET_W, DIFF_W, DIFF_W, DIFF_W)
IN_WIDTH = sum(IN_SIZES)
CONV_WIDTH = 4
LRU_C = 8.0
ROPE_BASE = 10000.0
NUM_BUCKETS = 32
MAX_DISTANCE = 128
N_EXPERTS = 16
EC_FACTOR = 2
D_FF = 2816
EPS = 1e-6

V7X_VMEM_BYTES = 64 * 1024 * 1024
SUBLANES = 8
LANES = 128

ROW_TILE = 512
SCAN_CHUNK = 256
RET_CHUNK = 256
ATTN_TILE = 512
ATTN_KEY_TILE = 512
SEL_ROW = 512
SLOT_BLOCK = 256
SLOT_INDEX_ROWS = 6
FF_CHUNK = 1408
COMB_TILE = 256
COMB_ALIGN = 16
COMB_WIN = COMB_TILE + COMB_ALIGN
COMB_NARROW = LANES


def _params(sem, vmem_mb=48):
    return pltpu.CompilerParams(dimension_semantics=sem,
                                vmem_limit_bytes=vmem_mb * 1024 * 1024)


def _full(shape):
    nd = len(shape)
    return pl.BlockSpec(shape, lambda *_: (0,) * nd)


def _rms(x, g):
    return x * lax.rsqrt(jnp.mean(x * x, axis=-1, keepdims=True) + EPS) * g


def _split_bf16(x):
    hi = x.astype(BF16)
    return hi, (x - hi.astype(F32)).astype(BF16)


def _group_sum(x, ones_bf16):
    hi, lo = _split_bf16(x)
    return (jnp.dot(hi, ones_bf16, preferred_element_type=F32)
            + jnp.dot(lo, ones_bf16, preferred_element_type=F32))


def _in_proj_body(x_ref, g_ref, w_ref, *o_refs):
    xn = _rms(x_ref[...], g_ref[...]).astype(BF16)
    off = 0
    for o_ref, width in zip(o_refs, IN_SIZES):
        o_ref[...] = jnp.dot(xn, w_ref[:, off:off + width], preferred_element_type=F32)
        off += width


def in_proj(x2d, g, w_bf16, tm=ROW_TILE):
    n = x2d.shape[0]
    tm = min(tm, n)
    return pl.pallas_call(
        _in_proj_body,
        grid=(n // tm,),
        in_specs=[pl.BlockSpec((tm, D_MODEL), lambda i: (i, 0)),
                  _full((1, D_MODEL)), _full((D_MODEL, IN_WIDTH))],
        out_specs=[pl.BlockSpec((tm, w), lambda i: (i, 0)) for w in IN_SIZES],
        out_shape=[jax.ShapeDtypeStruct((n, w), F32) for w in IN_SIZES],
        compiler_params=_params(("parallel",)),
        name="in_proj",
    )(x2d, g.reshape(1, D_MODEL), w_bf16)


def _shift_rows(ext, s, tc):
    n = ext.shape[0]
    return pltpu.roll(ext, (-s) % n, axis=0)[SUBLANES:SUBLANES + tc]


def _neg_expm1(y):
    series = -y * (1.0 + y * (1.0 / 2) * (1.0 + y * (1.0 / 3) * (1.0 + y * (1.0 / 4) * (1.0 + y * (1.0 / 5)))))
    return jnp.where(y > -1.0 / 64, series, 1.0 - jnp.exp(y))


def _lru_scan(a, b, rev):
    tc = a.shape[0]
    t = lax.broadcasted_iota(I32, a.shape, 0)
    d = 1
    while d < tc:
        if d % SUBLANES:
            if rev:
                keep = t < tc - d
                a_o = pltpu.roll(a, tc - d, axis=0)
                b_o = pltpu.roll(b, tc - d, axis=0)
            else:
                keep = t >= d
                a_o = pltpu.roll(a, d, axis=0)
                b_o = pltpu.roll(b, d, axis=0)
            b = jnp.where(keep, a * b_o + b, b)
            a = jnp.where(keep, a * a_o, a)
        elif rev:
            b = jnp.concatenate([a[:tc - d] * b[d:] + b[:tc - d], b[tc - d:]], axis=0)
            a = jnp.concatenate([a[:tc - d] * a[d:], a[tc - d:]], axis=0)
        else:
            b = jnp.concatenate([b[:d], a[d:] * b[:tc - d] + b[d:]], axis=0)
            a = jnp.concatenate([a[:d], a[d:] * a[:tc - d]], axis=0)
        d *= 2
    return a, b


def _lru_body(rev, *refs):
    if rev:
        (x_ref, xp_ref, xn_ref, gate_ref, hf_ref, cw_ref, cb_ref, w_ref, b_ref, c_ref,
         ng_ref, o_ref, carry_ref) = refs
    else:
        (x_ref, xp_ref, xn_ref, cw_ref, cb_ref, w_ref, b_ref, c_ref, o_ref, carry_ref) = refs
    step = pl.program_id(1)
    nc = pl.num_programs(1)
    ci = nc - 1 - step if rev else step

    @pl.when(step == 0)
    def _():
        carry_ref[...] = jnp.zeros_like(carry_ref)

    x = x_ref[0]
    tc = x.shape[0]
    prev = xp_ref[0] * (ci > 0).astype(F32)
    nxt = xn_ref[0] * (ci < nc - 1).astype(F32)
    ext = jnp.concatenate([prev, x, nxt], axis=0)
    xc = cb_ref[...] + sum(cw_ref[j:j + 1, :] * _shift_rows(ext, j - CONV_WIDTH // 2, tc)
                           for j in range(CONV_WIDTH))
    z = jnp.dot(xc.astype(BF16), w_ref[...], preferred_element_type=F32) + b_ref[...]
    r = jax.nn.sigmoid(z[:, :LRU_W])
    i = jax.nn.sigmoid(z[:, LRU_W:])
    log_a = c_ref[...] * r
    a = jnp.exp(log_a)
    b = jnp.sqrt(_neg_expm1(2.0 * log_a)) * (i * xc)
    a_cum, h_loc = _lru_scan(a, b, rev)
    h = h_loc + a_cum * carry_ref[0:1, :]
    carry_ref[0:1, :] = h[0:1, :] if rev else h[tc - 1:tc, :]
    if rev:
        y = (hf_ref[0] + h) * jax.nn.gelu(gate_ref[0])
        o_ref[0] = _rms(y, ng_ref[...])
    else:
        o_ref[0] = h


def lru_mixer(lx, lgate, cw, cb, wcat, bcat, cdec, ng, tc=SCAN_CHUNK):
    bsz, s, _ = lx.shape
    tc = min(tc, s)
    nc = s // tc
    r8 = tc // SUBLANES
    nb8 = s // SUBLANES

    def specs(rev):
        cmap = (lambda b, c: (b, nc - 1 - c, 0)) if rev else (lambda b, c: (b, c, 0))
        if rev:
            pmap = lambda b, c: (b, jnp.maximum((nc - 1 - c) * r8 - 1, 0), 0)
            nmap = lambda b, c: (b, jnp.minimum((nc - c) * r8, nb8 - 1), 0)
        else:
            pmap = lambda b, c: (b, jnp.maximum(c * r8 - 1, 0), 0)
            nmap = lambda b, c: (b, jnp.minimum((c + 1) * r8, nb8 - 1), 0)
        main = pl.BlockSpec((1, tc, LRU_W), cmap)
        halo = [pl.BlockSpec((1, SUBLANES, LRU_W), pmap), pl.BlockSpec((1, SUBLANES, LRU_W), nmap)]
        return main, halo

    common = [_full((CONV_WIDTH, LRU_W)), _full((1, LRU_W)), _full((LRU_W, 2 * LRU_W)),
              _full((1, 2 * LRU_W)), _full((1, LRU_W))]
    main, halo = specs(False)
    hf = pl.pallas_call(
        functools.partial(_lru_body, False),
        grid=(bsz, nc),
        in_specs=[main] + halo + common,
        out_specs=main,
        out_shape=jax.ShapeDtypeStruct((bsz, s, LRU_W), F32),
        scratch_shapes=[pltpu.VMEM((SUBLANES, LRU_W), F32)],
        compiler_params=_params(("parallel", "arbitrary")),
        name="lru_fwd",
    )(lx, lx, lx, cw, cb, wcat[0], bcat[0], cdec[0])
    main, halo = specs(True)
    return pl.pallas_call(
        functools.partial(_lru_body, True),
        grid=(bsz, nc),
        in_specs=[main] + halo + [main, main] + common + [_full((1, LRU_W))],
        out_specs=main,
        out_shape=jax.ShapeDtypeStruct((bsz, s, LRU_W), F32),
        scratch_shapes=[pltpu.VMEM((SUBLANES, LRU_W), F32)],
        compiler_params=_params(("parallel", "arbitrary")),
        name="lru_rev",
    )(lx, lx, lx, lgate, hf, cw, cb, wcat[1], bcat[1], cdec[1], ng)


def _ret_log_gamma():
    return np.log1p(-np.exp2(-5.0 - np.arange(RET_HEADS, dtype=np.float64)))


@functools.lru_cache(maxsize=None)
def _ret_tables(c):
    lg = np.repeat(_ret_log_gamma(), HEAD_DIM)[None, :]
    idx = np.arange(c, dtype=np.float64)[:, None]
    dec = np.stack([np.exp((idx + 1.0) * lg),
                    np.exp((c - 1.0 - idx) * lg),
                    np.exp((c - idx) * lg),
                    np.exp(idx * lg)])
    chunk = np.exp(c * lg)
    dist = np.abs(idx - idx.T)
    intra = np.exp(dist[None] * _ret_log_gamma()[:, None, None])
    lane_head = np.arange(RET_W) // HEAD_DIM
    bd = (lane_head[:, None] == lane_head[None, :]).astype(np.float32)
    return (dec.astype(np.float32), chunk.astype(np.float32), intra.astype(np.float32), bd)


def _rope_tables(s):
    half = HEAD_DIM // 2
    freqs = ROPE_BASE ** (-jnp.arange(half, dtype=F32) / half)
    ang = jnp.arange(s, dtype=F32)[:, None] * freqs[None, :]
    cos = jnp.cos(ang)
    sin = jnp.sin(ang)
    cos_t = jnp.tile(jnp.concatenate([cos, cos], axis=1), (1, RET_W // HEAD_DIM))
    sin_t = jnp.tile(jnp.concatenate([-sin, sin], axis=1), (1, RET_W // HEAD_DIM))
    return cos_t, sin_t


def _rope(x, cos, sin_signed):
    lane = lax.broadcasted_iota(I32, x.shape, 1)
    w = x.shape[1]
    half = HEAD_DIM // 2
    swapped = jnp.where(lane % HEAD_DIM < half,
                        pltpu.roll(x, w - half, axis=1), pltpu.roll(x, half, axis=1))
    return x * cos + swapped * sin_signed


def _ret_body(rev, *refs):
    if rev:
        (q_ref, k_ref, v_ref, cos_ref, sin_ref, dec_ref, chunk_ref, bd_ref,
         of_ref, g_ref, ng_ref, o_ref, state_ref) = refs
    else:
        (q_ref, k_ref, v_ref, cos_ref, sin_ref, dec_ref, chunk_ref, bd_ref,
         intra_ref, o_ref, state_ref) = refs

    @pl.when(pl.program_id(1) == 0)
    def _():
        state_ref[...] = jnp.zeros_like(state_ref)

    cos = cos_ref[...]
    sin = sin_ref[...]
    q = _rope(q_ref[0], cos, sin)
    k = _rope(k_ref[0], cos, sin) * (HEAD_DIM ** -0.5)
    vb = v_ref[0].astype(BF16)
    qd, kd = (2, 3) if rev else (0, 1)
    state = state_ref[...]
    cross = jnp.dot((q * dec_ref[qd]).astype(BF16), state.astype(BF16), preferred_element_type=F32)
    kv = lax.dot_general((k * dec_ref[kd]).astype(BF16), vb, (((0,), (0,)), ((), ())),
                         preferred_element_type=F32)
    state_ref[...] = state * chunk_ref[...] + kv * bd_ref[...]
    if rev:
        o = of_ref[0] + cross
        ms = _group_sum(o * o, bd_ref[...].astype(BF16)) * (1.0 / HEAD_DIM)
        o = o * lax.rsqrt(ms + EPS) * ng_ref[...]
        o_ref[0] = jax.nn.silu(g_ref[0]) * o
    else:
        v = v_ref[0]
        upper = lax.broadcasted_iota(I32, (1, LANES), 1) >= HEAD_DIM
        pairs = []
        for pr in range(RET_W // LANES):
            lanes = slice(pr * LANES, (pr + 1) * LANES)
            kb = k[:, lanes].astype(BF16)
            acc = jnp.zeros((q.shape[0], LANES), F32)
            for hh in range(2):
                keep = upper if hh else jnp.logical_not(upper)
                s = lax.dot_general(jnp.where(keep, q[:, lanes], 0.0).astype(BF16), kb, (((1,), (1,)), ((), ())),
                                    preferred_element_type=F32)
                s = (s * intra_ref[2 * pr + hh]).astype(BF16)
                acc = acc + jnp.dot(s, jnp.where(keep, v[:, lanes], 0.0).astype(BF16), preferred_element_type=F32)
            pairs.append(acc)
        o_ref[0] = cross + jnp.concatenate(pairs, axis=1)


def ret_mixer(rq, rk, rv, rg, ng, c=RET_CHUNK):
    bsz, s, _ = rq.shape
    c = min(c, s)
    nc = s // c
    dec, chunk, intra, bd = _ret_tables(c)
    cos_t, sin_t = _rope_tables(s)

    def specs(rev):
        cmap = (lambda b, i: (b, nc - 1 - i, 0)) if rev else (lambda b, i: (b, i, 0))
        tmap = (lambda b, i: (nc - 1 - i, 0)) if rev else (lambda b, i: (i, 0))
        main = pl.BlockSpec((1, c, RET_W), cmap)
        tab = pl.BlockSpec((c, RET_W), tmap)
        return main, tab

    consts = [_full((4, c, RET_W)), _full((1, RET_W)), _full((RET_W, RET_W))]
    main, tab = specs(False)
    of = pl.pallas_call(
        functools.partial(_ret_body, False),
        grid=(bsz, nc),
        in_specs=[main, main, main, tab, tab] + consts + [_full((RET_HEADS, c, c))],
        out_specs=main,
        out_shape=jax.ShapeDtypeStruct((bsz, s, RET_W), F32),
        scratch_shapes=[pltpu.VMEM((RET_W, RET_W), F32)],
        compiler_params=_params(("parallel", "arbitrary")),
        name="ret_fwd",
    )(rq, rk, rv, cos_t, sin_t, dec, chunk, bd, intra)
    main, tab = specs(True)
    return pl.pallas_call(
        functools.partial(_ret_body, True),
        grid=(bsz, nc),
        in_specs=[main, main, main, tab, tab] + consts + [main, main, _full((1, RET_W))],
        out_specs=main,
        out_shape=jax.ShapeDtypeStruct((bsz, s, RET_W), F32),
        scratch_shapes=[pltpu.VMEM((RET_W, RET_W), F32)],
        compiler_params=_params(("parallel", "arbitrary")),
        name="ret_rev",
    )(rq, rk, rv, cos_t, sin_t, dec, chunk, bd, of, rg, ng.reshape(1, RET_W))


HEAD_SLOT = 128
LOG2E = 1.4426950408889634


@functools.lru_cache(maxsize=None)
def _attn_consts():
    lane = np.arange(DIFF_W)
    grp = lane // DIFF_HALF
    bd32 = (grp[:, None] == grp[None, :]).astype(np.float32)
    place = np.zeros((DIFF_W, DIFF_HEADS * HEAD_SLOT), np.float32)
    place[lane, (lane // HEAD_DIM) * HEAD_SLOT + lane % HEAD_DIM] = 1.0
    ones_col = np.zeros((1, DIFF_HEADS * HEAD_SLOT), np.float32)
    ones_col[0, np.arange(DIFF_HEADS) * HEAD_SLOT + HEAD_DIM] = 1.0
    return bd32, place, ones_col


def _attn_prep_body(q_ref, k_ref, v_ref, qg_ref, kg_ref, bd_ref, place_ref, ones_ref,
                    qn_ref, kn_ref, vv_ref):
    def qk_norm(x, g):
        ms = _group_sum(x * x, bd_ref[...]) * (1.0 / DIFF_HALF)
        return x * lax.rsqrt(ms + EPS) * g

    qn_ref[0] = (qk_norm(q_ref[0], qg_ref[...]) * (DIFF_HALF ** -0.5 * LOG2E)).astype(BF16)
    kn_ref[0] = qk_norm(k_ref[0], kg_ref[...]).astype(BF16)
    vb = v_ref[0].astype(BF16)
    vv_ref[0] = (jnp.dot(vb, place_ref[...], preferred_element_type=F32) + ones_ref[...]).astype(BF16)


def attn_prep(dq, dk, dv, qg, kg, tc=ROW_TILE):
    bsz, s, _ = dq.shape
    tc = min(tc, s)
    bd32, place, ones_col = _attn_consts()
    wide = DIFF_HEADS * HEAD_SLOT
    main = pl.BlockSpec((1, tc, DIFF_W), lambda b, c: (b, c, 0))
    outb = pl.BlockSpec((1, tc, wide), lambda b, c: (b, c, 0))
    rep = DIFF_W // DIFF_HALF
    return pl.pallas_call(
        _attn_prep_body,
        grid=(bsz, s // tc),
        in_specs=[main, main, main, _full((1, DIFF_W)), _full((1, DIFF_W)),
                  _full((DIFF_W, DIFF_W)), _full((DIFF_W, wide)), _full((1, wide))],
        out_specs=[main, main, outb],
        out_shape=[jax.ShapeDtypeStruct((bsz, s, DIFF_W), BF16)] * 2 + [jax.ShapeDtypeStruct((bsz, s, wide), BF16)],
        compiler_params=_params(("parallel", "parallel")),
        name="attn_prep",
    )(dq, dk, dv, jnp.tile(qg, rep).reshape(1, DIFF_W), jnp.tile(kg, rep).reshape(1, DIFF_W),
      jnp.asarray(bd32, BF16), jnp.asarray(place, BF16), ones_col)


def _t5_bucket_np(rel):
    nb = NUM_BUCKETS // 2
    max_exact = nb // 2
    n = np.abs(rel)
    nf = np.maximum(n, 1).astype(np.float64)
    large = max_exact + np.floor(2.0 * np.log2(nf / max_exact)).astype(np.int64)
    large = np.minimum(large, nb - 1)
    return (np.where(rel > 0, nb, 0) + np.where(n < max_exact, n, large)).astype(np.int32)


def _bias_tiles(rel_bias, t, tk):
    period = t + tk
    x = np.arange(period)[None, :]
    d = np.arange(-(tk // t), 2)[:, None]
    diag = rel_bias[_t5_bucket_np(x - (t - 1) + d * t)]
    diag = jnp.transpose(diag, (2, 0, 1))
    hankel = jnp.tile(diag, (1, 1, t + 1))[:, :, :t * (period + 1)].reshape(DIFF_HEADS, d.shape[0], t, period + 1)
    near = hankel[:, :, ::-1, :tk]
    nb = NUM_BUCKETS // 2
    left = jnp.broadcast_to(rel_bias[nb - 1][:, None, None, None], (DIFF_HEADS, 1, t, tk))
    right = jnp.broadcast_to(rel_bias[NUM_BUCKETS - 1][:, None, None, None], (DIFF_HEADS, 1, t, tk))
    return jnp.concatenate([left, near, right], axis=1) * LOG2E


def _attn_body(online, q_ref, k_ref, v_ref, bias_ref, lam_ref, linit_ref, g_ref, o_ref):
    t = q_ref.shape[1]
    tk = bias_ref.shape[3]
    kq = tk // t
    nk = k_ref.shape[1] // tk
    qi = pl.program_id(2)
    lam = (jnp.exp(jnp.sum(lam_ref[0:1, :] * lam_ref[1:2, :], axis=1, keepdims=True))
           - jnp.exp(jnp.sum(lam_ref[2:3, :] * lam_ref[3:4, :], axis=1, keepdims=True))
           + linit_ref[...])
    lane = lax.broadcasted_iota(I32, (t, HEAD_SLOT), 1)
    qf = q_ref[0].astype(F32)
    half = lane // DIFF_HALF
    qs = [jnp.concatenate([jnp.where(half == 2 * hh + m, qf, 0.0) for m in range(2)], axis=0).astype(BF16)
          for hh in range(2)]

    def body(j, carry):
        rows = pl.ds(pl.multiple_of(j * tk, tk), tk)
        bidx = jnp.clip(j * kq - qi, -kq - 1, 2) + kq + 1
        kb = k_ref[0, rows, :]
        new = []
        for hh in range(2):
            lo = hh * HEAD_SLOT
            m_i, acc = carry[hh]
            vb = v_ref[0, rows, lo:lo + HEAD_SLOT]
            s = lax.dot_general(qs[hh], kb, (((1,), (1,)), ((), ())), preferred_element_type=F32)
            bt = bias_ref[hh, bidx]
            s = s + jnp.concatenate([bt, bt], axis=0)
            if online:
                m_new = jnp.maximum(m_i, jnp.max(s, axis=1, keepdims=True))
                p = jnp.exp2(s - m_new)
                acc = jnp.exp2(m_i - m_new) * acc
            else:
                m_new = m_i
                p = jnp.exp2(s)
            acc = acc + jnp.dot(p.astype(BF16), vb, preferred_element_type=F32)
            new.append((m_new, acc))
        return tuple(new)

    init = (jnp.full((2 * t, 1) if online else (1, 1), -1e30, F32), jnp.zeros((2 * t, HEAD_SLOT), F32))
    res = lax.fori_loop(0, nk, body, (init, init), unroll=1 if online else 2)

    outs = []
    for hh in range(2):
        acc = res[hh][1]
        sm = acc / acc[:, HEAD_DIM:HEAD_DIM + 1]
        o = sm[:t] - lam * sm[t:]
        o = jnp.where(lane < HEAD_DIM, o, 0.0)
        ms = jnp.sum(o * o, axis=1, keepdims=True) * (1.0 / HEAD_DIM)
        outs.append(o * lax.rsqrt(ms + EPS))
    both = jnp.where(lane < HEAD_DIM, outs[0], pltpu.roll(outs[1], HEAD_DIM, axis=1))
    o_ref[0] = both * g_ref[...] * (1.0 - linit_ref[...])


MAX_SCORE_RANGE = 96.0


def _score_bound(qg, kg, rel_bias):
    qk = DIFF_HALF * (DIFF_HALF ** -0.5 * LOG2E) * jnp.max(jnp.abs(qg)) * jnp.max(jnp.abs(kg)) * 1.02
    hi = jnp.max(rel_bias) * LOG2E
    lo = jnp.min(rel_bias) * LOG2E
    return qk + hi, 2.0 * qk + (hi - lo)


def diff_attn(qn, kn, vv, bias, bound, spread, lam_vecs, lam_init, ng):
    bsz, s, _ = qn.shape
    _, n_tiles, t, tk = bias.shape
    assert t >= MAX_DISTANCE and tk % t == 0
    qspec = pl.BlockSpec((1, t, 2 * HEAD_DIM), lambda b, h, i: (b, i, h))
    kspec = pl.BlockSpec((1, s, 2 * HEAD_DIM), lambda b, h, i: (b, 0, h))
    vspec = pl.BlockSpec((1, s, 2 * HEAD_SLOT), lambda b, h, i: (b, 0, h))
    lam_pad = jnp.zeros((4, LANES), F32).at[:, :DIFF_HALF].set(lam_vecs)
    linit = jnp.full((1, LANES), lam_init, F32)
    g2 = jnp.tile(ng, 2).reshape(1, LANES)

    def call(online, bias_tiles):
        return pl.pallas_call(
            functools.partial(_attn_body, online),
            grid=(bsz, DIFF_HEADS // 2, s // t),
            in_specs=[qspec, kspec, vspec,
                      pl.BlockSpec((2, n_tiles, t, tk), lambda b, h, i: (h, 0, 0, 0)),
                      _full((4, LANES)), _full((1, LANES)), _full((1, LANES))],
            out_specs=pl.BlockSpec((1, t, 2 * HEAD_DIM), lambda b, h, i: (b, i, h)),
            out_shape=jax.ShapeDtypeStruct((bsz, s, DIFF_W), F32),
            compiler_params=_params(("parallel", "parallel", "arbitrary")),
            name="attn_online" if online else "attn",
        )(qn, kn, vv, bias_tiles, lam_pad, linit, g2)

    return lax.cond(spread <= MAX_SCORE_RANGE,
                    lambda: call(False, bias - bound), lambda: call(True, bias))


def _out_proj_body(x_ref, yl_ref, yr_ref, yd_ref, wl_ref, wr_ref, wd_ref, g_ref, whi_ref, wlo_ref,
                   h_ref, hn_ref, pt_ref):
    h = (x_ref[...]
         + jnp.dot(yl_ref[...].astype(BF16), wl_ref[...], preferred_element_type=F32)
         + jnp.dot(yr_ref[...].astype(BF16), wr_ref[...], preferred_element_type=F32)
         + jnp.dot(yd_ref[...].astype(BF16), wd_ref[...], preferred_element_type=F32))
    h_ref[...] = h
    hn = _rms(h, g_ref[...])
    hn_ref[...] = hn
    hn_hi, hn_lo = _split_bf16(hn)
    nt = (((1,), (1,)), ((), ()))
    logits_t = (lax.dot_general(whi_ref[...], hn_hi, nt, preferred_element_type=F32)
                + lax.dot_general(wlo_ref[...], hn_hi, nt, preferred_element_type=F32)
                + lax.dot_general(whi_ref[...], hn_lo, nt, preferred_element_type=F32))
    et = jnp.exp(logits_t - jnp.max(logits_t, axis=0, keepdims=True))
    pt_ref[...] = et / jnp.sum(et, axis=0, keepdims=True)


def out_proj(x2d, yl, yr, yd, w_out_bf16, g, w_router, tm=ROW_TILE):
    n = x2d.shape[0]
    tm = min(tm, n)
    row = lambda w: pl.BlockSpec((tm, w), lambda i: (i, 0))
    wr_hi, wr_lo = _split_bf16(w_router.T)
    return pl.pallas_call(
        _out_proj_body,
        grid=(n // tm,),
        in_specs=[row(D_MODEL), row(LRU_W), row(RET_W), row(DIFF_W),
                  _full((LRU_W, D_MODEL)), _full((RET_W, D_MODEL)), _full((DIFF_W, D_MODEL)),
                  _full((1, D_MODEL)), _full((N_EXPERTS, D_MODEL)), _full((N_EXPERTS, D_MODEL))],
        out_specs=[row(D_MODEL), row(D_MODEL), pl.BlockSpec((N_EXPERTS, tm), lambda i: (0, i))],
        out_shape=[jax.ShapeDtypeStruct((n, D_MODEL), F32), jax.ShapeDtypeStruct((n, D_MODEL), F32),
                   jax.ShapeDtypeStruct((N_EXPERTS, n), F32)],
        compiler_params=_params(("parallel",)),
        name="out_proj",
    )(x2d, yl, yr, yd, w_out_bf16[:LRU_W], w_out_bf16[LRU_W:LRU_W + RET_W], w_out_bf16[LRU_W + RET_W:],
      g.reshape(1, D_MODEL), wr_hi, wr_lo)


def _row_cumsum(x01, tri_ref, nr):
    within = jnp.dot(x01.astype(F32).astype(BF16), tri_ref[...], preferred_element_type=F32).astype(I32)
    tot = jnp.broadcast_to(within[:, SEL_ROW - 1:SEL_ROW], (nr, LANES))
    r = lax.broadcasted_iota(I32, (nr, LANES), 0)
    inc = tot
    d = 1
    while d < nr:
        inc = inc + jnp.where(r >= d, pltpu.roll(inc, d, axis=0), 0)
        d *= 2
    return within + (inc - tot)[:, 0:1]


def _select_body(cap, p_ref, tri_ref, cnt_ref, sel_ref):
    p = p_ref[0]
    nr = p.shape[0]
    bits = pltpu.bitcast(p, I32)

    def body(i, prefix):
        cand = prefix | (jnp.int32(1) << (30 - i))
        cnt = jnp.sum((bits >= cand).astype(I32), keepdims=True)
        return jnp.where(cnt >= cap, cand, prefix)

    thr = lax.fori_loop(0, 31, body, jnp.zeros((1, 1), I32))
    gt = bits > thr
    eq = bits == thr
    need = cap - jnp.sum(gt.astype(I32), keepdims=True)
    eq01 = eq.astype(I32)
    rank_eq = _row_cumsum(eq01, tri_ref, nr) - eq01
    sel = jnp.where(gt, 1, jnp.where(eq & (rank_eq < need), 1, 0))
    sel_ref[0] = sel
    cnt_ref[0] = _row_cumsum(sel, tri_ref, nr)


def select(probs_t, cap):
    e, n = probs_t.shape
    nr = n // SEL_ROW
    tri = np.triu(np.ones((SEL_ROW, SEL_ROW), np.float32))
    blk = pl.BlockSpec((1, nr, SEL_ROW), lambda i: (i, 0, 0))
    return pl.pallas_call(
        functools.partial(_select_body, cap),
        grid=(e,),
        in_specs=[blk, _full((SEL_ROW, SEL_ROW))],
        out_specs=[blk, blk],
        out_shape=[jax.ShapeDtypeStruct((e, nr, SEL_ROW), I32)] * 2,
        compiler_params=_params(("parallel",)),
        name="select",
    )(probs_t.reshape(e, nr, SEL_ROW), jnp.asarray(tri, BF16))


def _slot_index_body(lo_ref, hi_ref, pos_ref, o_ref):
    e = pl.program_id(0)
    n_sb, parts, w = o_ref.shape[1:]
    tok = lax.broadcasted_iota(I32, (parts, SEL_ROW), 1)
    part = lax.broadcasted_iota(I32, (parts, SEL_ROW), 0)
    base = jnp.where(part == 0, tok // 256, jnp.where(part == 1, tok % 256, 0))
    slot0 = lax.broadcasted_iota(I32, (w, 1), 0)

    nr = pos_ref.shape[1]
    group = min(SLOT_INDEX_ROWS, nr)

    def visit(sb, v):
        first = lo_ref[e, sb] + v * group
        start = jnp.minimum(first, nr - group)
        slots = slot0 + sb * w
        onehots, payloads = [], []
        for k in range(group):
            r = start + k
            onehots.append(jnp.where(pos_ref[0, pl.ds(r, 1), :] == slots, 1.0, 0.0).astype(BF16))
            row_part = jnp.where(part == 2, r, base)
            payloads.append(jnp.where(r >= first, row_part, 0).astype(F32).astype(BF16))
        return lax.dot_general(jnp.concatenate(payloads, axis=1), jnp.concatenate(onehots, axis=1),
                               (((1,), (1,)), ((), ())), preferred_element_type=F32)

    together = 2 if n_sb % 2 == 0 else 1

    def blocks(i, carry):
        sbs = [i * together + j for j in range(together)]
        firsts = [visit(sb, 0) for sb in sbs]
        for sb, acc in zip(sbs, firsts):
            visits = (hi_ref[e, sb] - lo_ref[e, sb]) // group + 1
            o_ref[0, sb] = lax.fori_loop(1, visits, lambda v, a, sb=sb: a + visit(sb, v), acc)
        return carry

    lax.fori_loop(0, n_sb // together, blocks, 0)


def slot_index(posm, row_end, cap, w):
    e, nr, _ = posm.shape
    assert nr <= 256 and SEL_ROW == 512
    n_sb = cap // w
    edges = jnp.arange(n_sb + 1, dtype=I32) * w
    lo = jnp.sum(row_end[:, None, :] <= edges[None, :-1, None], axis=2)
    hi = jnp.minimum(jnp.sum(row_end[:, None, :] < edges[None, 1:, None], axis=2), nr - 1)
    parts = pl.pallas_call(
        _slot_index_body,
        grid_spec=pltpu.PrefetchScalarGridSpec(
            num_scalar_prefetch=2,
            grid=(e,),
            in_specs=[pl.BlockSpec((1, nr, SEL_ROW), lambda i, lo, hi: (i, 0, 0))],
            out_specs=pl.BlockSpec((1, n_sb, SUBLANES, w), lambda i, lo, hi: (i, 0, 0, 0)),
        ),
        out_shape=jax.ShapeDtypeStruct((e, n_sb, SUBLANES, w), F32),
        compiler_params=_params(("arbitrary",)),
        name="slot_index",
    )(lo.astype(I32), hi.astype(I32), posm)
    idx = parts[:, :, 2] * SEL_ROW + parts[:, :, 0] * 256 + parts[:, :, 1]
    return idx.astype(I32).reshape(e * n_sb, 1, w)


def _ffn_body(idx_ref, idx_next_ref, x_hbm, wg_ref, wu_ref, wd_ref, o_ref, xbuf, sem):
    n_sb = pl.num_programs(1)
    step = pl.program_id(0) * n_sb + pl.program_id(1)
    n_steps = pl.num_programs(0) * n_sb
    half = step % 2
    w = xbuf.shape[1]

    def row_copy(idx_block, r, dst_half):
        return pltpu.make_async_copy(x_hbm.at[pl.ds(idx_block[0, 0, r], 1)], xbuf.at[dst_half, pl.ds(r, 1)],
                                     sem.at[dst_half])

    def wait_block(dst_half):
        for r in range(w):
            pltpu.make_async_copy(x_hbm.at[pl.ds(0, 1)], xbuf.at[dst_half, pl.ds(r, 1)], sem.at[dst_half]).wait()

    @pl.when(step == 0)
    def _():
        def body(r, carry):
            row_copy(idx_ref, r, 0).start()
            return carry

        lax.fori_loop(0, w, body, 0, unroll=8)

    wait_block(half)
    for r in range(w):
        row_copy(idx_next_ref, r, 1 - half).start()

    xe = xbuf[half].astype(BF16)
    acc = jnp.zeros((w, D_MODEL), F32)
    for c0 in range(0, D_FF, FF_CHUNK):
        g = jnp.dot(xe, wg_ref[0, :, c0:c0 + FF_CHUNK], preferred_element_type=F32)
        u = jnp.dot(xe, wu_ref[0, :, c0:c0 + FF_CHUNK], preferred_element_type=F32)
        mid = (jax.nn.silu(g) * u).astype(BF16)
        acc = acc + jnp.dot(mid, wd_ref[0, c0:c0 + FF_CHUNK, :], preferred_element_type=F32)
    o_ref[0] = acc.astype(BF16)

    @pl.when(step + 1 == n_steps)
    def _():
        wait_block(1 - half)


def expert_ffn(hn, idx, wg, wu, wd, cap, w):
    e = wg.shape[0]
    n_sb = cap // w
    last = e * n_sb - 1
    smem_block = lambda index_map: pl.BlockSpec((1, 1, w), index_map, memory_space=pltpu.SMEM)
    return pl.pallas_call(
        _ffn_body,
        grid=(e, n_sb),
        in_specs=[
            smem_block(lambda i, s: (i * n_sb + s, 0, 0)),
            smem_block(lambda i, s: (jnp.minimum(i * n_sb + s + 1, last), 0, 0)),
            pl.BlockSpec(memory_space=pl.ANY),
            pl.BlockSpec((1, D_MODEL, D_FF), lambda i, s: (i, 0, 0)),
            pl.BlockSpec((1, D_MODEL, D_FF), lambda i, s: (i, 0, 0)),
            pl.BlockSpec((1, D_FF, D_MODEL), lambda i, s: (i, 0, 0)),
        ],
        out_specs=pl.BlockSpec((1, w, D_MODEL), lambda i, s: (i, s, 0)),
        out_shape=jax.ShapeDtypeStruct((e, cap, D_MODEL), BF16),
        scratch_shapes=[pltpu.VMEM((2, w, D_MODEL), F32), pltpu.SemaphoreType.DMA((2,))],
        compiler_params=_params(("arbitrary", "arbitrary"), vmem_mb=58),
        name="ffn",
    )(idx, idx, hn, wg, wu, wd)


def _combine_body(ns_ref, ws_ref, wide_ref, h_ref, pos_ref, gate_ref, *refs):
    narrow_refs = refs[:N_EXPERTS]
    wide_refs = refs[N_EXPERTS:2 * N_EXPERTS]
    o_ref = refs[2 * N_EXPERTS]
    i = pl.program_id(0)
    pos = pos_ref[...]
    gate = gate_ref[...]
    wide = wide_ref[i]

    @pl.when(wide == 0)
    def _():
        lane = lax.broadcasted_iota(I32, (1, COMB_NARROW), 1)
        blocks = []
        for e in range(N_EXPERTS):
            slots = ns_ref[e, i] * COMB_ALIGN + lane
            blocks.append(jnp.where(pos[:, e:e + 1] == slots, gate[:, e:e + 1], 0.0).astype(BF16))
        g_all = jnp.concatenate(blocks, axis=1)
        rows = jnp.concatenate([r[...] for r in narrow_refs], axis=0)
        o_ref[...] = h_ref[...] + jnp.dot(g_all, rows, preferred_element_type=F32)

    @pl.when(wide != 0)
    def _():
        acc = h_ref[...]
        lane = lax.broadcasted_iota(I32, (1, COMB_WIN), 1)
        for e in range(N_EXPERTS):
            slots = ws_ref[e, i] * COMB_ALIGN + lane
            g = jnp.where(pos[:, e:e + 1] == slots, gate[:, e:e + 1], 0.0).astype(BF16)
            acc = acc + jnp.dot(g, wide_refs[e][...], preferred_element_type=F32)
        o_ref[...] = acc


def combine(h2d, pos_t, gates, ye, narrow_start, wide_start, wide_flag):
    n = h2d.shape[0]
    t = COMB_TILE
    row = lambda w: pl.BlockSpec((t, w), lambda i, ns, ws, fl: (i, 0))

    def narrow_spec(e):
        return pl.BlockSpec((None, pl.Element(COMB_NARROW), pl.Element(D_MODEL)),
                            lambda i, ns, ws, fl: (e, ns[e, i] * COMB_ALIGN, 0))

    def wide_spec(e):
        return pl.BlockSpec((None, pl.Element(COMB_WIN), pl.Element(D_MODEL)),
                            lambda i, ns, ws, fl: (e, ws[e, i] * COMB_ALIGN, 0))

    grid_spec = pltpu.PrefetchScalarGridSpec(
        num_scalar_prefetch=3,
        grid=(n // t,),
        in_specs=([row(D_MODEL), row(N_EXPERTS), row(N_EXPERTS)]
                  + [narrow_spec(e) for e in range(N_EXPERTS)] + [wide_spec(e) for e in range(N_EXPERTS)]),
        out_specs=row(D_MODEL),
    )
    return pl.pallas_call(
        _combine_body,
        grid_spec=grid_spec,
        out_shape=jax.ShapeDtypeStruct((n, D_MODEL), F32),
        compiler_params=_params(("arbitrary",)),
        name="combine",
    )(narrow_start, wide_start, wide_flag, h2d, pos_t, gates, *([ye] * (2 * N_EXPERTS)))


def ec_moe(h2d, hn, probs_t, wg, wu, wd):
    n = h2d.shape[0]
    probs = probs_t.T
    cap = EC_FACTOR * n // N_EXPERTS
    cnt, sel = select(probs_t, cap)
    posm = jnp.where(sel > 0, cnt - 1, -1)
    row_end = cnt[:, :, SEL_ROW - 1]
    w = min(SLOT_BLOCK, cap)
    ye = expert_ffn(hn, slot_index(posm, row_end, cap, w), wg, wu, wd, cap, w)
    pos_t = posm.reshape(N_EXPERTS, n).T
    base = (cnt - sel).reshape(N_EXPERTS, n)[:, ::COMB_TILE]
    stop = jnp.concatenate([base[:, 1:], jnp.full((N_EXPERTS, 1), cap, I32)], axis=1)
    narrow_start = jnp.minimum(base // COMB_ALIGN, (cap - COMB_NARROW) // COMB_ALIGN).astype(I32)
    wide_flag = jnp.any(stop > narrow_start * COMB_ALIGN + COMB_NARROW, axis=0)
    wide_start = jnp.where(wide_flag[None, :], jnp.minimum(base // COMB_ALIGN, (cap - COMB_WIN) // COMB_ALIGN), 0)
    return combine(h2d, pos_t, probs, ye, narrow_start, wide_start.astype(I32), wide_flag.astype(I32))


def _block_diag(w):
    h, d, _ = w.shape
    eye = jnp.eye(h, dtype=w.dtype)
    return jnp.einsum("hde,hg->hdge", w, eye).reshape(h * d, h * d)


def _prep_layer(l, p):
    wcat = jnp.stack([jnp.concatenate([_block_diag(p["lru_wa"][l, d]), _block_diag(p["lru_wx"][l, d])], axis=1)
                      for d in range(2)]).astype(BF16)
    bcat = jnp.stack([jnp.concatenate([p["lru_ba"][l, d], p["lru_bx"][l, d]])[None, :] for d in range(2)])
    cdec = (-LRU_C * jax.nn.softplus(-p["lru_lambda"][l]))[:, None, :]
    return dict(
        w_in=p["w_in"][l].astype(BF16), w_out=p["w_out"][l].astype(BF16),
        wcat=wcat, bcat=bcat, cdec=cdec,
        wg=p["w_gate"][l].astype(BF16), wu=p["w_up"][l].astype(BF16), wd=p["w_down"][l].astype(BF16))


def _trunk(x, p, prepped):
    bsz, s, _ = x.shape
    n = bsz * s
    x2d = x.reshape(n, D_MODEL)
    bias = _bias_tiles(p["rel_bias"], min(ATTN_TILE, s), min(ATTN_KEY_TILE, s))
    for l, w in enumerate(prepped):
        lam_init = 0.8 - 0.6 * math.exp(-0.3 * l)
        pieces = in_proj(x2d, p["ln1_g"][l], w["w_in"])
        lx, lgate, rq, rk, rv, rg, dq, dk, dv = [a.reshape(bsz, s, a.shape[1]) for a in pieces]
        y_lru = lru_mixer(lx, lgate, p["conv_w"][l], p["conv_b"][l].reshape(1, LRU_W), w["wcat"], w["bcat"],
                          w["cdec"], p["lru_norm_g"][l].reshape(1, LRU_W))
        y_ret = ret_mixer(rq, rk, rv, rg, p["ret_norm_g"][l])
        qn, kn, vv = attn_prep(dq, dk, dv, p["q_norm_g"][l], p["k_norm_g"][l])
        bound, spread = _score_bound(p["q_norm_g"][l], p["k_norm_g"][l], p["rel_bias"])
        y_diff = diff_attn(qn, kn, vv, bias, bound, spread, p["diff_lambda"][l], lam_init, p["diff_norm_g"][l])
        h2d, hn, probs_t = out_proj(x2d, y_lru.reshape(n, LRU_W), y_ret.reshape(n, RET_W),
                                    y_diff.reshape(n, DIFF_W), w["w_out"], p["ln2_g"][l], p["w_router"][l])
        x2d = ec_moe(h2d, hn, probs_t, w["wg"], w["wu"], w["wd"])
    return x2d.reshape(bsz, s, D_MODEL)


def kernel(x_prompt, x_sample, rel_bias, ln1_g, ln2_g, w_in, conv_w, conv_b, lru_wa, lru_ba, lru_wx, lru_bx,
           lru_lambda, lru_norm_g, ret_norm_g, q_norm_g, k_norm_g, diff_lambda, diff_norm_g, w_out, w_router,
           w_gate, w_up, w_down):
    p = dict(rel_bias=rel_bias, ln1_g=ln1_g, ln2_g=ln2_g, w_in=w_in, conv_w=conv_w, conv_b=conv_b,
             lru_wa=lru_wa, lru_ba=lru_ba, lru_wx=lru_wx, lru_bx=lru_bx, lru_lambda=lru_lambda,
             lru_norm_g=lru_norm_g, ret_norm_g=ret_norm_g, q_norm_g=q_norm_g, k_norm_g=k_norm_g,
             diff_lambda=diff_lambda, diff_norm_g=diff_norm_g, w_out=w_out, w_router=w_router,
             w_gate=w_gate, w_up=w_up, w_down=w_down)
    prepped = [_prep_layer(l, p) for l in range(w_in.shape[0])]
    return _trunk(x_prompt, p, prepped), _trunk(x_sample, p, prepped)
```
